```python
import math
import jax, jax.numpy as jnp
from jax import lax
import numpy as np

D_MODEL = 1024
BATCH = 4
SEQ = 4096
DEPTH = 1
DEC_BATCH = 32
DEC_SEQ = 1
PAST_LEN = 16384
PAGE_SIZE = 128

HEAD_DIM = 64
NSA_WIDTH = D_MODEL // 2
NSA_HEADS = NSA_WIDTH // HEAD_DIM
NSA_GROUPS = 2
NSA_REP = NSA_HEADS // NSA_GROUPS
CMP_BLOCK = 32
CMP_STRIDE = 16
CMP_HIDDEN = 128
SEL_BLOCK = 64
SEL_TOPN = 16
WINDOW = 512
FORCE_BONUS = 1.0e4
DIFF_WIDTH = D_MODEL - NSA_WIDTH
DIFF_VDIM = 2 * HEAD_DIM
DIFF_HEADS = DIFF_WIDTH // DIFF_VDIM
MIX_WIDTH = NSA_WIDTH + DIFF_WIDTH
D_FF = ((8 * D_MODEL // 3 + 255) // 256) * 256
CONV_W = 3
ROPE_THETA = 10000.0
Q_BLOCK = 128
RMS_EPS = 1e-6
ATTN_SCALE = 1.0 / math.sqrt(HEAD_DIM)

KV_W = NSA_GROUPS * HEAD_DIM
SPLITS = (NSA_WIDTH, KV_W, KV_W, KV_W, KV_W, KV_W, KV_W, NSA_HEADS * 3,
          DIFF_HEADS * 2 * HEAD_DIM, DIFF_HEADS * 2 * HEAD_DIM, DIFF_WIDTH)
IN_WIDTH = NSA_WIDTH + 6 * KV_W + NSA_HEADS * 3 + 2 * DIFF_HEADS * 2 * HEAD_DIM + DIFF_WIDTH

kernel_name = "nsa_diffattn_parallel_convffn_step"


def _rmsnorm(x, g):
    xf = x.astype(jnp.float32)
    y = xf * lax.rsqrt(jnp.mean(xf * xf, axis=-1, keepdims=True) + RMS_EPS)
    return (y * g.astype(jnp.float32)).astype(x.dtype)


def _rope(x, pos):
    half = HEAD_DIM // 2
    inv = 1.0 / (ROPE_THETA ** (jnp.arange(half, dtype=jnp.float32) / half))
    ang = pos.astype(jnp.float32)[:, None] * inv[None, :]
    shape = (1, pos.shape[0]) + (1,) * (x.ndim - 3) + (half,)
    cos = jnp.cos(ang).reshape(shape)
    sin = jnp.sin(ang).reshape(shape)
    xf = x.astype(jnp.float32)
    x1, x2 = xf[..., :half], xf[..., half:]
    return jnp.concatenate([x1 * cos - x2 * sin, x2 * cos + x1 * sin], axis=-1).astype(x.dtype)


def _masked_softmax(s, mask):
    s = jnp.where(mask, s, -jnp.inf)
    m = jnp.max(s, axis=-1, keepdims=True)
    m = jnp.where(jnp.isfinite(m), m, 0.0)
    p = jnp.exp(s - m)
    return p / jnp.maximum(jnp.sum(p, axis=-1, keepdims=True), 1e-30)


def _project(x, pos, norm_g, w_in):
    B, T = x.shape[:2]
    h = _rmsnorm(x, norm_g)
    z = h @ w_in
    cuts = [int(c) for c in np.cumsum(SPLITS)[:-1]]
    q, ck, cv, sk, sv, wk, wv, gate, dq, dk, dv = jnp.split(z, cuts, axis=-1)
    kv = lambda a: a.reshape(B, T, NSA_GROUPS, HEAD_DIM)
    q = q.reshape(B, T, NSA_GROUPS, NSA_REP, HEAD_DIM)
    q_rot = _rope(q, pos)
    gate = jax.nn.sigmoid(gate.astype(jnp.float32)).reshape(B, T, NSA_GROUPS, NSA_REP, 3)
    dq = _rope(dq.reshape(B, T, DIFF_HEADS, 2, HEAD_DIM), pos)
    dk = _rope(dk.reshape(B, T, DIFF_HEADS, 2, HEAD_DIM), pos)
    dv = dv.reshape(B, T, DIFF_HEADS, DIFF_VDIM)
    return (q, q_rot, kv(ck), kv(cv), _rope(kv(sk), pos), kv(sv), _rope(kv(wk), pos), kv(wv),
            gate, dq, dk, dv)


def _compress(x_raw, pos_emb, w1, w2):
    B, T, G, d = x_raw.shape
    r = CMP_BLOCK // CMP_STRIDE
    n = T // CMP_STRIDE - r + 1
    c = x_raw.reshape(B, T // CMP_STRIDE, CMP_STRIDE, G, d)
    pos_r = pos_emb.reshape(r, CMP_STRIDE, d)
    w1_r = w1.reshape(r, CMP_STRIDE, d, CMP_HIDDEN)
    hid = 0.0
    for i in range(r):
        hid = hid + jnp.einsum('bnsgd,sdh->bngh', c[:, i:i + n] + pos_r[i][None, None, :, None, :], w1_r[i])
    return jax.nn.gelu(hid) @ w2


def _nsa_branches(q, q_rot, gate, q_pos, kc, vc, cmp_end, n_sel, sel_gather, kw, vw, w_pos):
    B, Q = q.shape[:2]
    s = jnp.einsum('bqgrd,bkgd->bgrqk', q, kc, preferred_element_type=jnp.float32) * ATTN_SCALE
    p = _masked_softmax(s, cmp_end[None, :] <= q_pos[:, None])
    o_c = jnp.einsum('bgrqk,bkgd->bqgrd', p, vc)
    imp = jnp.sum(p, axis=2)
    n_cmp = kc.shape[1]
    cs = jnp.arange(n_cmp, dtype=jnp.int32) * CMP_STRIDE
    ss = jnp.arange(n_sel, dtype=jnp.int32) * SEL_BLOCK
    ov = ((cs[:, None] < ss[None, :] + SEL_BLOCK) & (cs[:, None] + CMP_BLOCK > ss[None, :])).astype(jnp.float32)
    score = jnp.einsum('bgqi,ij->bgqj', imp, ov)
    blk = jnp.arange(n_sel, dtype=jnp.int32)[None, :]
    cur = (q_pos // SEL_BLOCK)[:, None]
    forced = (blk == 0) | (blk == cur) | (blk == cur - 1)
    ok = blk * SEL_BLOCK <= q_pos[:, None]
    score = jnp.where(ok, score + jnp.where(forced, FORCE_BONUS, 0.0), -jnp.inf)
    top_vals, top_idx = lax.top_k(score, min(SEL_TOPN, n_sel))
    tok = top_idx[..., None] * SEL_BLOCK + jnp.arange(SEL_BLOCK, dtype=jnp.int32)
    kmask = (jnp.isfinite(top_vals)[..., None] & (tok <= q_pos[None, None, :, None, None]))
    kmask = kmask.reshape(B, NSA_GROUPS, Q, -1)
    tok = tok.reshape(B, NSA_GROUPS, Q, -1)
    ks, vs = sel_gather(tok)
    s = jnp.einsum('bqgrd,bgqnd->bgrqn', q_rot, ks, preferred_element_type=jnp.float32) * ATTN_SCALE
    p = _masked_softmax(s, kmask[:, :, None])
    o_s = jnp.einsum('bgrqn,bgqnd->bqgrd', p, vs)
    s = jnp.einsum('bqgrd,bkgd->bgrqk', q_rot, kw, preferred_element_type=jnp.float32) * ATTN_SCALE
    wm = ((w_pos[None, :] <= q_pos[:, None]) & (w_pos[None, :] > q_pos[:, None] - WINDOW)
          & (w_pos[None, :] >= 0))
    p = _masked_softmax(s, wm)
    o_w = jnp.einsum('bgrqk,bkgd->bqgrd', p, vw)
    o = gate[..., 0:1] * o_c + gate[..., 1:2] * o_s + gate[..., 2:3] * o_w
    return o.reshape(B, Q, NSA_WIDTH)


def _nsa_prompt(q, q_rot, gate, ck, cv, sk, sv, wk, wv, pk, w1k, w2k, pv, w1v, w2v):
    B, T = q.shape[:2]
    kc = _compress(ck, pk, w1k, w2k)
    vc = _compress(cv, pv, w1v, w2v)
    cmp_end = jnp.arange(kc.shape[1], dtype=jnp.int32) * CMP_STRIDE + CMP_BLOCK - 1
    n_sel = T // SEL_BLOCK
    pad = ((0, 0), (WINDOW, 0), (0, 0), (0, 0))
    wk_pad, wv_pad = jnp.pad(wk, pad), jnp.pad(wv, pad)
    b_idx = jnp.arange(B)[:, None, None, None]
    g_idx = jnp.arange(NSA_GROUPS)[None, :, None, None]

    def sel_gather(tok):
        return sk[b_idx, tok, g_idx], sv[b_idx, tok, g_idx]

    def block(c):
        start = c * Q_BLOCK
        sl = lambda a: lax.dynamic_slice_in_dim(a, start, Q_BLOCK, axis=1)
        q_pos = start + jnp.arange(Q_BLOCK, dtype=jnp.int32)
        kw = lax.dynamic_slice_in_dim(wk_pad, start, Q_BLOCK + WINDOW, axis=1)
        vw = lax.dynamic_slice_in_dim(wv_pad, start, Q_BLOCK + WINDOW, axis=1)
        w_pos = start - WINDOW + jnp.arange(Q_BLOCK + WINDOW, dtype=jnp.int32)
        return _nsa_branches(sl(q), sl(q_rot), sl(gate), q_pos, kc, vc, cmp_end, n_sel,
                             sel_gather, kw, vw, w_pos)

    out = lax.map(block, jnp.arange(T // Q_BLOCK, dtype=jnp.int32))
    return out.transpose(1, 0, 2, 3).reshape(B, T, NSA_WIDTH)


def _nsa_sample(q, q_rot, gate, ck, cv, sk, sv, wk, wv, page_table, pool_ck, pool_cv,
                pool_sk, pool_sv, win_k, win_v, pk, w1k, w2k, pv, w1v, w2v):
    B, Q = q.shape[:2]
    P = PAST_LEN
    q_pos = P + jnp.arange(Q, dtype=jnp.int32)
    t_full = P + Q
    t_pad = -(-t_full // SEL_BLOCK) * SEL_BLOCK
    padw = ((0, 0), (0, t_pad - t_full), (0, 0), (0, 0))
    past_ck = pool_ck[page_table].reshape(B, P, NSA_GROUPS, HEAD_DIM)
    past_cv = pool_cv[page_table].reshape(B, P, NSA_GROUPS, HEAD_DIM)
    kc = _compress(jnp.pad(jnp.concatenate([past_ck, ck], axis=1), padw), pk, w1k, w2k)
    vc = _compress(jnp.pad(jnp.concatenate([past_cv, cv], axis=1), padw), pv, w1v, w2v)
    cmp_end = jnp.arange(kc.shape[1], dtype=jnp.int32) * CMP_STRIDE + CMP_BLOCK - 1
    n_sel = t_pad // SEL_BLOCK
    b_idx = jnp.arange(B)[:, None, None, None]
    g_idx = jnp.arange(NSA_GROUPS)[None, :, None, None]

    def sel_gather(tok):
        in_past = (tok < P)[..., None]
        tp = jnp.minimum(tok, P - 1)
        phys = page_table[b_idx, tp // PAGE_SIZE]
        off = tp % PAGE_SIZE
        tn = jnp.clip(tok - P, 0, Q - 1)
        kg = jnp.where(in_past, pool_sk[phys, off, g_idx], sk[b_idx, tn, g_idx])
        vg = jnp.where(in_past, pool_sv[phys, off, g_idx], sv[b_idx, tn, g_idx])
        return kg, vg

    wbuf = win_k.shape[1]
    kw = jnp.concatenate([win_k, wk], axis=1)
    vw = jnp.concatenate([win_v, wv], axis=1)
    w_pos = P - wbuf + jnp.arange(wbuf + Q, dtype=jnp.int32)
    o = _nsa_branches(q, q_rot, gate, q_pos, kc, vc, cmp_end, n_sel, sel_gather, kw, vw, w_pos)
    return o, kw[:, -wbuf:], vw[:, -wbuf:]


def _diff_prompt(dq, dk, dv):
    B, T = dq.shape[:2]
    k_pos = jnp.arange(T, dtype=jnp.int32)

    def block(c):
        start = c * Q_BLOCK
        qb = lax.dynamic_slice_in_dim(dq, start, Q_BLOCK, axis=1)
        q_pos = start + jnp.arange(Q_BLOCK, dtype=jnp.int32)
        s = jnp.einsum('bqhmd,bkhmd->bhmqk', qb, dk, preferred_element_type=jnp.float32) * ATTN_SCALE
        s = jnp.where(k_pos[None, :] <= q_pos[:, None], s, -jnp.inf)
        p = jax.nn.softmax(s, axis=-1)
        return jnp.einsum('bhmqk,bkhe->bqhme', p, dv)

    o = lax.map(block, jnp.arange(T // Q_BLOCK, dtype=jnp.int32))
    return o.transpose(1, 0, 2, 3, 4, 5).reshape(B, T, DIFF_HEADS, 2, DIFF_VDIM)


def _attn_partial(q, k, v, mask):
    s = jnp.einsum('bqhmd,bkhmd->bhmqk', q, k, preferred_element_type=jnp.float32) * ATTN_SCALE
    s = jnp.where(mask, s, -jnp.inf)
    m = jnp.max(s, axis=-1)
    p = jnp.exp(s - jnp.where(jnp.isfinite(m), m, 0.0)[..., None])
    return m, jnp.sum(p, axis=-1), jnp.einsum('bhmqk,bkhe->bhmqe', p, v)


def _diff_sample(dq, dk, dv, page_table, pool_k, pool_v):
    B, Q = dq.shape[:2]
    P = PAST_LEN
    q_pos = P + jnp.arange(Q, dtype=jnp.int32)
    full = jnp.ones((Q, PAGE_SIZE), dtype=bool)

    def page(n):
        phys = page_table[:, n]
        return _attn_partial(dq, pool_k[phys], pool_v[phys], full)

    m_p, l_p, a_p = lax.map(page, jnp.arange(P // PAGE_SIZE, dtype=jnp.int32))
    m_n, l_n, a_n = _attn_partial(dq, dk, dv, q_pos[None, :] <= q_pos[:, None])
    m_all = jnp.concatenate([m_p, m_n[None]], axis=0)
    l_all = jnp.concatenate([l_p, l_n[None]], axis=0)
    a_all = jnp.concatenate([a_p, a_n[None]], axis=0)
    w = jnp.exp(m_all - jnp.max(m_all, axis=0, keepdims=True))
    o = jnp.sum(w[..., None] * a_all, axis=0) / jnp.sum(w * l_all, axis=0)[..., None]
    return o.transpose(0, 3, 1, 2, 4)


def _diff_merge(o, lam, lam_init, subln_g):
    B, T = o.shape[:2]
    od = o[..., 0, :] - lam * o[..., 1, :]
    od = _rmsnorm(od.astype(jnp.float32), subln_g) * (1.0 - lam_init)
    return od.reshape(B, T, DIFF_WIDTH)


def _conv_ffn(x, prev, norm_g, w_up, conv_w, conv_b, w_down):
    T = x.shape[1]
    u = _rmsnorm(x, norm_g) @ w_up
    ext = jnp.concatenate([prev, u], axis=1)
    c = conv_b
    for i in range(CONV_W):
        c = c + ext[:, i:i + T] * conv_w[i]
    a, b = jnp.split(c, 2, axis=-1)
    return (jax.nn.silu(a) * b) @ w_down, ext[:, -(CONV_W - 1):]


def setup_inputs(seed: int = 0) -> dict:
    key = jax.random.key(seed)
    ks = iter(jax.random.split(key, 40))
    f32 = jnp.float32
    nrm = lambda shape, scale: jax.random.normal(next(ks), shape, f32) * scale
    n_pages = PAST_LEN // PAGE_SIZE
    n_pool = (5 * DEC_BATCH * n_pages) // 4
    wbuf = min(WINDOW, PAST_LEN)
    page_table = jax.random.permutation(next(ks), n_pool)[:DEC_BATCH * n_pages]
    page_table = page_table.reshape(DEC_BATCH, n_pages).astype(jnp.int32)
    kvp = (DEPTH, n_pool, PAGE_SIZE, NSA_GROUPS, HEAD_DIM)
    return {
        "x_prompt": nrm((BATCH, SEQ, D_MODEL), 1.0),
        "x_sample": nrm((DEC_BATCH, DEC_SEQ, D_MODEL), 1.0),
        "cache_cmp_k": nrm(kvp, 1.0),
        "cache_cmp_v": nrm(kvp, 1.0),
        "cache_sel_k": nrm(kvp, 1.0),
        "cache_sel_v": nrm(kvp, 1.0),
        "cache_diff_k": nrm((DEPTH, n_pool, PAGE_SIZE, DIFF_HEADS, 2, HEAD_DIM), 1.0),
        "cache_diff_v": nrm((DEPTH, n_pool, PAGE_SIZE, DIFF_HEADS, DIFF_VDIM), 1.0),
        "cache_win_k": nrm((DEPTH, DEC_BATCH, wbuf, NSA_GROUPS, HEAD_DIM), 1.0),
        "cache_win_v": nrm((DEPTH, DEC_BATCH, wbuf, NSA_GROUPS, HEAD_DIM), 1.0),
        "state_ffn_conv": nrm((DEPTH, DEC_BATCH, CONV_W - 1, 2 * D_FF), 1.0),
        "page_table": page_table,
        "attn_norm": 1.0 + nrm((DEPTH, D_MODEL), 0.01),
        "w_in": nrm((DEPTH, D_MODEL, IN_WIDTH), D_MODEL ** -0.5),
        "cmp_pos_k": nrm((DEPTH, CMP_BLOCK, HEAD_DIM), 0.1),
        "cmp_w1_k": nrm((DEPTH, CMP_BLOCK * HEAD_DIM, CMP_HIDDEN), (CMP_BLOCK * HEAD_DIM) ** -0.5),
        "cmp_w2_k": nrm((DEPTH, CMP_HIDDEN, HEAD_DIM), CMP_HIDDEN ** -0.5),
        "cmp_pos_v": nrm((DEPTH, CMP_BLOCK, HEAD_DIM), 0.1),
        "cmp_w1_v": nrm((DEPTH, CMP_BLOCK * HEAD_DIM, CMP_HIDDEN), (CMP_BLOCK * HEAD_DIM) ** -0.5),
        "cmp_w2_v": nrm((DEPTH, CMP_HIDDEN, HEAD_DIM), CMP_HIDDEN ** -0.5),
        "lambda_q1": nrm((DEPTH, HEAD_DIM), 0.1),
        "lambda_k1": nrm((DEPTH, HEAD_DIM), 0.1),
        "lambda_q2": nrm((DEPTH, HEAD_DIM), 0.1),
        "lambda_k2": nrm((DEPTH, HEAD_DIM), 0.1),
        "subln_g": 1.0 + nrm((DEPTH, DIFF_VDIM), 0.01),
        "w_out": nrm((DEPTH, MIX_WIDTH, D_MODEL), MIX_WIDTH ** -0.5),
        "ffn_norm": 1.0 + nrm((DEPTH, D_MODEL), 0.01),
        "w_up": nrm((DEPTH, D_MODEL, 2 * D_FF), D_MODEL ** -0.5),
        "conv_w": nrm((DEPTH, CONV_W, 2 * D_FF), CONV_W ** -0.5),
        "conv_b": nrm((DEPTH, 2 * D_FF), 0.01),
        "w_down": nrm((DEPTH, D_FF, D_MODEL), D_FF ** -0.5),
        "final_norm": 1.0 + nrm((D_MODEL,), 0.01),
    }


def reference(x_prompt, x_sample, cache_cmp_k, cache_cmp_v, cache_sel_k, cache_sel_v,
              cache_diff_k, cache_diff_v, cache_win_k, cache_win_v, state_ffn_conv, page_table,
              attn_norm, w_in, cmp_pos_k, cmp_w1_k, cmp_w2_k, cmp_pos_v, cmp_w1_v, cmp_w2_v,
              lambda_q1, lambda_k1, lambda_q2, lambda_k2, subln_g, w_out, ffn_norm, w_up,
              conv_w, conv_b, w_down, final_norm):
    xp, xs = x_prompt, x_sample
    Bp, Tp = xp.shape[:2]
    pos_p = jnp.arange(Tp, dtype=jnp.int32)
    pos_s = PAST_LEN + jnp.arange(xs.shape[1], dtype=jnp.int32)
    new_p = [[] for _ in range(9)]
    new_s = [[] for _ in range(9)]
    for l in range(DEPTH):
        lam_init = 0.8 - 0.6 * math.exp(-0.3 * l)
        f32 = jnp.float32
        lam = (jnp.exp(jnp.sum(lambda_q1[l].astype(f32) * lambda_k1[l].astype(f32)))
               - jnp.exp(jnp.sum(lambda_q2[l].astype(f32) * lambda_k2[l].astype(f32))) + lam_init)
        cw = (cmp_pos_k[l], cmp_w1_k[l], cmp_w2_k[l], cmp_pos_v[l], cmp_w1_v[l], cmp_w2_v[l])

        q, q_rot, ck, cv, sk, sv, wk, wv, gate, dq, dk, dv = _project(xp, pos_p, attn_norm[l], w_in[l])
        o_nsa = _nsa_prompt(q, q_rot, gate, ck, cv, sk, sv, wk, wv, *cw)
        o_diff = _diff_merge(_diff_prompt(dq, dk, dv), lam, lam_init, subln_g[l])
        xp = xp + jnp.concatenate([o_nsa, o_diff], axis=-1).astype(xp.dtype) @ w_out[l]
        f, conv_p = _conv_ffn(xp, jnp.zeros((Bp, CONV_W - 1, 2 * D_FF), xp.dtype), ffn_norm[l],
                              w_up[l], conv_w[l], conv_b[l], w_down[l])
        xp = xp + f
        wb = min(WINDOW, Tp)
        for lst, val in zip(new_p, (ck, cv, sk, sv, dk, dv, wk[:, -wb:], wv[:, -wb:], conv_p)):
            lst.append(val)

        q, q_rot, ck, cv, sk, sv, wk, wv, gate, dq, dk, dv = _project(xs, pos_s, attn_norm[l], w_in[l])
        o_nsa, nwk, nwv = _nsa_sample(q, q_rot, gate, ck, cv, sk, sv, wk, wv, page_table,
                                      cache_cmp_k[l], cache_cmp_v[l], cache_sel_k[l], cache_sel_v[l],
                                      cache_win_k[l], cache_win_v[l], *cw)
        o_diff = _diff_merge(_diff_sample(dq, dk, dv, page_table, cache_diff_k[l], cache_diff_v[l]),
                             lam, lam_init, subln_g[l])
        xs = xs + jnp.concatenate([o_nsa, o_diff], axis=-1).astype(xs.dtype) @ w_out[l]
        f, conv_s = _conv_ffn(xs, state_ffn_conv[l], ffn_norm[l], w_up[l], conv_w[l], conv_b[l], w_down[l])
        xs = xs + f
        for lst, val in zip(new_s, (ck, cv, sk, sv, dk, dv, nwk, nwv, conv_s)):
            lst.append(val)

    y_prompt = _rmsnorm(xp, final_norm)
    y_sample = _rmsnorm(xs, final_norm)
    p_cmp_k, p_cmp_v, p_sel_k, p_sel_v, p_diff_k, p_diff_v, p_win_k, p_win_v, p_conv = [jnp.stack(v) for v in new_p]
    s_cmp_k, s_cmp_v, s_sel_k, s_sel_v, s_diff_k, s_diff_v, s_win_k, s_win_v, s_conv = [jnp.stack(v) for v in new_s]
    return (y_prompt, y_sample, p_cmp_k, p_cmp_v, p_sel_k, p_sel_v, p_diff_k, p_diff_v, p_win_k, p_win_v,
            p_conv, s_cmp_k, s_cmp_v, s_sel_k, s_sel_v, s_diff_k, s_diff_v, s_win_k, s_win_v, s_conv)
```

```python
import functools
import math

import numpy as np
import jax
import jax.numpy as jnp
from jax import lax
from jax.experimental import pallas as pl
from jax.experimental.pallas import tpu as pltpu

F32 = jnp.float32
BF16 = jnp.bfloat16

D_MODEL = 1024
HEAD_DIM = 64
NSA_WIDTH = 512
NSA_HEADS = 8
NSA_GROUPS = 2
NSA_REP = 4
KV_W = NSA_GROUPS * HEAD_DIM
CMP_BLOCK = 32
CMP_STRIDE = 16
CMP_HIDDEN = 128
SEL_BLOCK = 64
SEL_TOPN = 16
WINDOW = 512
FORCE_BONUS = 1.0e4
DIFF_WIDTH = 512
DIFF_VDIM = 128
DIFF_HEADS = 4
D_FF = 2816
CONV_W = 3
ROPE_THETA = 10000.0
RMS_EPS = 1e-6
ATTN_SCALE = 1.0 / math.sqrt(HEAD_DIM)
PAGE_SIZE = 128

LANES = 128
VMEM_LIMIT = 48 * 1024 * 1024
NEG_INF = float("-inf")

C_Q = 0
C_CK, C_CV, C_SK, C_SV, C_WK, C_WV = 512, 640, 768, 896, 1024, 1152
C_DQ, C_DK, C_DV = 1280, 1792, 2304
C_GATE = 2816
W_IN_COLS = 2944


def _cparams(sem):
    return pltpu.CompilerParams(dimension_semantics=sem, vmem_limit_bytes=VMEM_LIMIT)


def _rope128(v, cos, sin_signed):
    lane = lax.broadcasted_iota(jnp.int32, v.shape, 1)
    first = (lane % HEAD_DIM) < (HEAD_DIM // 2)
    partner = jnp.where(first, pltpu.roll(v, LANES - HEAD_DIM // 2, 1), pltpu.roll(v, HEAD_DIM // 2, 1))
    return v * cos + partner * sin_signed


def _proj_kernel(x_ref, g_ref, w_ref, cos_ref, sin_ref,
                 qraw_ref, qrot_ref, ck_ref, cv_ref, sk_ref, sv_ref, wk_ref, wv_ref,
                 gate_ref, dq_ref, dk_ref, dv_ref):
    x = x_ref[...]
    h = x * lax.rsqrt(jnp.mean(x * x, axis=-1, keepdims=True) + RMS_EPS) * g_ref[...]
    z = jnp.dot(h.astype(BF16), w_ref[...], preferred_element_type=F32)
    cos = cos_ref[...]
    sin = sin_ref[...]

    def put(ref, col, width, rope):
        for c in range(width // LANES):
            v = z[:, col + c * LANES: col + (c + 1) * LANES]
            ref[:, c * LANES:(c + 1) * LANES] = _rope128(v, cos, sin) if rope else v

    put(qraw_ref, C_Q, NSA_WIDTH, False)
    put(qrot_ref, C_Q, NSA_WIDTH, True)
    put(ck_ref, C_CK, KV_W, False)
    put(cv_ref, C_CV, KV_W, False)
    put(sk_ref, C_SK, KV_W, True)
    put(sv_ref, C_SV, KV_W, False)
    put(wk_ref, C_WK, KV_W, True)
    put(wv_ref, C_WV, KV_W, False)
    put(dq_ref, C_DQ, DIFF_WIDTH, True)
    put(dk_ref, C_DK, DIFF_WIDTH, True)
    put(dv_ref, C_DV, DIFF_WIDTH, False)
    gate_ref[...] = jax.nn.sigmoid(z[:, C_GATE:C_GATE + LANES])


def _project(x2d, norm_g, w_in_r, cos_t, sin_t, tm):
    M = x2d.shape[0]
    nt = cos_t.shape[0] // tm
    widths = (NSA_WIDTH, NSA_WIDTH, KV_W, KV_W, KV_W, KV_W, KV_W, KV_W, LANES,
              DIFF_WIDTH, DIFF_WIDTH, DIFF_WIDTH)
    row = lambda i: (i, 0)
    return pl.pallas_call(
        _proj_kernel,
        grid=(M // tm,),
        in_specs=[pl.BlockSpec((tm, D_MODEL), row),
                  pl.BlockSpec((1, D_MODEL), lambda i: (0, 0)),
                  pl.BlockSpec((D_MODEL, W_IN_COLS), lambda i: (0, 0)),
                  pl.BlockSpec((tm, LANES), lambda i: (i % nt, 0)),
                  pl.BlockSpec((tm, LANES), lambda i: (i % nt, 0))],
        out_specs=[pl.BlockSpec((tm, w), row) for w in widths],
        out_shape=[jax.ShapeDtypeStruct((M, w), F32) for w in widths],
        compiler_params=_cparams(("arbitrary",)),
        name="in_proj",
    )(x2d, norm_g, w_in_r, cos_t, sin_t)


def _gelu(x):
    return 0.5 * x * (1.0 + jnp.tanh(math.sqrt(2.0 / math.pi) * (x + 0.044715 * (x * x * x))))


def _compress_rows(c, w1cat, posb, w1n, w2bd):
    R = c.shape[0]
    r = jnp.dot(c.astype(BF16), w1cat, preferred_element_type=F32)
    bias = jnp.dot(posb, w1n, preferred_element_type=F32)[0:1]
    bias2 = jnp.concatenate([bias, bias], axis=1)
    hid = r[:, :2 * CMP_HIDDEN] + pltpu.roll(r[:, 2 * CMP_HIDDEN:], R - 1, 0) + bias2
    return jnp.dot(_gelu(hid).astype(BF16), w2bd, preferred_element_type=F32)


def _compress_kernel(c_ref, w1cat_ref, posb_ref, w1n_ref, w2bd_ref, o_ref):
    o_ref[0] = _compress_rows(c_ref[0], w1cat_ref[...], posb_ref[...], w1n_ref[...], w2bd_ref[...])


def _compress_weights(pos, w1, w2):
    w1r = w1.reshape(2, CMP_STRIDE, HEAD_DIM, CMP_HIDDEN)
    eye = jnp.eye(NSA_GROUPS, dtype=w1.dtype)
    big = jnp.einsum('isdh,gk->isgdkh', w1r, eye).reshape(2, CMP_STRIDE * KV_W, NSA_GROUPS * CMP_HIDDEN)
    w1cat = jnp.concatenate([big[0], big[1]], axis=1).astype(BF16)
    w2bd = jnp.einsum('hd,gk->ghkd', w2, eye).reshape(NSA_GROUPS * CMP_HIDDEN, KV_W).astype(BF16)
    posb = jnp.broadcast_to(pos.reshape(1, CMP_BLOCK * HEAD_DIM), (8, CMP_BLOCK * HEAD_DIM)).astype(BF16)
    return w1cat, posb, w1.astype(BF16), w2bd


def _compress_prompt(x, cw):
    B, T, _ = x.shape
    R = T // CMP_STRIDE
    c = x.reshape(B, R, CMP_STRIDE * KV_W)
    full = lambda a: pl.BlockSpec(a.shape, lambda b: (0,) * a.ndim)
    return pl.pallas_call(
        _compress_kernel,
        grid=(B,),
        in_specs=[pl.BlockSpec((1, R, CMP_STRIDE * KV_W), lambda b: (b, 0, 0))] + [full(a) for a in cw],
        out_specs=pl.BlockSpec((1, R, KV_W), lambda b: (b, 0, 0)),
        out_shape=jax.ShapeDtypeStruct((B, R, KV_W), F32),
        compiler_params=_cparams(("arbitrary",)),
        name="compress_prompt",
    )(c, *cw)


def _lane_group_masks(shape):
    lane = lax.broadcasted_iota(jnp.int32, shape, 1)
    lo = lane < HEAD_DIM
    return lo, jnp.logical_not(lo)


def _block_diag_rows(x, dtype):
    lo, hi = _lane_group_masks(x.shape)
    return jnp.concatenate([jnp.where(lo, x, 0.0), jnp.where(hi, x, 0.0)], axis=0).astype(dtype)


def _dot_nt(a, b):
    return lax.dot_general(a, b, (((1,), (1,)), ((), ())), preferred_element_type=F32)


def _softmax_update(s, mask, m_ref, l_ref, acc_ref, vbd, tk):
    s = jnp.where(mask, s * ATTN_SCALE, NEG_INF)
    ps = []
    alphas = []
    for g in range(NSA_GROUPS):
        sg = s[:, g * tk:(g + 1) * tk]
        m_old = m_ref[g]
        m_new = jnp.maximum(m_old, jnp.max(sg, axis=-1, keepdims=True))
        m_safe = jnp.where(m_new == NEG_INF, 0.0, m_new)
        alpha = jnp.exp(m_old - m_safe)
        p = jnp.exp(sg - m_safe)
        l_ref[g] = alpha * l_ref[g] + jnp.sum(p, axis=-1, keepdims=True)
        m_ref[g] = m_new
        ps.append(p)
        alphas.append(alpha)
    p = jnp.concatenate(ps, axis=1).astype(BF16)
    lo, _ = _lane_group_masks(acc_ref.shape)
    alpha = jnp.where(lo, alphas[0], alphas[1])
    acc_ref[...] = alpha * acc_ref[...] + jnp.dot(p, vbd, preferred_element_type=F32)


def _finish(acc_ref, l_ref):
    lo, _ = _lane_group_masks(acc_ref.shape)
    l = jnp.where(lo, l_ref[0], l_ref[1])
    return acc_ref[...] / jnp.maximum(l, 1e-30)


def _reset(m_ref, l_ref, acc_ref):
    m_ref[...] = jnp.full(m_ref.shape, NEG_INF, F32)
    l_ref[...] = jnp.zeros(l_ref.shape, F32)
    acc_ref[...] = jnp.zeros(acc_ref.shape, F32)


def _stack_reps(q):
    return jnp.concatenate([q[:, r * LANES:(r + 1) * LANES] for r in range(NSA_REP)], axis=0)


def _unstack_reps(o, tq):
    return jnp.concatenate([o[r * tq:(r + 1) * tq] for r in range(NSA_REP)], axis=1)


def _split2(x):
    hi = x.astype(BF16)
    return hi, (x - hi.astype(F32)).astype(BF16)


def _select_blocks(score_t, q_pos_row, n_sel):
    n_pad = score_t.shape[0] // NSA_GROUPS
    tq = score_t.shape[1]
    out = []
    blk = lax.broadcasted_iota(jnp.int32, (n_pad, tq), 0)
    cur = q_pos_row // SEL_BLOCK
    forced = (blk == 0) | (blk == cur) | (blk == cur - 1)
    ok = (blk * SEL_BLOCK <= q_pos_row) & (blk < n_sel)
    for g in range(NSA_GROUPS):
        sc = score_t[g * n_pad:(g + 1) * n_pad]
        sc = jnp.where(ok, sc + jnp.where(forced, FORCE_BONUS, 0.0), NEG_INF)
        cnt = jnp.zeros((n_pad, tq), F32)
        for i in range(n_pad):
            row = sc[i:i + 1]
            cnt = cnt + jnp.where(blk > i, jnp.where(row >= sc, 1.0, 0.0), jnp.where(row > sc, 1.0, 0.0))
        out.append(jnp.where((cnt < float(SEL_TOPN)) & ok, 1.0, 0.0))
    return jnp.concatenate(out, axis=0)


def _nsa_prompt_kernel(qraw_ref, qrot_ref, gate_ref, kc_ref, vc_ref, sk_ref, sv_ref, wk_ref, wv_ref,
                       ovbd_ref, gexp_ref, o_ref,
                       skbd, svbd, wkbd, wvbd, m_ref, l_ref, acc_ref, *, tq, tk, seq):
    i = pl.program_id(1)
    n_kt = seq // tk
    n_sel = seq // SEL_BLOCK
    n_cmp = kc_ref.shape[1]

    @pl.when(i == 0)
    def _():
        for src, dst in ((sk_ref, skbd), (sv_ref, svbd), (wk_ref, wkbd), (wv_ref, wvbd)):
            def fill(j, _, src=src, dst=dst):
                dst[j] = _block_diag_rows(src[0, pl.ds(pl.multiple_of(j * tk, tk), tk), :], BF16)
                return 0
            lax.fori_loop(0, n_kt, fill, 0)

    q0 = i * tq
    rows = NSA_REP * tq
    q_pos = q0 + lax.broadcasted_iota(jnp.int32, (tq, 1), 0)
    q_pos4 = jnp.concatenate([q_pos] * NSA_REP, axis=0)

    qs = _stack_reps(qraw_ref[...]).astype(BF16)
    kcbd = _block_diag_rows(kc_ref[0], BF16)
    vcbd = _block_diag_rows(vc_ref[0], BF16)
    s = _dot_nt(qs, kcbd)
    kidx = lax.broadcasted_iota(jnp.int32, (rows, NSA_GROUPS * n_cmp), 1) % n_cmp
    cmask = (kidx * CMP_STRIDE + (CMP_BLOCK - 1) <= q_pos4) & (kidx < n_cmp - 1)
    _reset(m_ref, l_ref, acc_ref)
    s = jnp.where(cmask, s * ATTN_SCALE, NEG_INF)
    pcs = []
    for g in range(NSA_GROUPS):
        sg = s[:, g * n_cmp:(g + 1) * n_cmp]
        m = jnp.max(sg, axis=-1, keepdims=True)
        m = jnp.where(m == NEG_INF, 0.0, m)
        p = jnp.exp(sg - m)
        pcs.append(p / jnp.maximum(jnp.sum(p, axis=-1, keepdims=True), 1e-30))
    pc = jnp.concatenate(pcs, axis=1)
    o_c = jnp.dot(pc.astype(BF16), vcbd, preferred_element_type=F32)
    imp = pc[0:tq] + pc[tq:2 * tq] + pc[2 * tq:3 * tq] + pc[3 * tq:4 * tq]
    ih, il = _split2(imp)
    score = (jnp.dot(ih, ovbd_ref[...], preferred_element_type=F32)
             + jnp.dot(il, ovbd_ref[...], preferred_element_type=F32))
    q_pos_row = q0 + lax.broadcasted_iota(jnp.int32, (1, tq), 1)
    sel_t = _select_blocks(score.T, q_pos_row, n_sel)
    sel = sel_t.T.astype(BF16)
    n_pad = sel.shape[1] // NSA_GROUPS

    qr = _stack_reps(qrot_ref[...]).astype(BF16)
    _reset(m_ref, l_ref, acc_ref)

    def sel_step(j, _):
        k0 = j * tk
        s = _dot_nt(qr, skbd[j])
        col = lax.broadcasted_iota(jnp.int32, (NSA_GROUPS * n_pad, NSA_GROUPS * tk), 1)
        rowi = lax.broadcasted_iota(jnp.int32, (NSA_GROUPS * n_pad, NSA_GROUPS * tk), 0)
        want = (col // tk) * n_pad + (k0 + col % tk) // SEL_BLOCK
        expand = jnp.where(rowi == want, 1.0, 0.0).astype(BF16)
        m1 = jnp.dot(sel, expand, preferred_element_type=F32)
        kpos = k0 + lax.broadcasted_iota(jnp.int32, (tq, NSA_GROUPS * tk), 1) % tk
        m1 = (m1 > 0.5) & (kpos <= q_pos)
        mask = jnp.concatenate([m1] * NSA_REP, axis=0)
        _softmax_update(s, mask, m_ref, l_ref, acc_ref, svbd[j], tk)
        return 0

    lax.fori_loop(0, (q0 + tq - 1) // tk + 1, sel_step, 0)
    o_s = _finish(acc_ref, l_ref)

    _reset(m_ref, l_ref, acc_ref)

    def win_step(j, _):
        k0 = j * tk
        s = _dot_nt(qr, wkbd[j])
        kpos = k0 + lax.broadcasted_iota(jnp.int32, (rows, NSA_GROUPS * tk), 1) % tk
        mask = (kpos <= q_pos4) & (kpos > q_pos4 - WINDOW)
        _softmax_update(s, mask, m_ref, l_ref, acc_ref, wvbd[j], tk)
        return 0

    lax.fori_loop(jnp.maximum(q0 - WINDOW + 1, 0) // tk, (q0 + tq - 1) // tk + 1, win_step, 0)
    o_w = _finish(acc_ref, l_ref)

    gh, gl = _split2(gate_ref[...])
    gx = (jnp.dot(gh, gexp_ref[...], preferred_element_type=F32)
          + jnp.dot(gl, gexp_ref[...], preferred_element_type=F32))
    o = (gx[:, 0:NSA_WIDTH] * _unstack_reps(o_c, tq)
         + gx[:, NSA_WIDTH:2 * NSA_WIDTH] * _unstack_reps(o_s, tq)
         + gx[:, 2 * NSA_WIDTH:3 * NSA_WIDTH] * _unstack_reps(o_w, tq))
    o_ref[...] = o


def _overlap_matrix(n_cmp_rows, n_sel_pad):
    cs = np.arange(n_cmp_rows)[:, None] * CMP_STRIDE
    ss = np.arange(n_sel_pad)[None, :] * SEL_BLOCK
    ov = ((cs < ss + SEL_BLOCK) & (cs + CMP_BLOCK > ss)).astype(np.float32)
    z = np.zeros_like(ov)
    return jnp.asarray(np.block([[ov, z], [z, ov]]), BF16)


def _gate_expand_matrix():
    e = np.zeros((LANES, 3 * NSA_WIDTH), np.float32)
    for c in range(3):
        for r in range(NSA_REP):
            for g in range(NSA_GROUPS):
                col = c * NSA_WIDTH + r * LANES + g * HEAD_DIM
                e[c * NSA_HEADS + r * NSA_GROUPS + g, col:col + HEAD_DIM] = 1.0
    return jnp.asarray(e, BF16)


def _nsa_prompt(qraw, qrot, gate, kc, vc, sk, sv, wk, wv, B, T, tq, tk):
    nq = T // tq
    n_cmp = kc.shape[1]
    n_sel_pad = max(T // SEL_BLOCK, HEAD_DIM)
    ovbd = _overlap_matrix(n_cmp, n_sel_pad)
    gexp = _gate_expand_matrix()
    qspec = lambda w: pl.BlockSpec((tq, w), lambda b, i: (b * nq + i, 0))
    kvspec = lambda n: pl.BlockSpec((1, n, KV_W), lambda b, i: (b, 0, 0))
    const = lambda a: pl.BlockSpec(a.shape, lambda b, i: (0, 0))
    rows = NSA_REP * tq
    kern = functools.partial(_nsa_prompt_kernel, tq=tq, tk=tk, seq=T)
    kv3 = lambda a: a.reshape(B, T, KV_W)
    return pl.pallas_call(
        kern,
        grid=(B, nq),
        in_specs=[qspec(NSA_WIDTH), qspec(NSA_WIDTH), qspec(LANES), kvspec(n_cmp), kvspec(n_cmp),
                  kvspec(T), kvspec(T), kvspec(T), kvspec(T), const(ovbd), const(gexp)],
        out_specs=qspec(NSA_WIDTH),
        out_shape=jax.ShapeDtypeStruct((B * T, NSA_WIDTH), F32),
        scratch_shapes=[pltpu.VMEM((T // tk, 2 * tk, KV_W), BF16) for _ in range(4)]
        + [pltpu.VMEM((NSA_GROUPS, rows, 1), F32), pltpu.VMEM((NSA_GROUPS, rows, 1), F32),
           pltpu.VMEM((rows, KV_W), F32)],
        compiler_params=_cparams(("arbitrary", "arbitrary")),
        name="nsa_prompt",
    )(qraw, qrot, gate, kc, vc, kv3(sk), kv3(sv), kv3(wk), kv3(wv), ovbd, gexp)


def _lambda_value(lam_ref, lam_init):
    a = jnp.sum(lam_ref[0:1] * lam_ref[1:2], axis=-1, keepdims=True)
    b = jnp.sum(lam_ref[2:3] * lam_ref[3:4], axis=-1, keepdims=True)
    return jnp.exp(a) - jnp.exp(b) + lam_init


def _diff_merge_rows(o0, o1, lam, g, lam_init):
    od = o0 - lam * o1
    od = od * lax.rsqrt(jnp.mean(od * od, axis=-1, keepdims=True) + RMS_EPS) * g
    return od * (1.0 - lam_init)


def _diff_prompt_kernel(lam_ref, dq_ref, dk_ref, dv_ref, g_ref, o_ref,
                        kbf, vbf, m_ref, l_ref, acc_ref, *, tq, tk, lam_init):
    i = pl.program_id(2)

    @pl.when(i == 0)
    def _():
        kbf[...] = dk_ref[...].astype(BF16)
        vbf[...] = dv_ref[...].astype(BF16)

    q = dq_ref[...]
    lo, hi = _lane_group_masks(q.shape)
    q2 = jnp.concatenate([jnp.where(lo, q, 0.0), jnp.where(hi, q, 0.0)], axis=0).astype(BF16)
    q0 = i * tq
    q_pos = q0 + lax.broadcasted_iota(jnp.int32, (2 * tq, 1), 0) % tq
    m_ref[...] = jnp.full(m_ref.shape, NEG_INF, F32)
    l_ref[...] = jnp.zeros(l_ref.shape, F32)
    acc_ref[...] = jnp.zeros(acc_ref.shape, F32)

    def step(j, _):
        k0 = pl.multiple_of(j * tk, tk)
        s = _dot_nt(q2, kbf[pl.ds(k0, tk), :]) * ATTN_SCALE
        kpos = k0 + lax.broadcasted_iota(jnp.int32, s.shape, 1)
        s = jnp.where(kpos <= q_pos, s, NEG_INF)
        m_old = m_ref[...]
        m_new = jnp.maximum(m_old, jnp.max(s, axis=-1, keepdims=True))
        alpha = jnp.exp(m_old - m_new)
        p = jnp.exp(s - m_new)
        l_ref[...] = alpha * l_ref[...] + jnp.sum(p, axis=-1, keepdims=True)
        m_ref[...] = m_new
        acc_ref[...] = alpha * acc_ref[...] + jnp.dot(p.astype(BF16), vbf[pl.ds(k0, tk), :],
                                                      preferred_element_type=F32)
        return 0

    lax.fori_loop(0, (q0 + tq - 1) // tk + 1, step, 0)
    o = acc_ref[...] / l_ref[...]
    lam = _lambda_value(lam_ref, lam_init)
    o_ref[...] = _diff_merge_rows(o[:tq], o[tq:], lam, g_ref[...], lam_init)


def _diff_prompt(lam4, dq, dk, dv, subln_g, B, T, tq, tk, lam_init):
    nq = T // tq
    kern = functools.partial(_diff_prompt_kernel, tq=tq, tk=tk, lam_init=lam_init)
    return pl.pallas_call(
        kern,
        grid=(B, DIFF_HEADS, nq),
        in_specs=[pl.BlockSpec((4, HEAD_DIM), lambda b, h, i: (0, 0)),
                  pl.BlockSpec((tq, DIFF_VDIM), lambda b, h, i: (b * nq + i, h)),
                  pl.BlockSpec((T, DIFF_VDIM), lambda b, h, i: (b, h)),
                  pl.BlockSpec((T, DIFF_VDIM), lambda b, h, i: (b, h)),
                  pl.BlockSpec((1, DIFF_VDIM), lambda b, h, i: (0, 0))],
        out_specs=pl.BlockSpec((tq, DIFF_VDIM), lambda b, h, i: (b * nq + i, h)),
        out_shape=jax.ShapeDtypeStruct((B * T, DIFF_WIDTH), F32),
        scratch_shapes=[pltpu.VMEM((T, DIFF_VDIM), BF16), pltpu.VMEM((T, DIFF_VDIM), BF16),
                        pltpu.VMEM((2 * tq, 1), F32), pltpu.VMEM((2 * tq, 1), F32),
                        pltpu.VMEM((2 * tq, DIFF_VDIM), F32)],
        compiler_params=_cparams(("arbitrary", "arbitrary", "arbitrary")),
        name="diff_prompt",
    )(lam4, dq, dk, dv, subln_g)


FF_CHUNK = 1408


def _rms(x, g):
    return x * lax.rsqrt(jnp.mean(x * x, axis=-1, keepdims=True) + RMS_EPS) * g


def _ffn_seq_kernel(x_ref, on_ref, od_ref, won_ref, wod_ref, g2_ref, wup_ref, cw_ref, cb_ref, wdn_ref,
                    gf_ref, y_ref, conv_ref, carry_ref, *, tm, tiles_per_seq):
    i = pl.program_id(0)
    first = (i % tiles_per_seq) == 0
    x1 = (x_ref[...] + jnp.dot(on_ref[...].astype(BF16), won_ref[...], preferred_element_type=F32)
          + jnp.dot(od_ref[...].astype(BF16), wod_ref[...], preferred_element_type=F32))
    h = _rms(x1, g2_ref[...]).astype(BF16)
    rowi = lax.broadcasted_iota(jnp.int32, (tm, 1), 0)
    f = jnp.zeros((tm, D_MODEL), F32)
    for k in range(D_FF // FF_CHUNK):
        halves = []
        for part in range(2):
            c0 = part * D_FF + k * FF_CHUNK
            cs = slice(c0, c0 + FF_CHUNK)
            u = jnp.dot(h, wup_ref[:, cs], preferred_element_type=F32)
            pm2 = jnp.where(first, 0.0, carry_ref[6:7, cs])
            pm1 = jnp.where(first, 0.0, carry_ref[7:8, cs])
            u1 = jnp.where(rowi == 0, pm1, pltpu.roll(u, 1, 0))
            u2 = jnp.where(rowi == 0, pm2, jnp.where(rowi == 1, pm1, pltpu.roll(u, 2, 0)))
            halves.append(cb_ref[:, cs] + u2 * cw_ref[0:1, cs] + u1 * cw_ref[1:2, cs] + u * cw_ref[2:3, cs])
            carry_ref[6:8, cs] = u[tm - 2:tm]
            conv_ref[0, :, cs] = u[tm - 2:tm]
        act = (jax.nn.silu(halves[0]) * halves[1]).astype(BF16)
        f = f + jnp.dot(act, wdn_ref[k * FF_CHUNK:(k + 1) * FF_CHUNK, :], preferred_element_type=F32)
    y_ref[...] = _rms(x1 + f, gf_ref[...])


def _ffn_step_kernel(x_ref, on_ref, od_ref, won_ref, wod_ref, g2_ref, wup_ref, cw_ref, cb_ref, wdn_ref,
                     gf_ref, p0_ref, p1_ref, y_ref, u_ref):
    x1 = (x_ref[...] + jnp.dot(on_ref[...].astype(BF16), won_ref[...], preferred_element_type=F32)
          + jnp.dot(od_ref[...].astype(BF16), wod_ref[...], preferred_element_type=F32))
    h = _rms(x1, g2_ref[...]).astype(BF16)
    f = jnp.zeros(x1.shape, F32)
    for k in range(D_FF // FF_CHUNK):
        halves = []
        for part in range(2):
            c0 = part * D_FF + k * FF_CHUNK
            cs = slice(c0, c0 + FF_CHUNK)
            u = jnp.dot(h, wup_ref[:, cs], preferred_element_type=F32)
            u_ref[:, cs] = u
            halves.append(cb_ref[:, cs] + p0_ref[:, cs] * cw_ref[0:1, cs] + p1_ref[:, cs] * cw_ref[1:2, cs]
                          + u * cw_ref[2:3, cs])
        act = (jax.nn.silu(halves[0]) * halves[1]).astype(BF16)
        f = f + jnp.dot(act, wdn_ref[k * FF_CHUNK:(k + 1) * FF_CHUNK, :], preferred_element_type=F32)
    y_ref[...] = _rms(x1 + f, gf_ref[...])


def _const_spec(a, ngrid):
    return pl.BlockSpec(a.shape, lambda *idx: (0,) * a.ndim, pipeline_mode=pl.Buffered(1))


def _ffn_weights_specs(ws):
    return [_const_spec(a, 1) for a in ws]


def _ffn_seq(x2d, o_nsa, o_diff, ws, B, T, tm):
    M = x2d.shape[0]
    tps = T // tm
    row = lambda w: pl.BlockSpec((tm, w), lambda i: (i, 0))
    kern = functools.partial(_ffn_seq_kernel, tm=tm, tiles_per_seq=tps)
    return pl.pallas_call(
        kern,
        grid=(M // tm,),
        in_specs=[row(D_MODEL), row(NSA_WIDTH), row(DIFF_WIDTH)] + _ffn_weights_specs(ws),
        out_specs=[row(D_MODEL), pl.BlockSpec((1, CONV_W - 1, 2 * D_FF), lambda i: (i // tps, 0, 0))],
        out_shape=[jax.ShapeDtypeStruct((M, D_MODEL), F32),
                   jax.ShapeDtypeStruct((B, CONV_W - 1, 2 * D_FF), F32)],
        scratch_shapes=[pltpu.VMEM((8, 2 * D_FF), F32)],
        compiler_params=_cparams(("arbitrary",)),
        name="ffn_prompt",
    )(x2d, o_nsa, o_diff, *ws)


def _ffn_step(x2d, o_nsa, o_diff, ws, p0, p1):
    M = x2d.shape[0]
    full = lambda a: pl.BlockSpec(a.shape, lambda i: (0,) * a.ndim)
    ins = (x2d, o_nsa, o_diff) + tuple(ws) + (p0, p1)
    return pl.pallas_call(
        _ffn_step_kernel,
        grid=(1,),
        in_specs=[full(a) for a in ins],
        out_specs=[pl.BlockSpec((M, D_MODEL), lambda i: (0, 0)), pl.BlockSpec((M, 2 * D_FF), lambda i: (0, 0))],
        out_shape=[jax.ShapeDtypeStruct((M, D_MODEL), F32), jax.ShapeDtypeStruct((M, 2 * D_FF), F32)],
        compiler_params=_cparams(("arbitrary",)),
        name="ffn_sample",
    )(*ins)


ROWS_PER_PAGE = PAGE_SIZE // CMP_STRIDE


def _compress_sample_kernel(pt_ref, pool_ref, tail_ref, w1cat_ref, posb_ref, w1n_ref, w2bd_ref, o_ref,
                            buf, sem, *, n_pages, nb, rows):
    b = pl.program_id(0)
    past_rows = n_pages * ROWS_PER_PAGE

    def page_copy(bb, n, slot):
        return pltpu.make_async_copy(pool_ref.at[pt_ref[bb * n_pages + n]],
                                     buf.at[slot, pl.ds(n * ROWS_PER_PAGE, ROWS_PER_PAGE), :], sem.at[slot])

    def start_all(bb, slot):
        def body(n, _):
            page_copy(bb, n, slot).start()
            return 0
        lax.fori_loop(0, n_pages, body, 0)

    @pl.when(b == 0)
    def _():
        start_all(0, 0)

    @pl.when(b + 1 < nb)
    def _():
        start_all(b + 1, (b + 1) % 2)

    slot = b % 2

    def wait_body(n, _):
        page_copy(b, n, slot).wait()
        return 0
    lax.fori_loop(0, n_pages, wait_body, 0)

    buf[slot, past_rows:rows, :] = tail_ref[0]
    o_ref[0, 0:rows, :] = _compress_rows(buf[slot], w1cat_ref[...], posb_ref[...], w1n_ref[...], w2bd_ref[...])
    o_ref[0, rows:, :] = jnp.zeros((o_ref.shape[1] - rows, KV_W), F32)


def _compress_sample(pool, pt_flat, new_row, cw, nb, n_pages, rows, rows_out):
    n_pool = pool.shape[0]
    width = CMP_STRIDE * KV_W
    pool_r = pool.reshape(n_pool, ROWS_PER_PAGE, width)
    tail_rows = rows - n_pages * ROWS_PER_PAGE
    tail = jnp.zeros((nb, tail_rows, width), F32).at[:, 0, :KV_W].set(new_row)
    kern = functools.partial(_compress_sample_kernel, n_pages=n_pages, nb=nb, rows=rows)
    full = lambda a: pl.BlockSpec(a.shape, lambda b, pt: (0,) * a.ndim)
    return pl.pallas_call(
        kern,
        grid_spec=pltpu.PrefetchScalarGridSpec(
            num_scalar_prefetch=1,
            grid=(nb,),
            in_specs=[pl.BlockSpec(memory_space=pl.ANY),
                      pl.BlockSpec((1, tail_rows, width), lambda b, pt: (b, 0, 0))] + [full(a) for a in cw],
            out_specs=pl.BlockSpec((1, rows_out, KV_W), lambda b, pt: (b, 0, 0)),
            scratch_shapes=[pltpu.VMEM((2, rows, width), F32), pltpu.SemaphoreType.DMA((2,))]),
        out_shape=jax.ShapeDtypeStruct((nb, rows_out, KV_W), F32),
        compiler_params=_cparams(("arbitrary",)),
        name="compress_sample",
    )(pt_flat, pool_r, tail, *cw)


def _pad_rows8(x):
    return jnp.concatenate([x, jnp.zeros((8 - x.shape[0], x.shape[1]), x.dtype)], axis=0)


def _nsa_sample_a_kernel(q_ref, kc_ref, vc_ref, ovbd_ref, oc_ref, info_ref, *, n_cmp, n_sel, q_pos):
    rk = kc_ref.shape[1]
    n_pad = ovbd_ref.shape[1] // NSA_GROUPS
    q8 = _pad_rows8(q_ref[0]).astype(BF16)
    kcbd = _block_diag_rows(kc_ref[0], BF16)
    vcbd = _block_diag_rows(vc_ref[0], BF16)
    s = _dot_nt(q8, kcbd)
    kidx = lax.broadcasted_iota(jnp.int32, s.shape, 1) % rk
    cmask = (kidx * CMP_STRIDE + (CMP_BLOCK - 1) <= q_pos) & (kidx < n_cmp)
    s = jnp.where(cmask, s * ATTN_SCALE, NEG_INF)
    pcs = []
    for g in range(NSA_GROUPS):
        sg = s[:, g * rk:(g + 1) * rk]
        m = jnp.max(sg, axis=-1, keepdims=True)
        m = jnp.where(m == NEG_INF, 0.0, m)
        p = jnp.exp(sg - m)
        pcs.append(p / jnp.maximum(jnp.sum(p, axis=-1, keepdims=True), 1e-30))
    pc = jnp.concatenate(pcs, axis=1)
    oc_ref[0] = jnp.dot(pc.astype(BF16), vcbd, preferred_element_type=F32)
    imp = jnp.sum(pc[0:NSA_REP], axis=0, keepdims=True)
    imp8 = jnp.concatenate([imp] * 8, axis=0)
    ih, il = _split2(imp8)
    score = (jnp.dot(ih, ovbd_ref[...], preferred_element_type=F32)
             + jnp.dot(il, ovbd_ref[...], preferred_element_type=F32))[0:1]

    blk_r = lax.broadcasted_iota(jnp.int32, (1, n_pad), 1)
    blk_c = lax.broadcasted_iota(jnp.int32, (n_pad, 1), 0)
    cur = q_pos // SEL_BLOCK
    forced_r = (blk_r == 0) | (blk_r == cur) | (blk_r == cur - 1)
    ok_r = (blk_r * SEL_BLOCK <= q_pos) & (blk_r < n_sel)
    ok_c = (blk_c * SEL_BLOCK <= q_pos) & (blk_c < n_sel)
    ii = lax.broadcasted_iota(jnp.int32, (n_pad, n_pad), 1)
    jj = lax.broadcasted_iota(jnp.int32, (n_pad, n_pad), 0)
    kk = lax.broadcasted_iota(jnp.int32, (n_pad, LANES), 1)
    jf = lax.broadcasted_iota(jnp.int32, (n_pad, LANES), 0).astype(F32)
    rows_out = []
    for g in range(NSA_GROUPS):
        sr = score[:, g * n_pad:(g + 1) * n_pad]
        sr = jnp.where(ok_r, sr + jnp.where(forced_r, FORCE_BONUS, 0.0), NEG_INF)
        sc = jnp.sum(jnp.where(ii == jj, sr, 0.0), axis=1, keepdims=True)
        ahead = jnp.where(ii < jj, jnp.where(sr >= sc, 1.0, 0.0), jnp.where(sr > sc, 1.0, 0.0))
        cnt = jnp.sum(ahead, axis=1, keepdims=True)
        chosen = (cnt < float(SEL_TOPN)) & ok_c
        hit = chosen & (cnt == kk.astype(F32))
        rows_out.append((jnp.sum(jnp.where(hit, jf, 0.0), axis=0, keepdims=True),
                         jnp.sum(jnp.where(hit, 1.0, 0.0), axis=0, keepdims=True)))
    info = jnp.concatenate([rows_out[0][0], rows_out[1][0], rows_out[0][1], rows_out[1][1],
                            jnp.zeros((4, LANES), F32)], axis=0)
    info_ref[0] = info.astype(jnp.int32)


def _nsa_sample_a(qraw3, kc, vc, q_pos, n_cmp, n_sel):
    nb, rk, _ = kc.shape
    n_pad = -(-n_sel // LANES) * LANES
    ovbd = _overlap_matrix(rk, n_pad)
    kern = functools.partial(_nsa_sample_a_kernel, n_cmp=n_cmp, n_sel=n_sel, q_pos=q_pos)
    return pl.pallas_call(
        kern,
        grid=(nb,),
        in_specs=[pl.BlockSpec((1, NSA_REP, LANES), lambda b: (b, 0, 0)),
                  pl.BlockSpec((1, rk, KV_W), lambda b: (b, 0, 0)),
                  pl.BlockSpec((1, rk, KV_W), lambda b: (b, 0, 0)),
                  pl.BlockSpec(ovbd.shape, lambda b: (0, 0))],
        out_specs=[pl.BlockSpec((1, 8, LANES), lambda b: (b, 0, 0)),
                   pl.BlockSpec((1, 8, LANES), lambda b: (b, 0, 0))],
        out_shape=[jax.ShapeDtypeStruct((nb, 8, LANES), F32), jax.ShapeDtypeStruct((nb, 8, LANES), jnp.int32)],
        compiler_params=_cparams(("arbitrary",)),
        name="nsa_sample_select",
    )(qraw3, kc, vc, ovbd)


N_SLOTS = NSA_GROUPS * SEL_TOPN


def _decode_attend(s, mask, vbd, s_new, inc_new, v_new, half):
    s = jnp.where(mask, s, NEG_INF)
    ps, ls, pn = [], [], []
    for g in range(NSA_GROUPS):
        sg = s[:, g * half:(g + 1) * half]
        sn = jnp.where(inc_new[g] > 0.5, s_new[g], NEG_INF)
        m = jnp.maximum(jnp.max(sg, axis=-1, keepdims=True), sn)
        m = jnp.where(m == NEG_INF, 0.0, m)
        p = jnp.exp(sg - m)
        pnew = jnp.exp(sn - m)
        ps.append(p)
        pn.append(pnew)
        ls.append(jnp.sum(p, axis=-1, keepdims=True) + pnew)
    p = jnp.concatenate(ps, axis=1).astype(BF16)
    lo, _ = _lane_group_masks((8, KV_W))
    o = jnp.dot(p, vbd, preferred_element_type=F32) + jnp.where(lo, pn[0], pn[1]) * v_new
    return o / jnp.maximum(jnp.where(lo, ls[0], ls[1]), 1e-30)


def _nsa_sample_b_kernel(sel_ref, pt_ref, q_ref, gate_ref, oc_ref, info_ref, psk_ref, psv_ref,
                         sknew_ref, svnew_ref, wink_ref, winv_ref, wknew_ref, wvnew_ref, eexp_ref, e2_ref,
                         o_ref, swk_ref, swv_ref, kbuf, vbuf, sem, *, nb, n_pages, n_past_blk, wbuf):
    b = pl.program_id(0)
    blk_per_page = PAGE_SIZE // SEL_BLOCK

    def copies(bb, idx, slot):
        j = jnp.minimum(sel_ref[bb * N_SLOTS + idx], n_past_blk - 1)
        page = pt_ref[bb * n_pages + j // blk_per_page]
        off = pl.multiple_of((j % blk_per_page) * SEL_BLOCK, SEL_BLOCK)
        return (pltpu.make_async_copy(psk_ref.at[page, pl.ds(off, SEL_BLOCK), :], kbuf.at[slot, idx], sem.at[slot]),
                pltpu.make_async_copy(psv_ref.at[page, pl.ds(off, SEL_BLOCK), :], vbuf.at[slot, idx], sem.at[slot]))

    def start_all(bb, slot):
        def body(idx, _):
            ck, cv = copies(bb, idx, slot)
            ck.start()
            cv.start()
            return 0
        lax.fori_loop(0, N_SLOTS, body, 0)

    @pl.when(b == 0)
    def _():
        start_all(0, 0)

    @pl.when(b + 1 < nb)
    def _():
        start_all(b + 1, (b + 1) % 2)

    slot = b % 2

    def wait_body(idx, _):
        ck, cv = copies(b, idx, slot)
        ck.wait()
        cv.wait()
        return 0
    lax.fori_loop(0, N_SLOTS, wait_body, 0)

    q8f = _pad_rows8(q_ref[0])
    q8 = q8f.astype(BF16)
    lo, hi = _lane_group_masks((8, KV_W))

    def new_scores(k_new):
        prod = q8f * k_new
        return [jnp.sum(jnp.where(lo, prod, 0.0), axis=-1, keepdims=True) * ATTN_SCALE,
                jnp.sum(jnp.where(hi, prod, 0.0), axis=-1, keepdims=True) * ATTN_SCALE]

    half = SEL_TOPN * SEL_BLOCK
    kg = kbuf[slot].reshape(N_SLOTS * SEL_BLOCK, KV_W)
    vg = vbuf[slot].reshape(N_SLOTS * SEL_BLOCK, KV_W)
    rowg = lax.broadcasted_iota(jnp.int32, kg.shape, 0) // half
    laneg = lax.broadcasted_iota(jnp.int32, kg.shape, 1) // HEAD_DIM
    keep = rowg == laneg
    kbd = jnp.where(keep, kg, 0.0).astype(BF16)
    vbd = jnp.where(keep, vg, 0.0).astype(BF16)
    s = _dot_nt(q8, kbd) * ATTN_SCALE
    info = info_ref[0].astype(F32)
    lane1 = lax.broadcasted_iota(jnp.int32, (1, LANES), 1)
    jl, vl, inc = [], [], []
    for g in range(NSA_GROUPS):
        idx8 = jnp.concatenate([info[g:g + 1]] * 8, axis=0).astype(BF16)
        val8 = jnp.concatenate([info[2 + g:3 + g]] * 8, axis=0).astype(BF16)
        jl.append(jnp.dot(idx8, eexp_ref[...], preferred_element_type=F32))
        vl.append(jnp.dot(val8, eexp_ref[...], preferred_element_type=F32))
        is_new = (info[g:g + 1] > n_past_blk - 0.5) & (info[2 + g:3 + g] > 0.5) & (lane1 < SEL_TOPN)
        inc.append(jnp.sum(jnp.where(is_new, 1.0, 0.0), axis=-1, keepdims=True))
    jlane = jnp.concatenate(jl, axis=1)
    vlane = jnp.concatenate(vl, axis=1)
    mask = (vlane > 0.5) & (jlane < n_past_blk - 0.5)
    o_s = _decode_attend(s, mask, vbd, new_scores(sknew_ref[0]), inc, svnew_ref[0], half)

    wk_old = wink_ref[0]
    wv_old = winv_ref[0]
    sw = _dot_nt(q8, _block_diag_rows(wk_old, BF16)) * ATTN_SCALE
    widx = lax.broadcasted_iota(jnp.int32, sw.shape, 1) % wbuf
    wmask = widx > wbuf - WINDOW
    always = [jnp.ones((1, 1), F32)] * NSA_GROUPS
    o_w = _decode_attend(sw, wmask, _block_diag_rows(wv_old, BF16), new_scores(wknew_ref[0]), always,
                         wvnew_ref[0], wbuf)
    rowi = lax.broadcasted_iota(jnp.int32, (wbuf, 1), 0)
    swk_ref[0] = jnp.where(rowi == wbuf - 1, wknew_ref[0], pltpu.roll(wk_old, wbuf - 1, 0))
    swv_ref[0] = jnp.where(rowi == wbuf - 1, wvnew_ref[0], pltpu.roll(wv_old, wbuf - 1, 0))

    gate8 = jnp.concatenate([gate_ref[0]] * 8, axis=0)
    rr = lax.broadcasted_iota(jnp.int32, (8, LANES), 0)
    ll = lax.broadcasted_iota(jnp.int32, (8, LANES), 1)
    o = jnp.zeros((8, KV_W), F32)
    for c, ob in enumerate((oc_ref[0], o_s, o_w)):
        base = c * NSA_HEADS + rr * NSA_GROUPS
        gsel = jnp.where((ll >= base) & (ll < base + NSA_GROUPS), gate8, 0.0)
        gh, gl = _split2(gsel)
        gx = (jnp.dot(gh, e2_ref[...], preferred_element_type=F32)
              + jnp.dot(gl, e2_ref[...], preferred_element_type=F32))
        o = o + gx * ob
    o_ref[0] = o[0:NSA_REP]


def _nsa_sample_b(sel_flat, pt_flat, qrot3, gate3, oc, info, pool_sk, pool_sv, sknew, svnew,
                  wink, winv, wknew, wvnew, nb, n_pages):
    n_pool = pool_sk.shape[0]
    wbuf = wink.shape[1]
    n_past_blk = n_pages * (PAGE_SIZE // SEL_BLOCK)
    e = np.zeros((LANES, SEL_TOPN * SEL_BLOCK), np.float32)
    for k in range(SEL_TOPN):
        e[k, k * SEL_BLOCK:(k + 1) * SEL_BLOCK] = 1.0
    eexp = jnp.asarray(e, BF16)
    e2 = np.zeros((LANES, LANES), np.float32)
    for j in range(LANES):
        e2[j, (j % NSA_GROUPS) * HEAD_DIM:(j % NSA_GROUPS + 1) * HEAD_DIM] = 1.0
    e2 = jnp.asarray(e2, BF16)
    per_b = lambda r, w: pl.BlockSpec((1, r, w), lambda b, s, p: (b, 0, 0))
    const = lambda a: pl.BlockSpec(a.shape, lambda b, s, p: (0, 0))
    anyspec = pl.BlockSpec(memory_space=pl.ANY)
    kern = functools.partial(_nsa_sample_b_kernel, nb=nb, n_pages=n_pages, n_past_blk=n_past_blk, wbuf=wbuf)
    r3 = lambda a: a.reshape(nb, 1, KV_W)
    return pl.pallas_call(
        kern,
        grid_spec=pltpu.PrefetchScalarGridSpec(
            num_scalar_prefetch=2,
            grid=(nb,),
            in_specs=[per_b(NSA_REP, LANES), per_b(1, LANES), per_b(8, LANES), per_b(8, LANES),
                      anyspec, anyspec, per_b(1, KV_W), per_b(1, KV_W),
                      per_b(wbuf, KV_W), per_b(wbuf, KV_W), per_b(1, KV_W), per_b(1, KV_W),
                      const(eexp), const(e2)],
            out_specs=[per_b(NSA_REP, LANES), per_b(wbuf, KV_W), per_b(wbuf, KV_W)],
            scratch_shapes=[pltpu.VMEM((2, N_SLOTS, SEL_BLOCK, KV_W), F32),
                            pltpu.VMEM((2, N_SLOTS, SEL_BLOCK, KV_W), F32),
                            pltpu.SemaphoreType.DMA((2,))]),
        out_shape=[jax.ShapeDtypeStruct((nb, NSA_REP, LANES), F32),
                   jax.ShapeDtypeStruct((nb, wbuf, KV_W), F32),
                   jax.ShapeDtypeStruct((nb, wbuf, KV_W), F32)],
        compiler_params=_cparams(("arbitrary",)),
        name="nsa_sample_attend",
    )(sel_flat, pt_flat, qrot3, gate3, oc, info,
      pool_sk.reshape(n_pool, PAGE_SIZE, KV_W), pool_sv.reshape(n_pool, PAGE_SIZE, KV_W),
      r3(sknew), r3(svnew), wink, winv, r3(wknew), r3(wvnew), eexp, e2)


DIFF_PAGES_PER_STEP = 8


def _diff_sample_kernel(pt_ref, lam_ref, dq_ref, dknew_ref, dvnew_ref, g_ref, pk_ref, pv_ref, o_ref,
                        kbuf, vbuf, sem, m_ref, l_ref, acc_ref, *, nb, n_pages, lam_init):
    b = pl.program_id(0)
    c = pl.program_id(1)
    nc = n_pages // DIFF_PAGES_PER_STEP
    step = b * nc + c

    def copies(st, p, slot):
        bb = st // nc
        cc = st % nc
        page = pt_ref[bb * n_pages + cc * DIFF_PAGES_PER_STEP + p]
        dst = pl.ds(p * PAGE_SIZE, PAGE_SIZE)
        return (pltpu.make_async_copy(pk_ref.at[page], kbuf.at[slot, dst, :], sem.at[slot]),
                pltpu.make_async_copy(pv_ref.at[page], vbuf.at[slot, dst, :], sem.at[slot]))

    def start_all(st, slot):
        for p in range(DIFF_PAGES_PER_STEP):
            ck, cv = copies(st, p, slot)
            ck.start()
            cv.start()

    @pl.when(step == 0)
    def _():
        start_all(0, 0)

    @pl.when(step + 1 < nb * nc)
    def _():
        start_all(step + 1, (step + 1) % 2)

    slot = step % 2
    for p in range(DIFF_PAGES_PER_STEP):
        ck, cv = copies(step, p, slot)
        ck.wait()
        cv.wait()

    @pl.when(c == 0)
    def _():
        m_ref[...] = jnp.full(m_ref.shape, NEG_INF, F32)
        l_ref[...] = jnp.zeros(l_ref.shape, F32)
        acc_ref[...] = jnp.zeros(acc_ref.shape, F32)

    rr = lax.broadcasted_iota(jnp.int32, (8, DIFF_WIDTH), 0)
    ll = lax.broadcasted_iota(jnp.int32, (8, DIFF_WIDTH), 1)
    own = (ll // HEAD_DIM) == rr
    q8f = jnp.where(own, jnp.concatenate([dq_ref[0]] * 8, axis=0), 0.0)
    s = _dot_nt(q8f.astype(BF16), kbuf[slot].astype(BF16)) * ATTN_SCALE
    m_old = m_ref[...]
    m_new = jnp.maximum(m_old, jnp.max(s, axis=-1, keepdims=True))
    alpha = jnp.exp(m_old - m_new)
    p = jnp.exp(s - m_new)
    l_ref[...] = alpha * l_ref[...] + jnp.sum(p, axis=-1, keepdims=True)
    m_ref[...] = m_new
    acc_ref[...] = alpha * acc_ref[...] + jnp.dot(p.astype(BF16), vbuf[slot].astype(BF16),
                                                  preferred_element_type=F32)

    @pl.when(c == nc - 1)
    def _():
        s_new = jnp.sum(q8f * dknew_ref[0], axis=-1, keepdims=True) * ATTN_SCALE
        m_old = m_ref[...]
        m_new = jnp.maximum(m_old, s_new)
        alpha = jnp.exp(m_old - m_new)
        p_new = jnp.exp(s_new - m_new)
        l = alpha * l_ref[...] + p_new
        o = (alpha * acc_ref[...] + p_new * dvnew_ref[0]) / l
        lam = _lambda_value(lam_ref, lam_init)
        for h in range(DIFF_HEADS):
            cs = slice(h * DIFF_VDIM, (h + 1) * DIFF_VDIM)
            o_ref[0, :, cs] = _diff_merge_rows(o[2 * h:2 * h + 1, cs], o[2 * h + 1:2 * h + 2, cs], lam,
                                               g_ref[...], lam_init)


def _diff_sample(pt_flat, lam4, dq, dknew, dvnew, subln_g, pool_k, pool_v, nb, n_pages, lam_init):
    n_pool = pool_k.shape[0]
    nc = n_pages // DIFF_PAGES_PER_STEP
    keys = DIFF_PAGES_PER_STEP * PAGE_SIZE
    per_b = pl.BlockSpec((1, 1, DIFF_WIDTH), lambda b, c, pt: (b, 0, 0))
    anyspec = pl.BlockSpec(memory_space=pl.ANY)
    kern = functools.partial(_diff_sample_kernel, nb=nb, n_pages=n_pages, lam_init=lam_init)
    r3 = lambda a: a.reshape(nb, 1, DIFF_WIDTH)
    return pl.pallas_call(
        kern,
        grid_spec=pltpu.PrefetchScalarGridSpec(
            num_scalar_prefetch=1,
            grid=(nb, nc),
            in_specs=[pl.BlockSpec((4, HEAD_DIM), lambda b, c, pt: (0, 0)), per_b, per_b, per_b,
                      pl.BlockSpec((1, DIFF_VDIM), lambda b, c, pt: (0, 0)), anyspec, anyspec],
            out_specs=per_b,
            scratch_shapes=[pltpu.VMEM((2, keys, DIFF_WIDTH), F32), pltpu.VMEM((2, keys, DIFF_WIDTH), F32),
                            pltpu.SemaphoreType.DMA((2,)),
                            pltpu.VMEM((8, 1), F32), pltpu.VMEM((8, 1), F32), pltpu.VMEM((8, DIFF_WIDTH), F32)]),
        out_shape=jax.ShapeDtypeStruct((nb, 1, DIFF_WIDTH), F32),
        compiler_params=_cparams(("arbitrary", "arbitrary")),
        name="diff_sample",
    )(pt_flat, lam4, r3(dq), r3(dknew), r3(dvnew), subln_g,
      pool_k.reshape(n_pool, PAGE_SIZE, DIFF_WIDTH), pool_v.reshape(n_pool, PAGE_SIZE, DIFF_WIDTH))


def _prep_w_in(w):
    splits = (NSA_WIDTH, KV_W, KV_W, KV_W, KV_W, KV_W, KV_W, NSA_HEADS * 3, DIFF_WIDTH, DIFF_WIDTH, DIFF_WIDTH)
    cuts = [int(c) for c in np.cumsum(splits)[:-1]]
    q, ck, cv, sk, sv, wk, wv, gate, dq, dk, dv = jnp.split(w, cuts, axis=1)
    q = q.reshape(D_MODEL, NSA_GROUPS, NSA_REP, HEAD_DIM).transpose(0, 2, 1, 3).reshape(D_MODEL, NSA_WIDTH)
    gate = gate.reshape(D_MODEL, NSA_GROUPS, NSA_REP, 3).transpose(0, 3, 2, 1).reshape(D_MODEL, NSA_HEADS * 3)
    gate = jnp.pad(gate, ((0, 0), (0, LANES - NSA_HEADS * 3)))
    return jnp.concatenate([q, ck, cv, sk, sv, wk, wv, dq, dk, dv, gate], axis=1).astype(BF16)


def _prep_w_out(w):
    won = w[:NSA_WIDTH].reshape(NSA_GROUPS, NSA_REP, HEAD_DIM, D_MODEL).transpose(1, 0, 2, 3)
    return won.reshape(NSA_WIDTH, D_MODEL).astype(BF16), w[NSA_WIDTH:].astype(BF16)


def _prep_ffn(p):
    won, wod = _prep_w_out(p["w_out"][0])
    return (won, wod, p["ffn_norm"], p["w_up"][0].astype(BF16), p["conv_w"][0], p["conv_b"],
            p["w_down"][0].astype(BF16), p["final_norm"].reshape(1, D_MODEL))


def _rope_tables(pos):
    half = HEAD_DIM // 2
    inv = 1.0 / (ROPE_THETA ** (jnp.arange(half, dtype=F32) / half))
    ang = pos.astype(F32)[:, None] * inv[None, :]
    cos, sin = jnp.cos(ang), jnp.sin(ang)
    return jnp.tile(cos, (1, 4)), jnp.tile(jnp.concatenate([-sin, sin], axis=1), (1, 2))


def _prompt_group(xp, attn_norm, w_in_r, cw_k, cw_v, lam4, subln_g, ffn_ws, lam_init):
    B, T, _ = xp.shape
    x2d = xp.reshape(B * T, D_MODEL)
    tm = min(256, T)
    cos_t, sin_t = _rope_tables(jnp.arange(T, dtype=jnp.int32))
    qraw, qrot, ck, cv, sk, sv, wk, wv, gate, dq, dk, dv = _project(x2d, attn_norm, w_in_r, cos_t, sin_t, tm)
    kc = _compress_prompt(ck.reshape(B, T, KV_W), cw_k)
    vc = _compress_prompt(cv.reshape(B, T, KV_W), cw_v)
    o_nsa = _nsa_prompt(qraw, qrot, gate, kc, vc, sk, sv, wk, wv, B, T, tq=128, tk=128)
    o_diff = _diff_prompt(lam4, dq, dk, dv, subln_g, B, T, tq=256, tk=256, lam_init=lam_init)
    y, conv = _ffn_seq(x2d, o_nsa, o_diff, ffn_ws, B, T, tm)
    kv = lambda a: a.reshape(1, B, T, NSA_GROUPS, HEAD_DIM)
    wb = min(WINDOW, T)
    return (y.reshape(B, T, D_MODEL), kv(ck), kv(cv), kv(sk), kv(sv),
            dk.reshape(1, B, T, DIFF_HEADS, 2, HEAD_DIM), dv.reshape(1, B, T, DIFF_HEADS, DIFF_VDIM),
            kv(wk)[:, :, T - wb:], kv(wv)[:, :, T - wb:], conv[None])


def _sample_group(xs, caches, page_table, attn_norm, w_in_r, cw_k, cw_v, lam4, subln_g, ffn_ws, lam_init):
    pool_ck, pool_cv, pool_sk, pool_sv, pool_dk, pool_dv, win_k, win_v, conv_state = caches
    nb = xs.shape[0]
    n_pages = page_table.shape[1]
    past = n_pages * PAGE_SIZE
    t_pad = -(-(past + 1) // SEL_BLOCK) * SEL_BLOCK
    n_cmp = t_pad // CMP_STRIDE - CMP_BLOCK // CMP_STRIDE + 1
    n_sel = t_pad // SEL_BLOCK
    rows = -(-(t_pad // CMP_STRIDE) // 8) * 8
    rows_out = -(-rows // LANES) * LANES
    pt_flat = page_table.reshape(-1)
    x2d = xs.reshape(nb, D_MODEL)
    pos = jnp.full((nb,), past, dtype=jnp.int32)
    cos_t, sin_t = _rope_tables(pos)
    qraw, qrot, ck, cv, sk, sv, wk, wv, gate, dq, dk, dv = _project(x2d, attn_norm, w_in_r, cos_t, sin_t, nb)
    kc = _compress_sample(pool_ck, pt_flat, ck, cw_k, nb, n_pages, rows, rows_out)
    vc = _compress_sample(pool_cv, pt_flat, cv, cw_v, nb, n_pages, rows, rows_out)
    o_c, info = _nsa_sample_a(qraw.reshape(nb, NSA_REP, LANES), kc, vc, past, n_cmp, n_sel)
    sel_flat = info[:, 0:NSA_GROUPS, 0:SEL_TOPN].reshape(-1)
    wbuf = win_k.shape[1]
    o_nsa, swk, swv = _nsa_sample_b(sel_flat, pt_flat, qrot.reshape(nb, NSA_REP, LANES),
                                    gate.reshape(nb, 1, LANES), o_c, info, pool_sk, pool_sv, sk, sv,
                                    win_k.reshape(nb, wbuf, KV_W), win_v.reshape(nb, wbuf, KV_W), wk, wv,
                                    nb, n_pages)
    o_diff = _diff_sample(pt_flat, lam4, dq, dk, dv, subln_g, pool_dk, pool_dv, nb, n_pages, lam_init)
    y, u = _ffn_step(x2d, o_nsa.reshape(nb, NSA_WIDTH), o_diff.reshape(nb, DIFF_WIDTH), ffn_ws,
                     conv_state[:, 0], conv_state[:, 1])
    kv = lambda a: a.reshape(1, nb, 1, NSA_GROUPS, HEAD_DIM)
    return (y.reshape(nb, 1, D_MODEL), kv(ck), kv(cv), kv(sk), kv(sv),
            dk.reshape(1, nb, 1, DIFF_HEADS, 2, HEAD_DIM), dv.reshape(1, nb, 1, DIFF_HEADS, DIFF_VDIM),
            swk.reshape(1, nb, wbuf, NSA_GROUPS, HEAD_DIM), swv.reshape(1, nb, wbuf, NSA_GROUPS, HEAD_DIM),
            jnp.stack([conv_state[:, 1], u], axis=1)[None])


def kernel(x_prompt, x_sample, cache_cmp_k, cache_cmp_v, cache_sel_k, cache_sel_v, cache_diff_k, cache_diff_v,
           cache_win_k, cache_win_v, state_ffn_conv, page_table, attn_norm, w_in, cmp_pos_k, cmp_w1_k, cmp_w2_k,
           cmp_pos_v, cmp_w1_v, cmp_w2_v, lambda_q1, lambda_k1, lambda_q2, lambda_k2, subln_g, w_out, ffn_norm,
           w_up, conv_w, conv_b, w_down, final_norm):
    assert w_in.shape[0] == 1 and x_sample.shape[1] == 1, "one layer, one new token per sequence"
    lam_init = 0.8 - 0.6 * math.exp(0.0)
    w_in_r = _prep_w_in(w_in[0])
    cw_k = _compress_weights(cmp_pos_k[0], cmp_w1_k[0], cmp_w2_k[0])
    cw_v = _compress_weights(cmp_pos_v[0], cmp_w1_v[0], cmp_w2_v[0])
    lam4 = jnp.stack([lambda_q1[0], lambda_k1[0], lambda_q2[0], lambda_k2[0]])
    ffn_ws = _prep_ffn(dict(w_out=w_out, ffn_norm=ffn_norm, w_up=w_up, conv_w=conv_w, conv_b=conv_b,
                            w_down=w_down, final_norm=final_norm))
    p = _prompt_group(x_prompt, attn_norm, w_in_r, cw_k, cw_v, lam4, subln_g, ffn_ws, lam_init)
    caches = (cache_cmp_k[0], cache_cmp_v[0], cache_sel_k[0], cache_sel_v[0], cache_diff_k[0], cache_diff_v[0],
              cache_win_k[0], cache_win_v[0], state_ffn_conv[0])
    s = _sample_group(x_sample, caches, page_table, attn_norm, w_in_r, cw_k, cw_v, lam4, subln_g, ffn_ws,
                      lam_init)
    return (p[0], s[0]) + tuple(p[1:]) + tuple(s[1:])
```

```python
import functools
import math

import numpy as np
import jax
import jax.numpy as jnp
from jax import lax
from jax.experimental import pallas as pl
from jax.experimental.pallas import tpu as pltpu

F32 = jnp.float32
BF16 = jnp.bfloat16

D_MODEL = 1024
HEAD_DIM = 64
NSA_WIDTH = 512
NSA_HEADS = 8
NSA_GROUPS = 2
NSA_REP = 4
KV_W = NSA_GROUPS * HEAD_DIM
CMP_BLOCK = 32
CMP_STRIDE = 16
CMP_HIDDEN = 128
SEL_BLOCK = 64
SEL_TOPN = 16
WINDOW = 512
FORCE_BONUS = 1.0e4
DIFF_WIDTH = 512
DIFF_VDIM = 128
DIFF_HEADS = 4
D_FF = 2816
CONV_W = 3
ROPE_THETA = 10000.0
RMS_EPS = 1e-6
ATTN_SCALE = 1.0 / math.sqrt(HEAD_DIM)
SCALE_LOG2E = ATTN_SCALE * math.log2(math.e)
PAGE_SIZE = 128

LANES = 128
VMEM_LIMIT = 48 * 1024 * 1024
NEG_INF = float("-inf")

C_Q = 0
C_CK, C_CV, C_SK, C_SV, C_WK, C_WV = 512, 640, 768, 896, 1024, 1152
C_DQ, C_DK, C_DV = 1280, 1792, 2304
C_GATE = 2816
W_IN_COLS = 2944


def _cparams(sem):
    return pltpu.CompilerParams(dimension_semantics=sem, vmem_limit_bytes=VMEM_LIMIT)


def _rope128(v, cos, sin_signed):
    lane = lax.broadcasted_iota(jnp.int32, v.shape, 1)
    first = (lane % HEAD_DIM) < (HEAD_DIM // 2)
    partner = jnp.where(first, pltpu.roll(v, LANES - HEAD_DIM // 2, 1), pltpu.roll(v, HEAD_DIM // 2, 1))
    return v * cos + partner * sin_signed


def _proj_kernel(x_ref, g_ref, w_ref, cos_ref, sin_ref,
                 qraw_ref, qrot_ref, ck_ref, cv_ref, sk_ref, sv_ref, wk_ref, wv_ref,
                 gate_ref, dq_ref, dk_ref, dv_ref):
    x = x_ref[...]
    h = x * lax.rsqrt(jnp.mean(x * x, axis=-1, keepdims=True) + RMS_EPS) * g_ref[...]
    z = jnp.dot(h.astype(BF16), w_ref[...], preferred_element_type=F32)
    cos = cos_ref[...]
    sin = sin_ref[...]

    def put(ref, col, width, rope):
        for c in range(width // LANES):
            v = z[:, col + c * LANES: col + (c + 1) * LANES]
            ref[:, c * LANES:(c + 1) * LANES] = _rope128(v, cos, sin) if rope else v

    put(qraw_ref, C_Q, NSA_WIDTH, False)
    put(qrot_ref, C_Q, NSA_WIDTH, True)
    put(ck_ref, C_CK, KV_W, False)
    put(cv_ref, C_CV, KV_W, False)
    put(sk_ref, C_SK, KV_W, True)
    put(sv_ref, C_SV, KV_W, False)
    put(wk_ref, C_WK, KV_W, True)
    put(wv_ref, C_WV, KV_W, False)
    put(dq_ref, C_DQ, DIFF_WIDTH, True)
    put(dk_ref, C_DK, DIFF_WIDTH, True)
    put(dv_ref, C_DV, DIFF_WIDTH, False)
    gate_ref[...] = jax.nn.sigmoid(z[:, C_GATE:C_GATE + LANES])


def _project(x2d, norm_g, w_in_r, cos_t, sin_t, tm):
    M = x2d.shape[0]
    nt = cos_t.shape[0] // tm
    widths = (NSA_WIDTH, NSA_WIDTH, KV_W, KV_W, KV_W, KV_W, KV_W, KV_W, LANES,
              DIFF_WIDTH, DIFF_WIDTH, DIFF_WIDTH)
    row = lambda i: (i, 0)
    return pl.pallas_call(
        _proj_kernel,
        grid=(M // tm,),
        in_specs=[pl.BlockSpec((tm, D_MODEL), row),
                  pl.BlockSpec((1, D_MODEL), lambda i: (0, 0)),
                  pl.BlockSpec((D_MODEL, W_IN_COLS), lambda i: (0, 0)),
                  pl.BlockSpec((tm, LANES), lambda i: (i % nt, 0)),
                  pl.BlockSpec((tm, LANES), lambda i: (i % nt, 0))],
        out_specs=[pl.BlockSpec((tm, w), row) for w in widths],
        out_shape=[jax.ShapeDtypeStruct((M, w), F32) for w in widths],
        compiler_params=_cparams(("arbitrary",)),
        name="in_proj",
    )(x2d, norm_g, w_in_r, cos_t, sin_t)


def _gelu(x):
    return 0.5 * x * (1.0 + jnp.tanh(math.sqrt(2.0 / math.pi) * (x + 0.044715 * (x * x * x))))


def _compress_rows(c, w1cat, posb, w1n, w2bd):
    R = c.shape[0]
    r = jnp.dot(c.astype(BF16), w1cat, preferred_element_type=F32)
    bias = jnp.dot(posb, w1n, preferred_element_type=F32)[0:1]
    bias2 = jnp.concatenate([bias, bias], axis=1)
    hid = r[:, :2 * CMP_HIDDEN] + pltpu.roll(r[:, 2 * CMP_HIDDEN:], R - 1, 0) + bias2
    return jnp.dot(_gelu(hid).astype(BF16), w2bd, preferred_element_type=F32)


def _compress_kernel(c_ref, w1cat_ref, posb_ref, w1n_ref, w2bd_ref, o_ref):
    o_ref[0] = _compress_rows(c_ref[0], w1cat_ref[...], posb_ref[...], w1n_ref[...], w2bd_ref[...])


def _compress_weights(pos, w1, w2):
    w1r = w1.reshape(2, CMP_STRIDE, HEAD_DIM, CMP_HIDDEN)
    eye = jnp.eye(NSA_GROUPS, dtype=w1.dtype)
    big = jnp.einsum('isdh,gk->isgdkh', w1r, eye).reshape(2, CMP_STRIDE * KV_W, NSA_GROUPS * CMP_HIDDEN)
    w1cat = jnp.concatenate([big[0], big[1]], axis=1).astype(BF16)
    w2bd = jnp.einsum('hd,gk->ghkd', w2, eye).reshape(NSA_GROUPS * CMP_HIDDEN, KV_W).astype(BF16)
    posb = jnp.broadcast_to(pos.reshape(1, CMP_BLOCK * HEAD_DIM), (8, CMP_BLOCK * HEAD_DIM)).astype(BF16)
    return w1cat, posb, w1.astype(BF16), w2bd


def _compress_prompt(x, cw):
    B, T, _ = x.shape
    R = T // CMP_STRIDE
    c = x.reshape(B, R, CMP_STRIDE * KV_W)
    full = lambda a: pl.BlockSpec(a.shape, lambda b: (0,) * a.ndim)
    return pl.pallas_call(
        _compress_kernel,
        grid=(B,),
        in_specs=[pl.BlockSpec((1, R, CMP_STRIDE * KV_W), lambda b: (b, 0, 0))] + [full(a) for a in cw],
        out_specs=pl.BlockSpec((1, R, KV_W), lambda b: (b, 0, 0)),
        out_shape=jax.ShapeDtypeStruct((B, R, KV_W), F32),
        compiler_params=_cparams(("arbitrary",)),
        name="compress_prompt",
    )(c, *cw)


def _lane_group_masks(shape):
    lane = lax.broadcasted_iota(jnp.int32, shape, 1)
    lo = lane < HEAD_DIM
    return lo, jnp.logical_not(lo)


def _block_diag_rows(x, dtype):
    lo, hi = _lane_group_masks(x.shape)
    return jnp.concatenate([jnp.where(lo, x, 0.0), jnp.where(hi, x, 0.0)], axis=0).astype(dtype)


def _dot_nt(a, b):
    return lax.dot_general(a, b, (((1,), (1,)), ((), ())), preferred_element_type=F32)


def _flash_step_t(s_t, mask_t, m_ref, l_ref, acc_ref, v_t, tk):
    ps = []
    for g in range(NSA_GROUPS):
        sg = s_t[g * tk:(g + 1) * tk]
        if mask_t is not None:
            sg = jnp.where(mask_t[g * tk:(g + 1) * tk], sg, NEG_INF)
        m_old = m_ref[g]
        m_new = jnp.maximum(m_old, jnp.max(sg, axis=0, keepdims=True))
        m_safe = jnp.where(m_new == NEG_INF, 0.0, m_new)
        alpha = jnp.exp2(m_old - m_safe)
        p = jnp.exp2(sg - m_safe)
        l_ref[g] = alpha * l_ref[g] + jnp.sum(p, axis=0, keepdims=True)
        m_ref[g] = m_new
        ps.append(p.astype(BF16))
        rs = slice(g * HEAD_DIM, (g + 1) * HEAD_DIM)
        acc_ref[rs, :] = alpha * acc_ref[rs, :]
    pv = jnp.dot(v_t, jnp.concatenate(ps, axis=0), preferred_element_type=F32)
    acc_ref[...] = acc_ref[...] + pv


def _finish_t(acc_ref, l_ref):
    parts = [acc_ref[g * HEAD_DIM:(g + 1) * HEAD_DIM, :] * (1.0 / jnp.maximum(l_ref[g], 1e-30))
             for g in range(NSA_GROUPS)]
    return jnp.concatenate(parts, axis=0)


def _reset(m_ref, l_ref, acc_ref):
    m_ref[...] = jnp.full(m_ref.shape, NEG_INF, F32)
    l_ref[...] = jnp.zeros(l_ref.shape, F32)
    acc_ref[...] = jnp.zeros(acc_ref.shape, F32)


def _queries_t(q, tq):
    return jnp.concatenate([(q[:, r * LANES:(r + 1) * LANES] * SCALE_LOG2E).T for r in range(NSA_REP)],
                           axis=1).astype(BF16)


def _outputs_from_t(o_t, tq):
    return jnp.concatenate([o_t[:, r * tq:(r + 1) * tq].T for r in range(NSA_REP)], axis=1)


def _split2(x):
    hi = x.astype(BF16)
    return hi, (x - hi.astype(F32)).astype(BF16)


def _select_blocks(score_t, q_pos_row, n_sel):
    n_pad = score_t.shape[0] // NSA_GROUPS
    tq = score_t.shape[1]
    out = []
    blk = lax.broadcasted_iota(jnp.int32, (n_pad, tq), 0)
    cur = q_pos_row // SEL_BLOCK
    forced = (blk == 0) | (blk == cur) | (blk == cur - 1)
    ok = (blk * SEL_BLOCK <= q_pos_row) & (blk < n_sel)
    for g in range(NSA_GROUPS):
        sc = score_t[g * n_pad:(g + 1) * n_pad]
        sc = jnp.where(ok, sc + jnp.where(forced, FORCE_BONUS, 0.0), NEG_INF)
        cnt = jnp.zeros((n_pad, tq), F32)
        for i in range(n_pad):
            row = sc[i:i + 1]
            cnt = cnt + jnp.where(blk > i, jnp.where(row >= sc, 1.0, 0.0), jnp.where(row > sc, 1.0, 0.0))
        out.append(jnp.where((cnt < float(SEL_TOPN)) & ok, 1.0, 0.0))
    return jnp.concatenate(out, axis=0)


def _nsa_prompt_kernel(qraw_ref, qrot_ref, gate_ref, kc_ref, vc_ref, sk_ref, sv_ref, wk_ref, wv_ref,
                       ovt_ref, gexp_ref, o_ref,
                       skbd, svt, wkbd, wvt, kcbd, vct, sel_ref, m_ref, l_ref, acc_ref, *, tq, tk, seq):
    i = pl.program_id(1)
    n_kt = seq // tk
    n_sel = seq // SEL_BLOCK
    n_cmp = kc_ref.shape[1]
    n_pad = sel_ref.shape[0] // NSA_GROUPS
    L = NSA_REP * tq
    blk_per_tile = tk // SEL_BLOCK

    @pl.when(i == 0)
    def _():
        for ksrc, kdst, vsrc, vdst in ((sk_ref, skbd, sv_ref, svt), (wk_ref, wkbd, wv_ref, wvt)):
            def fill(j, _, ksrc=ksrc, kdst=kdst, vsrc=vsrc, vdst=vdst):
                rows = pl.ds(pl.multiple_of(j * tk, tk), tk)
                kdst[j] = _block_diag_rows(ksrc[0, rows, :], BF16)
                vdst[j] = _block_diag_rows(vsrc[0, rows, :], F32).T.astype(BF16)
                return 0
            lax.fori_loop(0, n_kt, fill, 0)
        kcbd[...] = _block_diag_rows(kc_ref[0], BF16)
        vct[...] = _block_diag_rows(vc_ref[0], F32).T.astype(BF16)

    q0 = i * tq
    q_pos_row = q0 + lax.broadcasted_iota(jnp.int32, (1, tq), 1)
    q_pos = jnp.concatenate([q_pos_row] * NSA_REP, axis=1)

    s = jnp.dot(kcbd[...], _queries_t(qraw_ref[...], tq), preferred_element_type=F32)
    kidx = lax.broadcasted_iota(jnp.int32, (n_cmp, 1), 0)
    cmask = (kidx * CMP_STRIDE + (CMP_BLOCK - 1) <= q_pos) & (kidx < n_cmp - 1)
    pcs = []
    for g in range(NSA_GROUPS):
        sg = jnp.where(cmask, s[g * n_cmp:(g + 1) * n_cmp], NEG_INF)
        m = jnp.max(sg, axis=0, keepdims=True)
        m = jnp.where(m == NEG_INF, 0.0, m)
        p = jnp.exp2(sg - m)
        pcs.append(p * (1.0 / jnp.maximum(jnp.sum(p, axis=0, keepdims=True), 1e-30)))
    pc = jnp.concatenate(pcs, axis=0)
    o_c = jnp.dot(vct[...], pc.astype(BF16), preferred_element_type=F32)
    imp = pc[:, 0:tq] + pc[:, tq:2 * tq] + pc[:, 2 * tq:3 * tq] + pc[:, 3 * tq:4 * tq]
    ih, il = _split2(imp)
    score = (jnp.dot(ovt_ref[...], ih, preferred_element_type=F32)
             + jnp.dot(ovt_ref[...], il, preferred_element_type=F32))
    sel_ref[...] = _select_blocks(score, q_pos_row, n_sel)

    qr = _queries_t(qrot_ref[...], tq)
    kiota = lax.broadcasted_iota(jnp.int32, (tk, 1), 0)

    def sel_mask(j):
        parts = []
        for g in range(NSA_GROUPS):
            rows = [jnp.broadcast_to(sel_ref[pl.ds(g * n_pad + j * blk_per_tile + t, 1), :], (SEL_BLOCK, tq))
                    for t in range(blk_per_tile)]
            parts.append(jnp.concatenate(rows, axis=0) if blk_per_tile > 1 else rows[0])
        m1 = jnp.concatenate(parts, axis=0) > 0.5
        return jnp.concatenate([m1] * NSA_REP, axis=1)

    def sel_step(causal):
        def step(j, _):
            s = jnp.dot(skbd[j], qr, preferred_element_type=F32)
            mask = sel_mask(j)
            if causal:
                c1 = (j * tk + kiota) <= q_pos
                mask = mask & jnp.concatenate([c1, c1], axis=0)
            _flash_step_t(s, mask, m_ref, l_ref, acc_ref, svt[j], tk)
            return 0
        return step

    _reset(m_ref, l_ref, acc_ref)
    n_full = q0 // tk
    lax.fori_loop(0, n_full, sel_step(False), 0)
    lax.fori_loop(n_full, (q0 + tq - 1) // tk + 1, sel_step(True), 0)
    o_s = _finish_t(acc_ref, l_ref)

    _reset(m_ref, l_ref, acc_ref)

    def win_step(j, _):
        s = jnp.dot(wkbd[j], qr, preferred_element_type=F32)
        kpos = j * tk + kiota
        c1 = (kpos <= q_pos) & (kpos > q_pos - WINDOW)
        _flash_step_t(s, jnp.concatenate([c1, c1], axis=0), m_ref, l_ref, acc_ref, wvt[j], tk)
        return 0

    lax.fori_loop(jnp.maximum(q0 - WINDOW + 1, 0) // tk, (q0 + tq - 1) // tk + 1, win_step, 0)
    o_w = _finish_t(acc_ref, l_ref)

    gh, gl = _split2(gate_ref[...])
    gx = (jnp.dot(gh, gexp_ref[...], preferred_element_type=F32)
          + jnp.dot(gl, gexp_ref[...], preferred_element_type=F32))
    o = (gx[:, 0:NSA_WIDTH] * _outputs_from_t(o_c, tq)
         + gx[:, NSA_WIDTH:2 * NSA_WIDTH] * _outputs_from_t(o_s, tq)
         + gx[:, 2 * NSA_WIDTH:3 * NSA_WIDTH] * _outputs_from_t(o_w, tq))
    o_ref[...] = o


def _overlap_matrix(n_cmp_rows, n_sel_pad):
    cs = np.arange(n_cmp_rows)[:, None] * CMP_STRIDE
    ss = np.arange(n_sel_pad)[None, :] * SEL_BLOCK
    ov = ((cs < ss + SEL_BLOCK) & (cs + CMP_BLOCK > ss)).astype(np.float32)
    z = np.zeros_like(ov)
    return jnp.asarray(np.block([[ov, z], [z, ov]]), BF16)


def _gate_expand_matrix():
    e = np.zeros((LANES, 3 * NSA_WIDTH), np.float32)
    for c in range(3):
        for r in range(NSA_REP):
            for g in range(NSA_GROUPS):
                col = c * NSA_WIDTH + r * LANES + g * HEAD_DIM
                e[c * NSA_HEADS + r * NSA_GROUPS + g, col:col + HEAD_DIM] = 1.0
    return jnp.asarray(e, BF16)


def _nsa_prompt(qraw, qrot, gate, kc, vc, sk, sv, wk, wv, B, T, tq, tk):
    nq = T // tq
    n_cmp = kc.shape[1]
    n_sel_pad = max(T // SEL_BLOCK, HEAD_DIM)
    ovt = _overlap_matrix(n_cmp, n_sel_pad).T
    gexp = _gate_expand_matrix()
    qspec = lambda w: pl.BlockSpec((tq, w), lambda b, i: (b * nq + i, 0))
    kvspec = lambda n: pl.BlockSpec((1, n, KV_W), lambda b, i: (b, 0, 0))
    const = lambda a: pl.BlockSpec(a.shape, lambda b, i: (0, 0))
    L = NSA_REP * tq
    n_kt = T // tk
    kern = functools.partial(_nsa_prompt_kernel, tq=tq, tk=tk, seq=T)
    kv3 = lambda a: a.reshape(B, T, KV_W)
    return pl.pallas_call(
        kern,
        grid=(B, nq),
        in_specs=[qspec(NSA_WIDTH), qspec(NSA_WIDTH), qspec(LANES), kvspec(n_cmp), kvspec(n_cmp),
                  kvspec(T), kvspec(T), kvspec(T), kvspec(T), const(ovt), const(gexp)],
        out_specs=qspec(NSA_WIDTH),
        out_shape=jax.ShapeDtypeStruct((B * T, NSA_WIDTH), F32),
        scratch_shapes=[pltpu.VMEM((n_kt, 2 * tk, KV_W), BF16), pltpu.VMEM((n_kt, KV_W, 2 * tk), BF16),
                        pltpu.VMEM((n_kt, 2 * tk, KV_W), BF16), pltpu.VMEM((n_kt, KV_W, 2 * tk), BF16),
                        pltpu.VMEM((2 * n_cmp, KV_W), BF16), pltpu.VMEM((KV_W, 2 * n_cmp), BF16),
                        pltpu.VMEM((NSA_GROUPS * n_sel_pad, tq), F32),
                        pltpu.VMEM((NSA_GROUPS, 1, L), F32), pltpu.VMEM((NSA_GROUPS, 1, L), F32),
                        pltpu.VMEM((KV_W, L), F32)],
        compiler_params=_cparams(("arbitrary", "arbitrary")),
        name="nsa_prompt",
    )(qraw, qrot, gate, kc, vc, kv3(sk), kv3(sv), kv3(wk), kv3(wv), ovt, gexp)


def _lambda_value(lam_ref, lam_init):
    a = jnp.sum(lam_ref[0:1] * lam_ref[1:2], axis=-1, keepdims=True)
    b = jnp.sum(lam_ref[2:3] * lam_ref[3:4], axis=-1, keepdims=True)
    return jnp.exp(a) - jnp.exp(b) + lam_init


def _diff_merge_rows(o0, o1, lam, g, lam_init):
    od = o0 - lam * o1
    od = od * lax.rsqrt(jnp.mean(od * od, axis=-1, keepdims=True) + RMS_EPS) * g
    return od * (1.0 - lam_init)


def _diff_prompt_kernel(lam_ref, dq_ref, dk_ref, dv_ref, gcol_ref, o_ref,
                        kbf, vbf, m_ref, l_ref, acc_ref, *, tq, tk, lam_init):
    i = pl.program_id(2)
    n_kt = kbf.shape[0] // tk

    @pl.when(i == 0)
    def _():
        kbf[...] = dk_ref[...].astype(BF16)

        def fill(j, _):
            vbf[j] = dv_ref[pl.ds(pl.multiple_of(j * tk, tk), tk), :].T.astype(BF16)
            return 0
        lax.fori_loop(0, n_kt, fill, 0)

    q = dq_ref[...] * SCALE_LOG2E
    lo, hi = _lane_group_masks(q.shape)
    q2t = jnp.concatenate([jnp.where(lo, q, 0.0), jnp.where(hi, q, 0.0)], axis=0).T.astype(BF16)
    q0 = i * tq
    q_pos = q0 + lax.broadcasted_iota(jnp.int32, (1, 2 * tq), 1) % tq
    kiota = lax.broadcasted_iota(jnp.int32, (tk, 1), 0)
    m_ref[...] = jnp.full(m_ref.shape, NEG_INF, F32)
    l_ref[...] = jnp.zeros(l_ref.shape, F32)
    acc_ref[...] = jnp.zeros(acc_ref.shape, F32)

    def make_step(causal):
        def step(j, _):
            k0 = pl.multiple_of(j * tk, tk)
            s = jnp.dot(kbf[pl.ds(k0, tk), :], q2t, preferred_element_type=F32)
            if causal:
                s = jnp.where((k0 + kiota) <= q_pos, s, NEG_INF)
            m_old = m_ref[...]
            m_new = jnp.maximum(m_old, jnp.max(s, axis=0, keepdims=True))
            alpha = jnp.exp2(m_old - m_new)
            p = jnp.exp2(s - m_new)
            l_ref[...] = alpha * l_ref[...] + jnp.sum(p, axis=0, keepdims=True)
            m_ref[...] = m_new
            acc_ref[...] = alpha * acc_ref[...] + jnp.dot(vbf[j], p.astype(BF16), preferred_element_type=F32)
            return 0
        return step

    n_full = q0 // tk
    lax.fori_loop(0, n_full, make_step(False), 0)
    lax.fori_loop(n_full, (q0 + tq - 1) // tk + 1, make_step(True), 0)
    o = acc_ref[...] * (1.0 / l_ref[...])
    lam = _lambda_value(lam_ref, lam_init)
    od = o[:, :tq] - lam * o[:, tq:]
    od = od * lax.rsqrt(jnp.mean(od * od, axis=0, keepdims=True) + RMS_EPS) * gcol_ref[...]
    o_ref[...] = (od * (1.0 - lam_init)).T


def _diff_prompt(lam4, dq, dk, dv, subln_g, B, T, tq, tk, lam_init):
    nq = T // tq
    kern = functools.partial(_diff_prompt_kernel, tq=tq, tk=tk, lam_init=lam_init)
    return pl.pallas_call(
        kern,
        grid=(B, DIFF_HEADS, nq),
        in_specs=[pl.BlockSpec((4, HEAD_DIM), lambda b, h, i: (0, 0)),
                  pl.BlockSpec((tq, DIFF_VDIM), lambda b, h, i: (b * nq + i, h)),
                  pl.BlockSpec((T, DIFF_VDIM), lambda b, h, i: (b, h)),
                  pl.BlockSpec((T, DIFF_VDIM), lambda b, h, i: (b, h)),
                  pl.BlockSpec((DIFF_VDIM, 1), lambda b, h, i: (0, 0))],
        out_specs=pl.BlockSpec((tq, DIFF_VDIM), lambda b, h, i: (b * nq + i, h)),
        out_shape=jax.ShapeDtypeStruct((B * T, DIFF_WIDTH), F32),
        scratch_shapes=[pltpu.VMEM((T, DIFF_VDIM), BF16), pltpu.VMEM((T // tk, DIFF_VDIM, tk), BF16),
                        pltpu.VMEM((1, 2 * tq), F32), pltpu.VMEM((1, 2 * tq), F32),
                        pltpu.VMEM((DIFF_VDIM, 2 * tq), F32)],
        compiler_params=_cparams(("arbitrary", "arbitrary", "arbitrary")),
        name="diff_prompt",
    )(lam4, dq, dk, dv, subln_g.reshape(DIFF_VDIM, 1))


FF_CHUNK = 1408


def _rms(x, g):
    return x * lax.rsqrt(jnp.mean(x * x, axis=-1, keepdims=True) + RMS_EPS) * g


def _ffn_seq_kernel(x_ref, on_ref, od_ref, won_ref, wod_ref, g2_ref, wup_ref, cw_ref, cb_ref, wdn_ref,
                    gf_ref, y_ref, conv_ref, carry_ref, *, tm, tiles_per_seq):
    i = pl.program_id(0)
    first = (i % tiles_per_seq) == 0
    x1 = (x_ref[...] + jnp.dot(on_ref[...].astype(BF16), won_ref[...], preferred_element_type=F32)
          + jnp.dot(od_ref[...].astype(BF16), wod_ref[...], preferred_element_type=F32))
    h = _rms(x1, g2_ref[...]).astype(BF16)
    rowi = lax.broadcasted_iota(jnp.int32, (tm, 1), 0)
    f = jnp.zeros((tm, D_MODEL), F32)
    for k in range(D_FF // FF_CHUNK):
        halves = []
        for part in range(2):
            c0 = part * D_FF + k * FF_CHUNK
            cs = slice(c0, c0 + FF_CHUNK)
            u = jnp.dot(h, wup_ref[:, cs], preferred_element_type=F32)
            pm2 = jnp.where(first, 0.0, carry_ref[6:7, cs])
            pm1 = jnp.where(first, 0.0, carry_ref[7:8, cs])
            u1 = jnp.where(rowi == 0, pm1, pltpu.roll(u, 1, 0))
            u2 = jnp.where(rowi == 0, pm2, jnp.where(rowi == 1, pm1, pltpu.roll(u, 2, 0)))
            halves.append(cb_ref[:, cs] + u2 * cw_ref[0:1, cs] + u1 * cw_ref[1:2, cs] + u * cw_ref[2:3, cs])
            carry_ref[6:8, cs] = u[tm - 2:tm]
            conv_ref[0, :, cs] = u[tm - 2:tm]
        act = (jax.nn.silu(halves[0]) * halves[1]).astype(BF16)
        f = f + jnp.dot(act, wdn_ref[k * FF_CHUNK:(k + 1) * FF_CHUNK, :], preferred_element_type=F32)
    y_ref[...] = _rms(x1 + f, gf_ref[...])


def _ffn_step_kernel(x_ref, on_ref, od_ref, won_ref, wod_ref, g2_ref, wup_ref, cw_ref, cb_ref, wdn_ref,
                     gf_ref, p0_ref, p1_ref, y_ref, u_ref):
    x1 = (x_ref[...] + jnp.dot(on_ref[...].astype(BF16), won_ref[...], preferred_element_type=F32)
          + jnp.dot(od_ref[...].astype(BF16), wod_ref[...], preferred_element_type=F32))
    h = _rms(x1, g2_ref[...]).astype(BF16)
    f = jnp.zeros(x1.shape, F32)
    for k in range(D_FF // FF_CHUNK):
        halves = []
        for part in range(2):
            c0 = part * D_FF + k * FF_CHUNK
            cs = slice(c0, c0 + FF_CHUNK)
            u = jnp.dot(h, wup_ref[:, cs], preferred_element_type=F32)
            u_ref[:, cs] = u
            halves.append(cb_ref[:, cs] + p0_ref[:, cs] * cw_ref[0:1, cs] + p1_ref[:, cs] * cw_ref[1:2, cs]
                          + u * cw_ref[2:3, cs])
        act = (jax.nn.silu(halves[0]) * halves[1]).astype(BF16)
        f = f + jnp.dot(act, wdn_ref[k * FF_CHUNK:(k + 1) * FF_CHUNK, :], preferred_element_type=F32)
    y_ref[...] = _rms(x1 + f, gf_ref[...])


def _const_spec(a, ngrid):
    return pl.BlockSpec(a.shape, lambda *idx: (0,) * a.ndim, pipeline_mode=pl.Buffered(1))


def _ffn_weights_specs(ws):
    return [_const_spec(a, 1) for a in ws]


def _ffn_seq(x2d, o_nsa, o_diff, ws, B, T, tm):
    M = x2d.shape[0]
    tps = T // tm
    row = lambda w: pl.BlockSpec((tm, w), lambda i: (i, 0))
    kern = functools.partial(_ffn_seq_kernel, tm=tm, tiles_per_seq=tps)
    return pl.pallas_call(
        kern,
        grid=(M // tm,),
        in_specs=[row(D_MODEL), row(NSA_WIDTH), row(DIFF_WIDTH)] + _ffn_weights_specs(ws),
        out_specs=[row(D_MODEL), pl.BlockSpec((1, CONV_W - 1, 2 * D_FF), lambda i: (i // tps, 0, 0))],
        out_shape=[jax.ShapeDtypeStruct((M, D_MODEL), F32),
                   jax.ShapeDtypeStruct((B, CONV_W - 1, 2 * D_FF), F32)],
        scratch_shapes=[pltpu.VMEM((8, 2 * D_FF), F32)],
        compiler_params=_cparams(("arbitrary",)),
        name="ffn_prompt",
    )(x2d, o_nsa, o_diff, *ws)


def _ffn_step(x2d, o_nsa, o_diff, ws, p0, p1):
    M = x2d.shape[0]
    full = lambda a: pl.BlockSpec(a.shape, lambda i: (0,) * a.ndim)
    ins = (x2d, o_nsa, o_diff) + tuple(ws) + (p0, p1)
    return pl.pallas_call(
        _ffn_step_kernel,
        grid=(1,),
        in_specs=[full(a) for a in ins],
        out_specs=[pl.BlockSpec((M, D_MODEL), lambda i: (0, 0)), pl.BlockSpec((M, 2 * D_FF), lambda i: (0, 0))],
        out_shape=[jax.ShapeDtypeStruct((M, D_MODEL), F32), jax.ShapeDtypeStruct((M, 2 * D_FF), F32)],
        compiler_params=_cparams(("arbitrary",)),
        name="ffn_sample",
    )(*ins)


ROWS_PER_PAGE = PAGE_SIZE // CMP_STRIDE


def _compress_sample_kernel(pt_ref, pool_ref, tail_ref, w1cat_ref, posb_ref, w1n_ref, w2bd_ref, o_ref,
                            buf, sem, *, n_pages, nb, rows):
    b = pl.program_id(0)
    past_rows = n_pages * ROWS_PER_PAGE

    def page_copy(bb, n, slot):
        return pltpu.make_async_copy(pool_ref.at[pt_ref[bb * n_pages + n]],
                                     buf.at[slot, pl.ds(n * ROWS_PER_PAGE, ROWS_PER_PAGE), :], sem.at[slot])

    def start_all(bb, slot):
        def body(n, _):
            page_copy(bb, n, slot).start()
            return 0
        lax.fori_loop(0, n_pages, body, 0)

    @pl.when(b == 0)
    def _():
        start_all(0, 0)

    @pl.when(b + 1 < nb)
    def _():
        start_all(b + 1, (b + 1) % 2)

    slot = b % 2

    def wait_body(n, _):
        page_copy(b, n, slot).wait()
        return 0
    lax.fori_loop(0, n_pages, wait_body, 0)

    buf[slot, past_rows:rows, :] = tail_ref[0]
    o_ref[0, 0:rows, :] = _compress_rows(buf[slot], w1cat_ref[...], posb_ref[...], w1n_ref[...], w2bd_ref[...])
    o_ref[0, rows:, :] = jnp.zeros((o_ref.shape[1] - rows, KV_W), F32)


def _compress_sample(pool, pt_flat, new_row, cw, nb, n_pages, rows, rows_out):
    n_pool = pool.shape[0]
    width = CMP_STRIDE * KV_W
    pool_r = pool.reshape(n_pool, ROWS_PER_PAGE, width)
    tail_rows = rows - n_pages * ROWS_PER_PAGE
    tail = jnp.zeros((nb, tail_rows, width), F32).at[:, 0, :KV_W].set(new_row)
    kern = functools.partial(_compress_sample_kernel, n_pages=n_pages, nb=nb, rows=rows)
    full = lambda a: pl.BlockSpec(a.shape, lambda b, pt: (0,) * a.ndim)
    return pl.pallas_call(
        kern,
        grid_spec=pltpu.PrefetchScalarGridSpec(
            num_scalar_prefetch=1,
            grid=(nb,),
            in_specs=[pl.BlockSpec(memory_space=pl.ANY),
                      pl.BlockSpec((1, tail_rows, width), lambda b, pt: (b, 0, 0))] + [full(a) for a in cw],
            out_specs=pl.BlockSpec((1, rows_out, KV_W), lambda b, pt: (b, 0, 0)),
            scratch_shapes=[pltpu.VMEM((2, rows, width), F32), pltpu.SemaphoreType.DMA((2,))]),
        out_shape=jax.ShapeDtypeStruct((nb, rows_out, KV_W), F32),
        compiler_params=_cparams(("arbitrary",)),
        name="compress_sample",
    )(pt_flat, pool_r, tail, *cw)


def _pad_rows8(x):
    return jnp.concatenate([x, jnp.zeros((8 - x.shape[0], x.shape[1]), x.dtype)], axis=0)


def _nsa_sample_a_kernel(q_ref, kc_ref, vc_ref, ovbd_ref, oc_ref, info_ref, *, n_cmp, n_sel, q_pos):
    rk = kc_ref.shape[1]
    n_pad = ovbd_ref.shape[1] // NSA_GROUPS
    q8 = _pad_rows8(q_ref[0]).astype(BF16)
    kcbd = _block_diag_rows(kc_ref[0], BF16)
    vcbd = _block_diag_rows(vc_ref[0], BF16)
    s = _dot_nt(q8, kcbd)
    kidx = lax.broadcasted_iota(jnp.int32, s.shape, 1) % rk
    cmask = (kidx * CMP_STRIDE + (CMP_BLOCK - 1) <= q_pos) & (kidx < n_cmp)
    s = jnp.where(cmask, s * ATTN_SCALE, NEG_INF)
    pcs = []
    for g in range(NSA_GROUPS):
        sg = s[:, g * rk:(g + 1) * rk]
        m = jnp.max(sg, axis=-1, keepdims=True)
        m = jnp.where(m == NEG_INF, 0.0, m)
        p = jnp.exp(sg - m)
        pcs.append(p / jnp.maximum(jnp.sum(p, axis=-1, keepdims=True), 1e-30))
    pc = jnp.concatenate(pcs, axis=1)
    oc_ref[0] = jnp.dot(pc.astype(BF16), vcbd, preferred_element_type=F32)
    imp = jnp.sum(pc[0:NSA_REP], axis=0, keepdims=True)
    imp8 = jnp.concatenate([imp] * 8, axis=0)
    ih, il = _split2(imp8)
    score = (jnp.dot(ih, ovbd_ref[...], preferred_element_type=F32)
             + jnp.dot(il, ovbd_ref[...], preferred_element_type=F32))[0:1]

    blk_r = lax.broadcasted_iota(jnp.int32, (1, n_pad), 1)
    blk_c = lax.broadcasted_iota(jnp.int32, (n_pad, 1), 0)
    cur = q_pos // SEL_BLOCK
    forced_r = (blk_r == 0) | (blk_r == cur) | (blk_r == cur - 1)
    ok_r = (blk_r * SEL_BLOCK <= q_pos) & (blk_r < n_sel)
    ok_c = (blk_c * SEL_BLOCK <= q_pos) & (blk_c < n_sel)
    ii = lax.broadcasted_iota(jnp.int32, (n_pad, n_pad), 1)
    jj = lax.broadcasted_iota(jnp.int32, (n_pad, n_pad), 0)
    kk = lax.broadcasted_iota(jnp.int32, (n_pad, LANES), 1)
    jf = lax.broadcasted_iota(jnp.int32, (n_pad, LANES), 0).astype(F32)
    rows_out = []
    for g in range(NSA_GROUPS):
        sr = score[:, g * n_pad:(g + 1) * n_pad]
        sr = jnp.where(ok_r, sr + jnp.where(forced_r, FORCE_BONUS, 0.0), NEG_INF)
        sc = jnp.sum(jnp.where(ii == jj, sr, 0.0), axis=1, keepdims=True)
        ahead = jnp.where(ii < jj, jnp.where(sr >= sc, 1.0, 0.0), jnp.where(sr > sc, 1.0, 0.0))
        cnt = jnp.sum(ahead, axis=1, keepdims=True)
        chosen = (cnt < float(SEL_TOPN)) & ok_c
        hit = chosen & (cnt == kk.astype(F32))
        rows_out.append((jnp.sum(jnp.where(hit, jf, 0.0), axis=0, keepdims=True),
                         jnp.sum(jnp.where(hit, 1.0, 0.0), axis=0, keepdims=True)))
    info = jnp.concatenate([rows_out[0][0], rows_out[1][0], rows_out[0][1], rows_out[1][1],
                            jnp.zeros((4, LANES), F32)], axis=0)
    info_ref[0] = info.astype(jnp.int32)


def _nsa_sample_a(qraw3, kc, vc, q_pos, n_cmp, n_sel):
    nb, rk, _ = kc.shape
    n_pad = -(-n_sel // LANES) * LANES
    ovbd = _overlap_matrix(rk, n_pad)
    kern = functools.partial(_nsa_sample_a_kernel, n_cmp=n_cmp, n_sel=n_sel, q_pos=q_pos)
    return pl.pallas_call(
        kern,
        grid=(nb,),
        in_specs=[pl.BlockSpec((1, NSA_REP, LANES), lambda b: (b, 0, 0)),
                  pl.BlockSpec((1, rk, KV_W), lambda b: (b, 0, 0)),
                  pl.BlockSpec((1, rk, KV_W), lambda b: (b, 0, 0)),
                  pl.BlockSpec(ovbd.shape, lambda b: (0, 0))],
        out_specs=[pl.BlockSpec((1, 8, LANES), lambda b: (b, 0, 0)),
                   pl.BlockSpec((1, 8, LANES), lambda b: (b, 0, 0))],
        out_shape=[jax.ShapeDtypeStruct((nb, 8, LANES), F32), jax.ShapeDtypeStruct((nb, 8, LANES), jnp.int32)],
        compiler_params=_cparams(("arbitrary",)),
        name="nsa_sample_select",
    )(qraw3, kc, vc, ovbd)


N_SLOTS = NSA_GROUPS * SEL_TOPN


def _decode_attend(s, mask, vbd, s_new, inc_new, v_new, half):
    s = jnp.where(mask, s, NEG_INF)
    ps, ls, pn = [], [], []
    for g in range(NSA_GROUPS):
        sg = s[:, g * half:(g + 1) * half]
        sn = jnp.where(inc_new[g] > 0.5, s_new[g], NEG_INF)
        m = jnp.maximum(jnp.max(sg, axis=-1, keepdims=True), sn)
        m = jnp.where(m == NEG_INF, 0.0, m)
        p = jnp.exp(sg - m)
        pnew = jnp.exp(sn - m)
        ps.append(p)
        pn.append(pnew)
        ls.append(jnp.sum(p, axis=-1, keepdims=True) + pnew)
    p = jnp.concatenate(ps, axis=1).astype(BF16)
    lo, _ = _lane_group_masks((8, KV_W))
    o = jnp.dot(p, vbd, preferred_element_type=F32) + jnp.where(lo, pn[0], pn[1]) * v_new
    return o / jnp.maximum(jnp.where(lo, ls[0], ls[1]), 1e-30)


def _nsa_sample_b_kernel(sel_ref, pt_ref, q_ref, gate_ref, oc_ref, info_ref, psk_ref, psv_ref,
                         sknew_ref, svnew_ref, wink_ref, winv_ref, wknew_ref, wvnew_ref, eexp_ref, e2_ref,
                         o_ref, swk_ref, swv_ref, kbuf, vbuf, sem, *, nb, n_pages, n_past_blk, wbuf):
    b = pl.program_id(0)
    blk_per_page = PAGE_SIZE // SEL_BLOCK

    def copies(bb, idx, slot):
        j = jnp.minimum(sel_ref[bb * N_SLOTS + idx], n_past_blk - 1)
        page = pt_ref[bb * n_pages + j // blk_per_page]
        off = pl.multiple_of((j % blk_per_page) * SEL_BLOCK, SEL_BLOCK)
        return (pltpu.make_async_copy(psk_ref.at[page, pl.ds(off, SEL_BLOCK), :], kbuf.at[slot, idx], sem.at[slot]),
                pltpu.make_async_copy(psv_ref.at[page, pl.ds(off, SEL_BLOCK), :], vbuf.at[slot, idx], sem.at[slot]))

    def start_all(bb, slot):
        def body(idx, _):
            ck, cv = copies(bb, idx, slot)
            ck.start()
            cv.start()
            return 0
        lax.fori_loop(0, N_SLOTS, body, 0)

    @pl.when(b == 0)
    def _():
        start_all(0, 0)

    @pl.when(b + 1 < nb)
    def _():
        start_all(b + 1, (b + 1) % 2)

    slot = b % 2

    def wait_body(idx, _):
        ck, cv = copies(b, idx, slot)
        ck.wait()
        cv.wait()
        return 0
    lax.fori_loop(0, N_SLOTS, wait_body, 0)

    q8f = _pad_rows8(q_ref[0])
    q8 = q8f.astype(BF16)
    lo, hi = _lane_group_masks((8, KV_W))

    def new_scores(k_new):
        prod = q8f * k_new
        return [jnp.sum(jnp.where(lo, prod, 0.0), axis=-1, keepdims=True) * ATTN_SCALE,
                jnp.sum(jnp.where(hi, prod, 0.0), axis=-1, keepdims=True) * ATTN_SCALE]

    half = SEL_TOPN * SEL_BLOCK
    kg = kbuf[slot].reshape(N_SLOTS * SEL_BLOCK, KV_W)
    vg = vbuf[slot].reshape(N_SLOTS * SEL_BLOCK, KV_W)
    rowg = lax.broadcasted_iota(jnp.int32, kg.shape, 0) // half
    laneg = lax.broadcasted_iota(jnp.int32, kg.shape, 1) // HEAD_DIM
    keep = rowg == laneg
    kbd = jnp.where(keep, kg, 0.0).astype(BF16)
    vbd = jnp.where(keep, vg, 0.0).astype(BF16)
    s = _dot_nt(q8, kbd) * ATTN_SCALE
    info = info_ref[0].astype(F32)
    lane1 = lax.broadcasted_iota(jnp.int32, (1, LANES), 1)
    jl, vl, inc = [], [], []
    for g in range(NSA_GROUPS):
        idx8 = jnp.concatenate([info[g:g + 1]] * 8, axis=0).astype(BF16)
        val8 = jnp.concatenate([info[2 + g:3 + g]] * 8, axis=0).astype(BF16)
        jl.append(jnp.dot(idx8, eexp_ref[...], preferred_element_type=F32))
        vl.append(jnp.dot(val8, eexp_ref[...], preferred_element_type=F32))
        is_new = (info[g:g + 1] > n_past_blk - 0.5) & (info[2 + g:3 + g] > 0.5) & (lane1 < SEL_TOPN)
        inc.append(jnp.sum(jnp.where(is_new, 1.0, 0.0), axis=-1, keepdims=True))
    jlane = jnp.concatenate(jl, axis=1)
    vlane = jnp.concatenate(vl, axis=1)
    mask = (vlane > 0.5) & (jlane < n_past_blk - 0.5)
    o_s = _decode_attend(s, mask, vbd, new_scores(sknew_ref[0]), inc, svnew_ref[0], half)

    wk_old = wink_ref[0]
    wv_old = winv_ref[0]
    sw = _dot_nt(q8, _block_diag_rows(wk_old, BF16)) * ATTN_SCALE
    widx = lax.broadcasted_iota(jnp.int32, sw.shape, 1) % wbuf
    wmask = widx > wbuf - WINDOW
    always = [jnp.ones((1, 1), F32)] * NSA_GROUPS
    o_w = _decode_attend(sw, wmask, _block_diag_rows(wv_old, BF16), new_scores(wknew_ref[0]), always,
                         wvnew_ref[0], wbuf)
    rowi = lax.broadcasted_iota(jnp.int32, (wbuf, 1), 0)
    swk_ref[0] = jnp.where(rowi == wbuf - 1, wknew_ref[0], pltpu.roll(wk_old, wbuf - 1, 0))
    swv_ref[0] = jnp.where(rowi == wbuf - 1, wvnew_ref[0], pltpu.roll(wv_old, wbuf - 1, 0))

    gate8 = jnp.concatenate([gate_ref[0]] * 8, axis=0)
    rr = lax.broadcasted_iota(jnp.int32, (8, LANES), 0)
    ll = lax.broadcasted_iota(jnp.int32, (8, LANES), 1)
    o = jnp.zeros((8, KV_W), F32)
    for c, ob in enumerate((oc_ref[0], o_s, o_w)):
        base = c * NSA_HEADS + rr * NSA_GROUPS
        gsel = jnp.where((ll >= base) & (ll < base + NSA_GROUPS), gate8, 0.0)
        gh, gl = _split2(gsel)
        gx = (jnp.dot(gh, e2_ref[...], preferred_element_type=F32)
              + jnp.dot(gl, e2_ref[...], preferred_element_type=F32))
        o = o + gx * ob
    o_ref[0] = o[0:NSA_REP]


def _nsa_sample_b(sel_flat, pt_flat, qrot3, gate3, oc, info, pool_sk, pool_sv, sknew, svnew,
                  wink, winv, wknew, wvnew, nb, n_pages):
    n_pool = pool_sk.shape[0]
    wbuf = wink.shape[1]
    n_past_blk = n_pages * (PAGE_SIZE // SEL_BLOCK)
    e = np.zeros((LANES, SEL_TOPN * SEL_BLOCK), np.float32)
    for k in range(SEL_TOPN):
        e[k, k * SEL_BLOCK:(k + 1) * SEL_BLOCK] = 1.0
    eexp = jnp.asarray(e, BF16)
    e2 = np.zeros((LANES, LANES), np.float32)
    for j in range(LANES):
        e2[j, (j % NSA_GROUPS) * HEAD_DIM:(j % NSA_GROUPS + 1) * HEAD_DIM] = 1.0
    e2 = jnp.asarray(e2, BF16)
    per_b = lambda r, w: pl.BlockSpec((1, r, w), lambda b, s, p: (b, 0, 0))
    const = lambda a: pl.BlockSpec(a.shape, lambda b, s, p: (0, 0))
    anyspec = pl.BlockSpec(memory_space=pl.ANY)
    kern = functools.partial(_nsa_sample_b_kernel, nb=nb, n_pages=n_pages, n_past_blk=n_past_blk, wbuf=wbuf)
    r3 = lambda a: a.reshape(nb, 1, KV_W)
    return pl.pallas_call(
        kern,
        grid_spec=pltpu.PrefetchScalarGridSpec(
            num_scalar_prefetch=2,
            grid=(nb,),
            in_specs=[per_b(NSA_REP, LANES), per_b(1, LANES), per_b(8, LANES), per_b(8, LANES),
                      anyspec, anyspec, per_b(1, KV_W), per_b(1, KV_W),
                      per_b(wbuf, KV_W), per_b(wbuf, KV_W), per_b(1, KV_W), per_b(1, KV_W),
                      const(eexp), const(e2)],
            out_specs=[per_b(NSA_REP, LANES), per_b(wbuf, KV_W), per_b(wbuf, KV_W)],
            scratch_shapes=[pltpu.VMEM((2, N_SLOTS, SEL_BLOCK, KV_W), F32),
                            pltpu.VMEM((2, N_SLOTS, SEL_BLOCK, KV_W), F32),
                            pltpu.SemaphoreType.DMA((2,))]),
        out_shape=[jax.ShapeDtypeStruct((nb, NSA_REP, LANES), F32),
                   jax.ShapeDtypeStruct((nb, wbuf, KV_W), F32),
                   jax.ShapeDtypeStruct((nb, wbuf, KV_W), F32)],
        compiler_params=_cparams(("arbitrary",)),
        name="nsa_sample_attend",
    )(sel_flat, pt_flat, qrot3, gate3, oc, info,
      pool_sk.reshape(n_pool, PAGE_SIZE, KV_W), pool_sv.reshape(n_pool, PAGE_SIZE, KV_W),
      r3(sknew), r3(svnew), wink, winv, r3(wknew), r3(wvnew), eexp, e2)


DIFF_PAGES_PER_STEP = 8


def _diff_sample_kernel(pt_ref, lam_ref, dq_ref, dknew_ref, dvnew_ref, g_ref, pk_ref, pv_ref, o_ref,
                        kbuf, vbuf, sem, m_ref, l_ref, acc_ref, *, nb, n_pages, lam_init):
    b = pl.program_id(0)
    c = pl.program_id(1)
    nc = n_pages // DIFF_PAGES_PER_STEP
    step = b * nc + c

    def copies(st, p, slot):
        bb = st // nc
        cc = st % nc
        page = pt_ref[bb * n_pages + cc * DIFF_PAGES_PER_STEP + p]
        dst = pl.ds(p * PAGE_SIZE * DIFF_HEADS, PAGE_SIZE * DIFF_HEADS)
        return (pltpu.make_async_copy(pk_ref.at[page], kbuf.at[slot, p], sem.at[slot]),
                pltpu.make_async_copy(pv_ref.at[page], vbuf.at[slot, dst, :], sem.at[slot]))

    def start_all(st, slot):
        for p in range(DIFF_PAGES_PER_STEP):
            ck, cv = copies(st, p, slot)
            ck.start()
            cv.start()

    @pl.when(step == 0)
    def _():
        start_all(0, 0)

    @pl.when(step + 1 < nb * nc)
    def _():
        start_all(step + 1, (step + 1) % 2)

    slot = step % 2
    for p in range(DIFF_PAGES_PER_STEP):
        ck, cv = copies(step, p, slot)
        ck.wait()
        cv.wait()

    @pl.when(c == 0)
    def _():
        m_ref[...] = jnp.full(m_ref.shape, NEG_INF, F32)
        l_ref[...] = jnp.zeros(l_ref.shape, F32)
        acc_ref[...] = jnp.zeros(acc_ref.shape, F32)

    rr = lax.broadcasted_iota(jnp.int32, (8, DIFF_WIDTH), 0)
    ll = lax.broadcasted_iota(jnp.int32, (8, DIFF_WIDTH), 1)
    own = (ll // HEAD_DIM) == rr
    q8f = jnp.where(own, jnp.concatenate([dq_ref[0]] * 8, axis=0), 0.0)
    q8 = q8f.astype(BF16)
    s = jnp.concatenate([jnp.dot(q8, kbuf[slot, p].astype(BF16), preferred_element_type=F32)
                         for p in range(DIFF_PAGES_PER_STEP)], axis=1) * ATTN_SCALE
    m_old = m_ref[...]
    m_new = jnp.maximum(m_old, jnp.max(s, axis=-1, keepdims=True))
    alpha = jnp.exp(m_old - m_new)
    p = jnp.exp(s - m_new)
    l_ref[...] = alpha * l_ref[...] + jnp.sum(p, axis=-1, keepdims=True)
    m_ref[...] = m_new
    pb = p.astype(BF16)
    keys = DIFF_PAGES_PER_STEP * PAGE_SIZE
    pv = jnp.concatenate([jnp.dot(pb, vbuf[slot, pl.ds(h, keys, stride=DIFF_HEADS), :].astype(BF16),
                                  preferred_element_type=F32) for h in range(DIFF_HEADS)], axis=1)
    acc_ref[...] = alpha * acc_ref[...] + pv

    @pl.when(c == nc - 1)
    def _():
        s_new = jnp.sum(q8f * dknew_ref[0], axis=-1, keepdims=True) * ATTN_SCALE
        m_old = m_ref[...]
        m_new = jnp.maximum(m_old, s_new)
        alpha = jnp.exp(m_old - m_new)
        p_new = jnp.exp(s_new - m_new)
        l = alpha * l_ref[...] + p_new
        o = (alpha * acc_ref[...] + p_new * dvnew_ref[0]) / l
        lam = _lambda_value(lam_ref, lam_init)
        for h in range(DIFF_HEADS):
            cs = slice(h * DIFF_VDIM, (h + 1) * DIFF_VDIM)
            o_ref[0, :, cs] = _diff_merge_rows(o[2 * h:2 * h + 1, cs], o[2 * h + 1:2 * h + 2, cs], lam,
                                               g_ref[...], lam_init)


def _diff_sample(pt_flat, lam4, dq, dknew, dvnew, subln_g, pool_k, pool_v, nb, n_pages, lam_init):
    n_pool = pool_k.shape[0]
    nc = n_pages // DIFF_PAGES_PER_STEP
    keys = DIFF_PAGES_PER_STEP * PAGE_SIZE
    per_b = pl.BlockSpec((1, 1, DIFF_WIDTH), lambda b, c, pt: (b, 0, 0))
    anyspec = pl.BlockSpec(memory_space=pl.ANY)
    kern = functools.partial(_diff_sample_kernel, nb=nb, n_pages=n_pages, lam_init=lam_init)
    r3 = lambda a: a.reshape(nb, 1, DIFF_WIDTH)
    return pl.pallas_call(
        kern,
        grid_spec=pltpu.PrefetchScalarGridSpec(
            num_scalar_prefetch=1,
            grid=(nb, nc),
            in_specs=[pl.BlockSpec((4, HEAD_DIM), lambda b, c, pt: (0, 0)), per_b, per_b, per_b,
                      pl.BlockSpec((1, DIFF_VDIM), lambda b, c, pt: (0, 0)), anyspec, anyspec],
            out_specs=per_b,
            scratch_shapes=[pltpu.VMEM((2, DIFF_PAGES_PER_STEP, DIFF_WIDTH, PAGE_SIZE), F32),
                            pltpu.VMEM((2, keys * DIFF_HEADS, DIFF_VDIM), F32),
                            pltpu.SemaphoreType.DMA((2,)),
                            pltpu.VMEM((8, 1), F32), pltpu.VMEM((8, 1), F32), pltpu.VMEM((8, DIFF_WIDTH), F32)]),
        out_shape=jax.ShapeDtypeStruct((nb, 1, DIFF_WIDTH), F32),
        compiler_params=_cparams(("arbitrary", "arbitrary")),
        name="diff_sample",
    )(pt_flat, lam4, r3(dq), r3(dknew), r3(dvnew), subln_g,
      jnp.transpose(pool_k, (0, 2, 3, 4, 1)).reshape(n_pool, DIFF_WIDTH, PAGE_SIZE),
      pool_v.reshape(n_pool, PAGE_SIZE * DIFF_HEADS, DIFF_VDIM))


def _prep_w_in(w):
    splits = (NSA_WIDTH, KV_W, KV_W, KV_W, KV_W, KV_W, KV_W, NSA_HEADS * 3, DIFF_WIDTH, DIFF_WIDTH, DIFF_WIDTH)
    cuts = [int(c) for c in np.cumsum(splits)[:-1]]
    q, ck, cv, sk, sv, wk, wv, gate, dq, dk, dv = jnp.split(w, cuts, axis=1)
    q = q.reshape(D_MODEL, NSA_GROUPS, NSA_REP, HEAD_DIM).transpose(0, 2, 1, 3).reshape(D_MODEL, NSA_WIDTH)
    gate = gate.reshape(D_MODEL, NSA_GROUPS, NSA_REP, 3).transpose(0, 3, 2, 1).reshape(D_MODEL, NSA_HEADS * 3)
    gate = jnp.pad(gate, ((0, 0), (0, LANES - NSA_HEADS * 3)))
    return jnp.concatenate([q, ck, cv, sk, sv, wk, wv, dq, dk, dv, gate], axis=1).astype(BF16)


def _prep_w_out(w):
    won = w[:NSA_WIDTH].reshape(NSA_GROUPS, NSA_REP, HEAD_DIM, D_MODEL).transpose(1, 0, 2, 3)
    return won.reshape(NSA_WIDTH, D_MODEL).astype(BF16), w[NSA_WIDTH:].astype(BF16)


def _prep_ffn(p):
    won, wod = _prep_w_out(p["w_out"][0])
    return (won, wod, p["ffn_norm"], p["w_up"][0].astype(BF16), p["conv_w"][0], p["conv_b"],
            p["w_down"][0].astype(BF16), p["final_norm"].reshape(1, D_MODEL))


def _rope_tables(pos):
    half = HEAD_DIM // 2
    inv = 1.0 / (ROPE_THETA ** (jnp.arange(half, dtype=F32) / half))
    ang = pos.astype(F32)[:, None] * inv[None, :]
    cos, sin = jnp.cos(ang), jnp.sin(ang)
    return jnp.tile(cos, (1, 4)), jnp.tile(jnp.concatenate([-sin, sin], axis=1), (1, 2))


def _prompt_group(xp, attn_norm, w_in_r, cw_k, cw_v, lam4, subln_g, ffn_ws, lam_init):
    B, T, _ = xp.shape
    x2d = xp.reshape(B * T, D_MODEL)
    tm = min(256, T)
    cos_t, sin_t = _rope_tables(jnp.arange(T, dtype=jnp.int32))
    qraw, qrot, ck, cv, sk, sv, wk, wv, gate, dq, dk, dv = _project(x2d, attn_norm, w_in_r, cos_t, sin_t, tm)
    kc = _compress_prompt(ck.reshape(B, T, KV_W), cw_k)
    vc = _compress_prompt(cv.reshape(B, T, KV_W), cw_v)
    o_nsa = _nsa_prompt(qraw, qrot, gate, kc, vc, sk, sv, wk, wv, B, T, tq=256, tk=256)
    o_diff = _diff_prompt(lam4, dq, dk, dv, subln_g, B, T, tq=256, tk=512, lam_init=lam_init)
    y, conv = _ffn_seq(x2d, o_nsa, o_diff, ffn_ws, B, T, tm)
    kv = lambda a: a.reshape(1, B, T, NSA_GROUPS, HEAD_DIM)
    wb = min(WINDOW, T)
    return (y.reshape(B, T, D_MODEL), kv(ck), kv(cv), kv(sk), kv(sv),
            dk.reshape(1, B, T, DIFF_HEADS, 2, HEAD_DIM), dv.reshape(1, B, T, DIFF_HEADS, DIFF_VDIM),
            kv(wk)[:, :, T - wb:], kv(wv)[:, :, T - wb:], conv[None])


def _sample_group(xs, caches, page_table, attn_norm, w_in_r, cw_k, cw_v, lam4, subln_g, ffn_ws, lam_init):
    pool_ck, pool_cv, pool_sk, pool_sv, pool_dk, pool_dv, win_k, win_v, conv_state = caches
    nb = xs.shape[0]
    n_pages = page_table.shape[1]
    past = n_pages * PAGE_SIZE
    t_pad = -(-(past + 1) // SEL_BLOCK) * SEL_BLOCK
    n_cmp = t_pad // CMP_STRIDE - CMP_BLOCK // CMP_STRIDE + 1
    n_sel = t_pad // SEL_BLOCK
    rows = -(-(t_pad // CMP_STRIDE) // 8) * 8
    rows_out = -(-rows // LANES) * LANES
    pt_flat = page_table.reshape(-1)
    x2d = xs.reshape(nb, D_MODEL)
    pos = jnp.full((nb,), past, dtype=jnp.int32)
    cos_t, sin_t = _rope_tables(pos)
    qraw, qrot, ck, cv, sk, sv, wk, wv, gate, dq, dk, dv = _project(x2d, attn_norm, w_in_r, cos_t, sin_t, nb)
    kc = _compress_sample(pool_ck, pt_flat, ck, cw_k, nb, n_pages, rows, rows_out)
    vc = _compress_sample(pool_cv, pt_flat, cv, cw_v, nb, n_pages, rows, rows_out)
    o_c, info = _nsa_sample_a(qraw.reshape(nb, NSA_REP, LANES), kc, vc, past, n_cmp, n_sel)
    sel_flat = info[:, 0:NSA_GROUPS, 0:SEL_TOPN].reshape(-1)
    wbuf = win_k.shape[1]
    o_nsa, swk, swv = _nsa_sample_b(sel_flat, pt_flat, qrot.reshape(nb, NSA_REP, LANES),
                                    gate.reshape(nb, 1, LANES), o_c, info, pool_sk, pool_sv, sk, sv,
                                    win_k.reshape(nb, wbuf, KV_W), win_v.reshape(nb, wbuf, KV_W), wk, wv,
                                    nb, n_pages)
    o_diff = _diff_sample(pt_flat, lam4, dq, dk, dv, subln_g, pool_dk, pool_dv, nb, n_pages, lam_init)
    y, u = _ffn_step(x2d, o_nsa.reshape(nb, NSA_WIDTH), o_diff.reshape(nb, DIFF_WIDTH), ffn_ws,
                     conv_state[:, 0], conv_state[:, 1])
    kv = lambda a: a.reshape(1, nb, 1, NSA_GROUPS, HEAD_DIM)
    return (y.reshape(nb, 1, D_MODEL), kv(ck), kv(cv), kv(sk), kv(sv),
            dk.reshape(1, nb, 1, DIFF_HEADS, 2, HEAD_DIM), dv.reshape(1, nb, 1, DIFF_HEADS, DIFF_VDIM),
            swk.reshape(1, nb, wbuf, NSA_GROUPS, HEAD_DIM), swv.reshape(1, nb, wbuf, NSA_GROUPS, HEAD_DIM),
            jnp.stack([conv_state[:, 1], u], axis=1)[None])


def kernel(x_prompt, x_sample, cache_cmp_k, cache_cmp_v, cache_sel_k, cache_sel_v, cache_diff_k, cache_diff_v,
           cache_win_k, cache_win_v, state_ffn_conv, page_table, attn_norm, w_in, cmp_pos_k, cmp_w1_k, cmp_w2_k,
           cmp_pos_v, cmp_w1_v, cmp_w2_v, lambda_q1, lambda_k1, lambda_q2, lambda_k2, subln_g, w_out, ffn_norm,
           w_up, conv_w, conv_b, w_down, final_norm):
    assert w_in.shape[0] == 1 and x_sample.shape[1] == 1, "one layer, one new token per sequence"
    lam_init = 0.8 - 0.6 * math.exp(0.0)
    w_in_r = _prep_w_in(w_in[0])
    cw_k = _compress_weights(cmp_pos_k[0], cmp_w1_k[0], cmp_w2_k[0])
    cw_v = _compress_weights(cmp_pos_v[0], cmp_w1_v[0], cmp_w2_v[0])
    lam4 = jnp.stack([lambda_q1[0], lambda_k1[0], lambda_q2[0], lambda_k2[0]])
    ffn_ws = _prep_ffn(dict(w_out=w_out, ffn_norm=ffn_norm, w_up=w_up, conv_w=conv_w, conv_b=conv_b,
                            w_down=w_down, final_norm=final_norm))
    p = _prompt_group(x_prompt, attn_norm, w_in_r, cw_k, cw_v, lam4, subln_g, ffn_ws, lam_init)
    caches = (cache_cmp_k[0], cache_cmp_v[0], cache_sel_k[0], cache_sel_v[0], cache_diff_k[0], cache_diff_v[0],
              cache_win_k[0], cache_win_v[0], state_ffn_conv[0])
    s = _sample_group(x_sample, caches, page_table, attn_norm, w_in_r, cw_k, cw_v, lam4, subln_g, ffn_ws,
                      lam_init)
    return (p[0], s[0]) + tuple(p[1:]) + tuple(s[1:])
```

```python
import functools
import math

import numpy as np
import jax
import jax.numpy as jnp
from jax import lax
from jax.experimental import pallas as pl
from jax.experimental.pallas import tpu as pltpu

F32 = jnp.float32
BF16 = jnp.bfloat16

D_MODEL = 1024
HEAD_DIM = 64
NSA_WIDTH = 512
NSA_HEADS = 8
NSA_GROUPS = 2
NSA_REP = 4
KV_W = NSA_GROUPS * HEAD_DIM
CMP_BLOCK = 32
CMP_STRIDE = 16
CMP_HIDDEN = 128
SEL_BLOCK = 64
SEL_TOPN = 16
WINDOW = 512
FORCE_BONUS = 1.0e4
DIFF_WIDTH = 512
DIFF_VDIM = 128
DIFF_HEADS = 4
D_FF = 2816
CONV_W = 3
ROPE_THETA = 10000.0
RMS_EPS = 1e-6
ATTN_SCALE = 1.0 / math.sqrt(HEAD_DIM)
SCALE_LOG2E = ATTN_SCALE * math.log2(math.e)
PAGE_SIZE = 128

LANES = 128
VMEM_LIMIT = 48 * 1024 * 1024
NEG_INF = float("-inf")

C_Q = 0
C_CK, C_CV, C_SK, C_SV, C_WK, C_WV = 512, 640, 768, 896, 1024, 1152
C_DQ, C_DK, C_DV = 1280, 1792, 2304
C_GATE = 2816
W_IN_COLS = 2944


def _cparams(sem):
    return pltpu.CompilerParams(dimension_semantics=sem, vmem_limit_bytes=VMEM_LIMIT)


def _rope128(v, cos, sin_signed):
    lane = lax.broadcasted_iota(jnp.int32, v.shape, 1)
    first = (lane % HEAD_DIM) < (HEAD_DIM // 2)
    partner = jnp.where(first, pltpu.roll(v, LANES - HEAD_DIM // 2, 1), pltpu.roll(v, HEAD_DIM // 2, 1))
    return v * cos + partner * sin_signed


def _proj_kernel(x_ref, g_ref, w_ref, cos_ref, sin_ref,
                 qraw_ref, qrot_ref, ck_ref, cv_ref, sk_ref, sv_ref, wk_ref, wv_ref,
                 gate_ref, dq_ref, dk_ref, dv_ref):
    x = x_ref[...]
    h = x * lax.rsqrt(jnp.mean(x * x, axis=-1, keepdims=True) + RMS_EPS) * g_ref[...]
    z = jnp.dot(h.astype(BF16), w_ref[...], preferred_element_type=F32)
    cos = cos_ref[...]
    sin = sin_ref[...]

    def put(ref, col, width, rope):
        for c in range(width // LANES):
            v = z[:, col + c * LANES: col + (c + 1) * LANES]
            ref[:, c * LANES:(c + 1) * LANES] = _rope128(v, cos, sin) if rope else v

    put(qraw_ref, C_Q, NSA_WIDTH, False)
    put(qrot_ref, C_Q, NSA_WIDTH, True)
    put(ck_ref, C_CK, KV_W, False)
    put(cv_ref, C_CV, KV_W, False)
    put(sk_ref, C_SK, KV_W, True)
    put(sv_ref, C_SV, KV_W, False)
    put(wk_ref, C_WK, KV_W, True)
    put(wv_ref, C_WV, KV_W, False)
    put(dq_ref, C_DQ, DIFF_WIDTH, True)
    put(dk_ref, C_DK, DIFF_WIDTH, True)
    put(dv_ref, C_DV, DIFF_WIDTH, False)
    gate_ref[...] = jax.nn.sigmoid(z[:, C_GATE:C_GATE + LANES])


def _project(x2d, norm_g, w_in_r, cos_t, sin_t, tm):
    M = x2d.shape[0]
    nt = cos_t.shape[0] // tm
    widths = (NSA_WIDTH, NSA_WIDTH, KV_W, KV_W, KV_W, KV_W, KV_W, KV_W, LANES,
              DIFF_WIDTH, DIFF_WIDTH, DIFF_WIDTH)
    row = lambda i: (i, 0)
    return pl.pallas_call(
        _proj_kernel,
        grid=(M // tm,),
        in_specs=[pl.BlockSpec((tm, D_MODEL), row),
                  pl.BlockSpec((1, D_MODEL), lambda i: (0, 0)),
                  pl.BlockSpec((D_MODEL, W_IN_COLS), lambda i: (0, 0)),
                  pl.BlockSpec((tm, LANES), lambda i: (i % nt, 0)),
                  pl.BlockSpec((tm, LANES), lambda i: (i % nt, 0))],
        out_specs=[pl.BlockSpec((tm, w), row) for w in widths],
        out_shape=[jax.ShapeDtypeStruct((M, w), F32) for w in widths],
        compiler_params=_cparams(("arbitrary",)),
        name="in_proj",
    )(x2d, norm_g, w_in_r, cos_t, sin_t)


def _gelu(x):
    return 0.5 * x * (1.0 + jnp.tanh(math.sqrt(2.0 / math.pi) * (x + 0.044715 * (x * x * x))))


def _compress_rows(c, w1cat, posb, w1n, w2bd):
    r = jnp.dot(c.astype(BF16), w1cat, preferred_element_type=F32)
    return _compress_finish(r, posb, w1n, w2bd)


def _compress_finish(r, posb, w1n, w2bd):
    R = r.shape[0]
    bias = jnp.dot(posb, w1n, preferred_element_type=F32)[0:1]
    bias2 = jnp.concatenate([bias, bias], axis=1)
    hid = r[:, :2 * CMP_HIDDEN] + pltpu.roll(r[:, 2 * CMP_HIDDEN:], R - 1, 0) + bias2
    return jnp.dot(_gelu(hid).astype(BF16), w2bd, preferred_element_type=F32)


def _compress_kernel(c_ref, w1cat_ref, posb_ref, w1n_ref, w2bd_ref, o_ref):
    o_ref[0] = _compress_rows(c_ref[0], w1cat_ref[...], posb_ref[...], w1n_ref[...], w2bd_ref[...])


def _compress_weights(pos, w1, w2):
    w1r = w1.reshape(2, CMP_STRIDE, HEAD_DIM, CMP_HIDDEN)
    eye = jnp.eye(NSA_GROUPS, dtype=w1.dtype)
    big = jnp.einsum('isdh,gk->isgdkh', w1r, eye).reshape(2, CMP_STRIDE * KV_W, NSA_GROUPS * CMP_HIDDEN)
    w1cat = jnp.concatenate([big[0], big[1]], axis=1).astype(BF16)
    w2bd = jnp.einsum('hd,gk->ghkd', w2, eye).reshape(NSA_GROUPS * CMP_HIDDEN, KV_W).astype(BF16)
    posb = jnp.broadcast_to(pos.reshape(1, CMP_BLOCK * HEAD_DIM), (8, CMP_BLOCK * HEAD_DIM)).astype(BF16)
    return w1cat, posb, w1.astype(BF16), w2bd


def _compress_prompt(x, cw):
    B, T, _ = x.shape
    R = T // CMP_STRIDE
    c = x.reshape(B, R, CMP_STRIDE * KV_W)
    full = lambda a: pl.BlockSpec(a.shape, lambda b: (0,) * a.ndim)
    return pl.pallas_call(
        _compress_kernel,
        grid=(B,),
        in_specs=[pl.BlockSpec((1, R, CMP_STRIDE * KV_W), lambda b: (b, 0, 0))] + [full(a) for a in cw],
        out_specs=pl.BlockSpec((1, R, KV_W), lambda b: (b, 0, 0)),
        out_shape=jax.ShapeDtypeStruct((B, R, KV_W), F32),
        compiler_params=_cparams(("arbitrary",)),
        name="compress_prompt",
    )(c, *cw)


def _lane_group_masks(shape):
    lane = lax.broadcasted_iota(jnp.int32, shape, 1)
    lo = lane < HEAD_DIM
    return lo, jnp.logical_not(lo)


def _block_diag_rows(x, dtype):
    lo, hi = _lane_group_masks(x.shape)
    return jnp.concatenate([jnp.where(lo, x, 0.0), jnp.where(hi, x, 0.0)], axis=0).astype(dtype)


def _dot_nt(a, b):
    return lax.dot_general(a, b, (((1,), (1,)), ((), ())), preferred_element_type=F32)


SUM_ROWS = 16


def _values_t(v_bd):
    n = v_bd.shape[0] // NSA_GROUPS
    row = lax.broadcasted_iota(jnp.int32, (SUM_ROWS, v_bd.shape[0]), 0)
    col = lax.broadcasted_iota(jnp.int32, (SUM_ROWS, v_bd.shape[0]), 1)
    ones = jnp.where(row == col // n, 1.0, 0.0)
    return jnp.concatenate([v_bd.T, ones], axis=0).astype(BF16)


def _flash_step_t(s_t, mask_t, m_ref, acc_ref, v_t, tk):
    ps, alphas = [], []
    for g in range(NSA_GROUPS):
        sg = s_t[g * tk:(g + 1) * tk]
        if mask_t is not None:
            sg = jnp.where(mask_t, sg, NEG_INF)
        m_old = m_ref[g]
        m_new = jnp.maximum(m_old, jnp.max(sg, axis=0, keepdims=True))
        m_safe = jnp.where(m_new == NEG_INF, 0.0, m_new)
        alphas.append(jnp.exp2(m_old - m_safe))
        ps.append(jnp.exp2(sg - m_safe).astype(BF16))
        m_ref[g] = m_new
        rs = slice(g * HEAD_DIM, (g + 1) * HEAD_DIM)
        acc_ref[rs, :] = alphas[g] * acc_ref[rs, :]
    srow = lax.broadcasted_iota(jnp.int32, (SUM_ROWS, 1), 0)
    acc_ref[KV_W:, :] = jnp.where(srow == 0, alphas[0], alphas[1]) * acc_ref[KV_W:, :]
    acc_ref[...] = acc_ref[...] + jnp.dot(v_t, jnp.concatenate(ps, axis=0), preferred_element_type=F32)


def _finish_t(acc_ref):
    parts = [acc_ref[g * HEAD_DIM:(g + 1) * HEAD_DIM, :]
             * (1.0 / jnp.maximum(acc_ref[KV_W + g:KV_W + g + 1, :], 1e-30)) for g in range(NSA_GROUPS)]
    return jnp.concatenate(parts, axis=0)


def _reset(m_ref, acc_ref):
    m_ref[...] = jnp.full(m_ref.shape, NEG_INF, F32)
    acc_ref[...] = jnp.zeros(acc_ref.shape, F32)


def _queries_t(q, tq):
    return jnp.concatenate([(q[:, r * LANES:(r + 1) * LANES] * SCALE_LOG2E).T for r in range(NSA_REP)],
                           axis=1).astype(BF16)


def _outputs_from_t(o_t, tq):
    return jnp.concatenate([o_t[:, r * tq:(r + 1) * tq].T for r in range(NSA_REP)], axis=1)


def _split2(x):
    hi = x.astype(BF16)
    return hi, (x - hi.astype(F32)).astype(BF16)


def _select_blocks(score_t, q_pos_row, n_sel):
    n_pad = score_t.shape[0] // NSA_GROUPS
    tq = score_t.shape[1]
    out = []
    blk = lax.broadcasted_iota(jnp.int32, (n_pad, tq), 0)
    cur = q_pos_row // SEL_BLOCK
    forced = (blk == 0) | (blk == cur) | (blk == cur - 1)
    ok = (blk * SEL_BLOCK <= q_pos_row) & (blk < n_sel)
    for g in range(NSA_GROUPS):
        sc = score_t[g * n_pad:(g + 1) * n_pad]
        sc = jnp.where(ok, sc + jnp.where(forced, FORCE_BONUS, 0.0), NEG_INF)
        cnt = jnp.zeros((n_pad, tq), F32)
        for i in range(n_pad):
            row = sc[i:i + 1]
            cnt = cnt + jnp.where(blk > i, jnp.where(row >= sc, 1.0, 0.0), jnp.where(row > sc, 1.0, 0.0))
        out.append(jnp.where((cnt < float(SEL_TOPN)) & ok, 1.0, 0.0))
    return jnp.concatenate(out, axis=0)


def _nsa_prompt_kernel(qraw_ref, qrot_ref, gate_ref, kc_ref, vc_ref, sk_ref, sv_ref, wk_ref, wv_ref,
                       ovt_ref, gexp_ref, o_ref,
                       skbd, svt, wkbd, wvt, kcbd, vct, sel_ref, m_ref, acc_ref, *, tq, tk, seq):
    i = pl.program_id(1)
    n_kt = seq // tk
    n_sel = seq // SEL_BLOCK
    n_cmp = kc_ref.shape[1]
    n_pad = sel_ref.shape[0] // NSA_GROUPS
    L = NSA_REP * tq
    blk_per_tile = tk // SEL_BLOCK

    @pl.when(i == 0)
    def _():
        def fill(j, _):
            rows = pl.ds(pl.multiple_of(j * tk, tk), tk)
            skbd[j] = _block_diag_rows(sk_ref[0, rows, :], BF16)
            wkbd[j] = _block_diag_rows(wk_ref[0, rows, :], BF16)
            svt[j] = _values_t(_block_diag_rows(sv_ref[0, rows, :], F32))
            wvt[j] = _values_t(_block_diag_rows(wv_ref[0, rows, :], F32))
            return 0
        lax.fori_loop(0, n_kt, fill, 0)
        kcbd[...] = _block_diag_rows(kc_ref[0], BF16)
        vct[...] = _block_diag_rows(vc_ref[0], F32).T.astype(BF16)

    q0 = i * tq
    q_pos_row = q0 + lax.broadcasted_iota(jnp.int32, (1, tq), 1)
    q_pos = jnp.concatenate([q_pos_row] * NSA_REP, axis=1)

    s = jnp.dot(kcbd[...], _queries_t(qraw_ref[...], tq), preferred_element_type=F32)
    kidx = lax.broadcasted_iota(jnp.int32, (n_cmp, 1), 0)
    cmask = (kidx * CMP_STRIDE + (CMP_BLOCK - 1) <= q_pos) & (kidx < n_cmp - 1)
    pcs = []
    for g in range(NSA_GROUPS):
        sg = jnp.where(cmask, s[g * n_cmp:(g + 1) * n_cmp], NEG_INF)
        m = jnp.max(sg, axis=0, keepdims=True)
        m = jnp.where(m == NEG_INF, 0.0, m)
        p = jnp.exp2(sg - m)
        pcs.append(p * (1.0 / jnp.maximum(jnp.sum(p, axis=0, keepdims=True), 1e-30)))
    pc = jnp.concatenate(pcs, axis=0)
    o_c = jnp.dot(vct[...], pc.astype(BF16), preferred_element_type=F32)
    imp = pc[:, 0:tq] + pc[:, tq:2 * tq] + pc[:, 2 * tq:3 * tq] + pc[:, 3 * tq:4 * tq]
    ih, il = _split2(imp)
    score = (jnp.dot(ovt_ref[...], ih, preferred_element_type=F32)
             + jnp.dot(ovt_ref[...], il, preferred_element_type=F32))
    sel = _select_blocks(score, q_pos_row, n_sel)
    for jt in range(n_pad // blk_per_tile):
        for g in range(NSA_GROUPS):
            src = g * n_pad + jt * blk_per_tile
            dst = (jt * NSA_GROUPS + g) * blk_per_tile
            sel_ref[dst:dst + blk_per_tile, :] = sel[src:src + blk_per_tile]

    qr = _queries_t(qrot_ref[...], tq)
    kiota = lax.broadcasted_iota(jnp.int32, (tk, 1), 0)

    n_bias = NSA_GROUPS * blk_per_tile

    def sel_step(causal):
        def step(j, _):
            s = jnp.dot(skbd[j], qr, preferred_element_type=F32)
            sel8 = sel_ref[pl.ds(pl.multiple_of(j * n_bias, n_bias), n_bias), :]
            parts = [jnp.broadcast_to(sel8[r:r + 1], (SEL_BLOCK, tq)) for r in range(n_bias)]
            m1 = jnp.concatenate(parts, axis=0) > 0.5
            s = jnp.where(jnp.concatenate([m1] * NSA_REP, axis=1), s, NEG_INF)
            mask = ((j * tk + kiota) <= q_pos) if causal else None
            _flash_step_t(s, mask, m_ref, acc_ref, svt[j], tk)
            return 0
        return step

    _reset(m_ref, acc_ref)
    n_full = q0 // tk
    lax.fori_loop(0, n_full, sel_step(False), 0)
    lax.fori_loop(n_full, (q0 + tq - 1) // tk + 1, sel_step(True), 0)
    o_s = _finish_t(acc_ref)

    _reset(m_ref, acc_ref)

    def win_step(j, _):
        s = jnp.dot(wkbd[j], qr, preferred_element_type=F32)
        kpos = j * tk + kiota
        _flash_step_t(s, (kpos <= q_pos) & (kpos > q_pos - WINDOW), m_ref, acc_ref, wvt[j], tk)
        return 0

    lax.fori_loop(jnp.maximum(q0 - WINDOW + 1, 0) // tk, (q0 + tq - 1) // tk + 1, win_step, 0)
    o_w = _finish_t(acc_ref)

    gh, gl = _split2(gate_ref[...])
    gx = (jnp.dot(gh, gexp_ref[...], preferred_element_type=F32)
          + jnp.dot(gl, gexp_ref[...], preferred_element_type=F32))
    o = (gx[:, 0:NSA_WIDTH] * _outputs_from_t(o_c, tq)
         + gx[:, NSA_WIDTH:2 * NSA_WIDTH] * _outputs_from_t(o_s, tq)
         + gx[:, 2 * NSA_WIDTH:3 * NSA_WIDTH] * _outputs_from_t(o_w, tq))
    o_ref[...] = o


def _overlap_matrix(n_cmp_rows, n_sel_pad):
    cs = np.arange(n_cmp_rows)[:, None] * CMP_STRIDE
    ss = np.arange(n_sel_pad)[None, :] * SEL_BLOCK
    ov = ((cs < ss + SEL_BLOCK) & (cs + CMP_BLOCK > ss)).astype(np.float32)
    z = np.zeros_like(ov)
    return jnp.asarray(np.block([[ov, z], [z, ov]]), BF16)


def _gate_expand_matrix():
    e = np.zeros((LANES, 3 * NSA_WIDTH), np.float32)
    for c in range(3):
        for r in range(NSA_REP):
            for g in range(NSA_GROUPS):
                col = c * NSA_WIDTH + r * LANES + g * HEAD_DIM
                e[c * NSA_HEADS + r * NSA_GROUPS + g, col:col + HEAD_DIM] = 1.0
    return jnp.asarray(e, BF16)


def _nsa_prompt(qraw, qrot, gate, kc, vc, sk, sv, wk, wv, B, T, tq, tk):
    nq = T // tq
    n_cmp = kc.shape[1]
    n_sel_pad = max(T // SEL_BLOCK, HEAD_DIM)
    ovt = _overlap_matrix(n_cmp, n_sel_pad).T
    gexp = _gate_expand_matrix()
    qspec = lambda w: pl.BlockSpec((tq, w), lambda b, i: (b * nq + i, 0))
    kvspec = lambda n: pl.BlockSpec((1, n, KV_W), lambda b, i: (b, 0, 0))
    const = lambda a: pl.BlockSpec(a.shape, lambda b, i: (0, 0))
    L = NSA_REP * tq
    n_kt = T // tk
    kern = functools.partial(_nsa_prompt_kernel, tq=tq, tk=tk, seq=T)
    kv3 = lambda a: a.reshape(B, T, KV_W)
    return pl.pallas_call(
        kern,
        grid=(B, nq),
        in_specs=[qspec(NSA_WIDTH), qspec(NSA_WIDTH), qspec(LANES), kvspec(n_cmp), kvspec(n_cmp),
                  kvspec(T), kvspec(T), kvspec(T), kvspec(T), const(ovt), const(gexp)],
        out_specs=qspec(NSA_WIDTH),
        out_shape=jax.ShapeDtypeStruct((B * T, NSA_WIDTH), F32),
        scratch_shapes=[pltpu.VMEM((n_kt, 2 * tk, KV_W), BF16),
                        pltpu.VMEM((n_kt, KV_W + SUM_ROWS, 2 * tk), BF16),
                        pltpu.VMEM((n_kt, 2 * tk, KV_W), BF16),
                        pltpu.VMEM((n_kt, KV_W + SUM_ROWS, 2 * tk), BF16),
                        pltpu.VMEM((2 * n_cmp, KV_W), BF16), pltpu.VMEM((KV_W, 2 * n_cmp), BF16),
                        pltpu.VMEM((NSA_GROUPS * n_sel_pad, tq), F32),
                        pltpu.VMEM((NSA_GROUPS, 1, L), F32),
                        pltpu.VMEM((KV_W + SUM_ROWS, L), F32)],
        compiler_params=_cparams(("arbitrary", "arbitrary")),
        name="nsa_prompt",
    )(qraw, qrot, gate, kc, vc, kv3(sk), kv3(sv), kv3(wk), kv3(wv), ovt, gexp)


def _lambda_value(lam_ref, lam_init):
    a = jnp.sum(lam_ref[0:1] * lam_ref[1:2], axis=-1, keepdims=True)
    b = jnp.sum(lam_ref[2:3] * lam_ref[3:4], axis=-1, keepdims=True)
    return jnp.exp(a) - jnp.exp(b) + lam_init


def _diff_merge_rows(o0, o1, lam, g, lam_init):
    od = o0 - lam * o1
    od = od * lax.rsqrt(jnp.mean(od * od, axis=-1, keepdims=True) + RMS_EPS) * g
    return od * (1.0 - lam_init)


def _diff_prompt_kernel(lam_ref, dq_ref, dk_ref, dv_ref, gcol_ref, o_ref,
                        kbf, vbf, m_ref, acc_ref, *, tq, tk, lam_init):
    i = pl.program_id(2)
    n_kt = kbf.shape[0] // tk

    @pl.when(i == 0)
    def _():
        kbf[...] = dk_ref[...].astype(BF16)
        ones = jnp.where(lax.broadcasted_iota(jnp.int32, (SUM_ROWS, tk), 0) == 0, 1.0, 0.0)

        def fill(j, _):
            vt = dv_ref[pl.ds(pl.multiple_of(j * tk, tk), tk), :].T
            vbf[j] = jnp.concatenate([vt, ones], axis=0).astype(BF16)
            return 0
        lax.fori_loop(0, n_kt, fill, 0)

    q = dq_ref[...] * SCALE_LOG2E
    lo, hi = _lane_group_masks(q.shape)
    q2t = jnp.concatenate([jnp.where(lo, q, 0.0), jnp.where(hi, q, 0.0)], axis=0).T.astype(BF16)
    q0 = i * tq
    q_pos = q0 + lax.broadcasted_iota(jnp.int32, (1, 2 * tq), 1) % tq
    kiota = lax.broadcasted_iota(jnp.int32, (tk, 1), 0)
    m_ref[...] = jnp.full(m_ref.shape, NEG_INF, F32)
    acc_ref[...] = jnp.zeros(acc_ref.shape, F32)

    def make_step(causal):
        def step(j, _):
            k0 = pl.multiple_of(j * tk, tk)
            s = jnp.dot(kbf[pl.ds(k0, tk), :], q2t, preferred_element_type=F32)
            if causal:
                s = jnp.where((k0 + kiota) <= q_pos, s, NEG_INF)
            m_old = m_ref[...]
            m_new = jnp.maximum(m_old, jnp.max(s, axis=0, keepdims=True))
            alpha = jnp.exp2(m_old - m_new)
            p = jnp.exp2(s - m_new).astype(BF16)
            m_ref[...] = m_new
            acc_ref[...] = alpha * acc_ref[...] + jnp.dot(vbf[j], p, preferred_element_type=F32)
            return 0
        return step

    n_full = q0 // tk
    lax.fori_loop(0, n_full, make_step(False), 0)
    lax.fori_loop(n_full, (q0 + tq - 1) // tk + 1, make_step(True), 0)
    o = acc_ref[0:DIFF_VDIM, :] * (1.0 / acc_ref[DIFF_VDIM:DIFF_VDIM + 1, :])
    lam = _lambda_value(lam_ref, lam_init)
    od = o[:, :tq] - lam * o[:, tq:]
    od = od * lax.rsqrt(jnp.mean(od * od, axis=0, keepdims=True) + RMS_EPS) * gcol_ref[...]
    o_ref[...] = (od * (1.0 - lam_init)).T


def _diff_prompt(lam4, dq, dk, dv, subln_g, B, T, tq, tk, lam_init):
    nq = T // tq
    kern = functools.partial(_diff_prompt_kernel, tq=tq, tk=tk, lam_init=lam_init)
    return pl.pallas_call(
        kern,
        grid=(B, DIFF_HEADS, nq),
        in_specs=[pl.BlockSpec((4, HEAD_DIM), lambda b, h, i: (0, 0)),
                  pl.BlockSpec((tq, DIFF_VDIM), lambda b, h, i: (b * nq + i, h)),
                  pl.BlockSpec((T, DIFF_VDIM), lambda b, h, i: (b, h)),
                  pl.BlockSpec((T, DIFF_VDIM), lambda b, h, i: (b, h)),
                  pl.BlockSpec((DIFF_VDIM, 1), lambda b, h, i: (0, 0))],
        out_specs=pl.BlockSpec((tq, DIFF_VDIM), lambda b, h, i: (b * nq + i, h)),
        out_shape=jax.ShapeDtypeStruct((B * T, DIFF_WIDTH), F32),
        scratch_shapes=[pltpu.VMEM((T, DIFF_VDIM), BF16), pltpu.VMEM((T // tk, DIFF_VDIM + SUM_ROWS, tk), BF16),
                        pltpu.VMEM((1, 2 * tq), F32), pltpu.VMEM((DIFF_VDIM + SUM_ROWS, 2 * tq), F32)],
        compiler_params=_cparams(("arbitrary", "arbitrary", "arbitrary")),
        name="diff_prompt",
    )(lam4, dq, dk, dv, subln_g.reshape(DIFF_VDIM, 1))


FF_CHUNK = 1408


def _rms(x, g):
    return x * lax.rsqrt(jnp.mean(x * x, axis=-1, keepdims=True) + RMS_EPS) * g


def _ffn_seq_kernel(x_ref, on_ref, od_ref, won_ref, wod_ref, g2_ref, wup_ref, cw_ref, cb_ref, wdn_ref,
                    gf_ref, y_ref, conv_ref, carry_ref, *, tm, tiles_per_seq):
    i = pl.program_id(0)
    first = (i % tiles_per_seq) == 0
    x1 = (x_ref[...] + jnp.dot(on_ref[...].astype(BF16), won_ref[...], preferred_element_type=F32)
          + jnp.dot(od_ref[...].astype(BF16), wod_ref[...], preferred_element_type=F32))
    h = _rms(x1, g2_ref[...]).astype(BF16)
    rowi = lax.broadcasted_iota(jnp.int32, (tm, 1), 0)
    f = jnp.zeros((tm, D_MODEL), F32)
    for k in range(D_FF // FF_CHUNK):
        halves = []
        for part in range(2):
            c0 = part * D_FF + k * FF_CHUNK
            cs = slice(c0, c0 + FF_CHUNK)
            u = jnp.dot(h, wup_ref[:, cs], preferred_element_type=F32)
            pm2 = jnp.where(first, 0.0, carry_ref[6:7, cs])
            pm1 = jnp.where(first, 0.0, carry_ref[7:8, cs])
            u1 = jnp.where(rowi == 0, pm1, pltpu.roll(u, 1, 0))
            u2 = jnp.where(rowi == 0, pm2, jnp.where(rowi == 1, pm1, pltpu.roll(u, 2, 0)))
            halves.append(cb_ref[:, cs] + u2 * cw_ref[0:1, cs] + u1 * cw_ref[1:2, cs] + u * cw_ref[2:3, cs])
            carry_ref[6:8, cs] = u[tm - 2:tm]
            conv_ref[0, :, cs] = u[tm - 2:tm]
        act = (jax.nn.silu(halves[0]) * halves[1]).astype(BF16)
        f = f + jnp.dot(act, wdn_ref[k * FF_CHUNK:(k + 1) * FF_CHUNK, :], preferred_element_type=F32)
    y_ref[...] = _rms(x1 + f, gf_ref[...])


def _ffn_step_kernel(x_ref, on_ref, od_ref, won_ref, wod_ref, g2_ref, wup_ref, cw_ref, cb_ref, wdn_ref,
                     gf_ref, p0_ref, p1_ref, y_ref, u_ref):
    x1 = (x_ref[...] + jnp.dot(on_ref[...].astype(BF16), won_ref[...], preferred_element_type=F32)
          + jnp.dot(od_ref[...].astype(BF16), wod_ref[...], preferred_element_type=F32))
    h = _rms(x1, g2_ref[...]).astype(BF16)
    f = jnp.zeros(x1.shape, F32)
    for k in range(D_FF // FF_CHUNK):
        halves = []
        for part in range(2):
            c0 = part * D_FF + k * FF_CHUNK
            cs = slice(c0, c0 + FF_CHUNK)
            u = jnp.dot(h, wup_ref[:, cs], preferred_element_type=F32)
            u_ref[:, cs] = u
            halves.append(cb_ref[:, cs] + p0_ref[:, cs] * cw_ref[0:1, cs] + p1_ref[:, cs] * cw_ref[1:2, cs]
                          + u * cw_ref[2:3, cs])
        act = (jax.nn.silu(halves[0]) * halves[1]).astype(BF16)
        f = f + jnp.dot(act, wdn_ref[k * FF_CHUNK:(k + 1) * FF_CHUNK, :], preferred_element_type=F32)
    y_ref[...] = _rms(x1 + f, gf_ref[...])


def _const_spec(a, ngrid):
    return pl.BlockSpec(a.shape, lambda *idx: (0,) * a.ndim, pipeline_mode=pl.Buffered(1))


def _ffn_weights_specs(ws):
    return [_const_spec(a, 1) for a in ws]


def _ffn_seq(x2d, o_nsa, o_diff, ws, B, T, tm):
    M = x2d.shape[0]
    tps = T // tm
    row = lambda w: pl.BlockSpec((tm, w), lambda i: (i, 0))
    kern = functools.partial(_ffn_seq_kernel, tm=tm, tiles_per_seq=tps)
    return pl.pallas_call(
        kern,
        grid=(M // tm,),
        in_specs=[row(D_MODEL), row(NSA_WIDTH), row(DIFF_WIDTH)] + _ffn_weights_specs(ws),
        out_specs=[row(D_MODEL), pl.BlockSpec((1, CONV_W - 1, 2 * D_FF), lambda i: (i // tps, 0, 0))],
        out_shape=[jax.ShapeDtypeStruct((M, D_MODEL), F32),
                   jax.ShapeDtypeStruct((B, CONV_W - 1, 2 * D_FF), F32)],
        scratch_shapes=[pltpu.VMEM((8, 2 * D_FF), F32)],
        compiler_params=_cparams(("arbitrary",)),
        name="ffn_prompt",
    )(x2d, o_nsa, o_diff, *ws)


def _ffn_step(x2d, o_nsa, o_diff, ws, p0, p1):
    M = x2d.shape[0]
    full = lambda a: pl.BlockSpec(a.shape, lambda i: (0,) * a.ndim)
    ins = (x2d, o_nsa, o_diff) + tuple(ws) + (p0, p1)
    return pl.pallas_call(
        _ffn_step_kernel,
        grid=(1,),
        in_specs=[full(a) for a in ins],
        out_specs=[pl.BlockSpec((M, D_MODEL), lambda i: (0, 0)), pl.BlockSpec((M, 2 * D_FF), lambda i: (0, 0))],
        out_shape=[jax.ShapeDtypeStruct((M, D_MODEL), F32), jax.ShapeDtypeStruct((M, 2 * D_FF), F32)],
        compiler_params=_cparams(("arbitrary",)),
        name="ffn_sample",
    )(*ins)


ROWS_PER_PAGE = PAGE_SIZE // CMP_STRIDE


def _compress_sample_kernel(pt_ref, pool_ref, tail_ref, w1cat_ref, posb_ref, w1n_ref, w2bd_ref, o_ref,
                            pbuf, xs, sem, *, n_pages, nb, rows):
    b = pl.program_id(0)
    past = n_pages * PAGE_SIZE

    def page_copy(bb, n, slot):
        return pltpu.make_async_copy(pool_ref.at[pt_ref[bb * n_pages + n]], pbuf.at[slot, n], sem.at[slot])

    def start_all(bb, slot):
        def body(n, _):
            page_copy(bb, n, slot).start()
            return 0
        lax.fori_loop(0, n_pages, body, 0)

    @pl.when(b == 0)
    def _():
        start_all(0, 0)

    @pl.when(b + 1 < nb)
    def _():
        start_all(b + 1, (b + 1) % 2)

    slot = b % 2

    def wait_body(n, _):
        page_copy(b, n, slot).wait()
        return 0
    lax.fori_loop(0, n_pages, wait_body, 0)

    unroll = 8

    def untranspose(n8, _):
        for u in range(unroll):
            n = n8 * unroll + u
            xs[pl.ds(pl.multiple_of(n * PAGE_SIZE, PAGE_SIZE), PAGE_SIZE), :] = pbuf[slot, n].T
        return 0
    lax.fori_loop(0, n_pages // unroll, untranspose, 0)
    xs[past:past + 8, :] = tail_ref[0]
    xs[past + 8:, :] = jnp.zeros((xs.shape[0] - past - 8, KV_W), F32)

    r = None
    for s in range(0, CMP_STRIDE, 2):
        lhs = jnp.concatenate([xs[pl.ds(s, rows, stride=CMP_STRIDE), :],
                               xs[pl.ds(s + 1, rows, stride=CMP_STRIDE), :]], axis=1).astype(BF16)
        part = jnp.dot(lhs, w1cat_ref[s * KV_W:(s + 2) * KV_W, :], preferred_element_type=F32)
        r = part if r is None else r + part
    o_ref[0, 0:rows, :] = _compress_finish(r, posb_ref[...], w1n_ref[...], w2bd_ref[...])
    o_ref[0, rows:, :] = jnp.zeros((o_ref.shape[1] - rows, KV_W), F32)


def _compress_sample(pool, pt_flat, new_row, cw, nb, n_pages, rows, rows_out):
    pool_t = _pages_transposed(pool)
    tail = jnp.zeros((nb, 8, KV_W), F32).at[:, 0, :].set(new_row)
    kern = functools.partial(_compress_sample_kernel, n_pages=n_pages, nb=nb, rows=rows)
    full = lambda a: pl.BlockSpec(a.shape, lambda b, pt: (0,) * a.ndim)
    return pl.pallas_call(
        kern,
        grid_spec=pltpu.PrefetchScalarGridSpec(
            num_scalar_prefetch=1,
            grid=(nb,),
            in_specs=[pl.BlockSpec(memory_space=pl.ANY),
                      pl.BlockSpec((1, 8, KV_W), lambda b, pt: (b, 0, 0))] + [full(a) for a in cw],
            out_specs=pl.BlockSpec((1, rows_out, KV_W), lambda b, pt: (b, 0, 0)),
            scratch_shapes=[pltpu.VMEM((2, n_pages, KV_W, PAGE_SIZE), F32),
                            pltpu.VMEM((rows * CMP_STRIDE, KV_W), F32), pltpu.SemaphoreType.DMA((2,))]),
        out_shape=jax.ShapeDtypeStruct((nb, rows_out, KV_W), F32),
        compiler_params=_cparams(("arbitrary",)),
        name="compress_sample",
    )(pt_flat, pool_t, tail, *cw)


def _pages_transposed(pool):
    n_pool = pool.shape[0]
    return jnp.transpose(pool, (0, 2, 3, 1)).reshape(n_pool, KV_W, pool.shape[1])


def _pad_rows8(x):
    return jnp.concatenate([x, jnp.zeros((8 - x.shape[0], x.shape[1]), x.dtype)], axis=0)


def _nsa_sample_a_kernel(q_ref, kc_ref, vc_ref, ovbd_ref, oc_ref, info_ref, *, n_cmp, n_sel, q_pos):
    rk = kc_ref.shape[1]
    n_pad = ovbd_ref.shape[1] // NSA_GROUPS
    q8 = _pad_rows8(q_ref[0]).astype(BF16)
    kcbd = _block_diag_rows(kc_ref[0], BF16)
    vcbd = _block_diag_rows(vc_ref[0], BF16)
    s = _dot_nt(q8, kcbd)
    kidx = lax.broadcasted_iota(jnp.int32, s.shape, 1) % rk
    cmask = (kidx * CMP_STRIDE + (CMP_BLOCK - 1) <= q_pos) & (kidx < n_cmp)
    s = jnp.where(cmask, s * ATTN_SCALE, NEG_INF)
    pcs = []
    for g in range(NSA_GROUPS):
        sg = s[:, g * rk:(g + 1) * rk]
        m = jnp.max(sg, axis=-1, keepdims=True)
        m = jnp.where(m == NEG_INF, 0.0, m)
        p = jnp.exp(sg - m)
        pcs.append(p / jnp.maximum(jnp.sum(p, axis=-1, keepdims=True), 1e-30))
    pc = jnp.concatenate(pcs, axis=1)
    oc_ref[0] = jnp.dot(pc.astype(BF16), vcbd, preferred_element_type=F32)
    imp = jnp.sum(pc[0:NSA_REP], axis=0, keepdims=True)
    imp8 = jnp.concatenate([imp] * 8, axis=0)
    ih, il = _split2(imp8)
    score = (jnp.dot(ih, ovbd_ref[...], preferred_element_type=F32)
             + jnp.dot(il, ovbd_ref[...], preferred_element_type=F32))[0:1]

    blk_r = lax.broadcasted_iota(jnp.int32, (1, n_pad), 1)
    blk_c = lax.broadcasted_iota(jnp.int32, (n_pad, 1), 0)
    cur = q_pos // SEL_BLOCK
    forced_r = (blk_r == 0) | (blk_r == cur) | (blk_r == cur - 1)
    ok_r = (blk_r * SEL_BLOCK <= q_pos) & (blk_r < n_sel)
    ok_c = (blk_c * SEL_BLOCK <= q_pos) & (blk_c < n_sel)
    ii = lax.broadcasted_iota(jnp.int32, (n_pad, n_pad), 1)
    jj = lax.broadcasted_iota(jnp.int32, (n_pad, n_pad), 0)
    kk = lax.broadcasted_iota(jnp.int32, (n_pad, LANES), 1)
    jf = lax.broadcasted_iota(jnp.int32, (n_pad, LANES), 0).astype(F32)
    rows_out = []
    for g in range(NSA_GROUPS):
        sr = score[:, g * n_pad:(g + 1) * n_pad]
        sr = jnp.where(ok_r, sr + jnp.where(forced_r, FORCE_BONUS, 0.0), NEG_INF)
        sc = jnp.sum(jnp.where(ii == jj, sr, 0.0), axis=1, keepdims=True)
        ahead = jnp.where(ii < jj, jnp.where(sr >= sc, 1.0, 0.0), jnp.where(sr > sc, 1.0, 0.0))
        cnt = jnp.sum(ahead, axis=1, keepdims=True)
        chosen = (cnt < float(SEL_TOPN)) & ok_c
        hit = chosen & (cnt == kk.astype(F32))
        rows_out.append((jnp.sum(jnp.where(hit, jf, 0.0), axis=0, keepdims=True),
                         jnp.sum(jnp.where(hit, 1.0, 0.0), axis=0, keepdims=True)))
    info = jnp.concatenate([rows_out[0][0], rows_out[1][0], rows_out[0][1], rows_out[1][1],
                            jnp.zeros((4, LANES), F32)], axis=0)
    info_ref[0] = info.astype(jnp.int32)


def _nsa_sample_a(qraw3, kc, vc, q_pos, n_cmp, n_sel):
    nb, rk, _ = kc.shape
    n_pad = -(-n_sel // LANES) * LANES
    ovbd = _overlap_matrix(rk, n_pad)
    kern = functools.partial(_nsa_sample_a_kernel, n_cmp=n_cmp, n_sel=n_sel, q_pos=q_pos)
    return pl.pallas_call(
        kern,
        grid=(nb,),
        in_specs=[pl.BlockSpec((1, NSA_REP, LANES), lambda b: (b, 0, 0)),
                  pl.BlockSpec((1, rk, KV_W), lambda b: (b, 0, 0)),
                  pl.BlockSpec((1, rk, KV_W), lambda b: (b, 0, 0)),
                  pl.BlockSpec(ovbd.shape, lambda b: (0, 0))],
        out_specs=[pl.BlockSpec((1, 8, LANES), lambda b: (b, 0, 0)),
                   pl.BlockSpec((1, 8, LANES), lambda b: (b, 0, 0))],
        out_shape=[jax.ShapeDtypeStruct((nb, 8, LANES), F32), jax.ShapeDtypeStruct((nb, 8, LANES), jnp.int32)],
        compiler_params=_cparams(("arbitrary",)),
        name="nsa_sample_select",
    )(qraw3, kc, vc, ovbd)


N_SLOTS = NSA_GROUPS * SEL_TOPN


def _decode_attend(q8f, kts, vts, masks, s_new, inc_new, v_new):
    lo, hi = _lane_group_masks((8, KV_W))
    outs = []
    for g, lanes in enumerate((lo, hi)):
        qg = jnp.where(lanes, q8f, 0.0).astype(BF16)
        sg = jnp.dot(qg, kts[g], preferred_element_type=F32) * ATTN_SCALE
        sg = jnp.where(masks[g], sg, NEG_INF)
        sn = jnp.where(inc_new[g] > 0.5, s_new[g], NEG_INF)
        m = jnp.maximum(jnp.max(sg, axis=-1, keepdims=True), sn)
        m = jnp.where(m == NEG_INF, 0.0, m)
        p = jnp.exp(sg - m)
        pnew = jnp.exp(sn - m)
        l = jnp.sum(p, axis=-1, keepdims=True) + pnew
        og = _dot_nt(p.astype(BF16), vts[g]) + pnew * v_new
        outs.append(og / jnp.maximum(l, 1e-30))
    return jnp.where(lo, outs[0], outs[1])


def _nsa_sample_b_kernel(sel_ref, pt_ref, q_ref, gate_ref, oc_ref, info_ref, psk_ref, psv_ref,
                         sknew_ref, svnew_ref, wink_ref, winv_ref, wknew_ref, wvnew_ref, eexp_ref, e2_ref,
                         o_ref, swk_ref, swv_ref, kbuf, vbuf, sem, *, nb, n_pages, n_past_blk, wbuf):
    b = pl.program_id(0)
    blk_per_page = PAGE_SIZE // SEL_BLOCK

    def copies(bb, idx, slot):
        j = jnp.minimum(sel_ref[bb * N_SLOTS + idx], n_past_blk - 1)
        page = pt_ref[bb * n_pages + j // blk_per_page]
        return (pltpu.make_async_copy(psk_ref.at[page], kbuf.at[slot, idx], sem.at[slot]),
                pltpu.make_async_copy(psv_ref.at[page], vbuf.at[slot, idx], sem.at[slot]))

    def start_all(bb, slot):
        def body(idx, _):
            ck, cv = copies(bb, idx, slot)
            ck.start()
            cv.start()
            return 0
        lax.fori_loop(0, N_SLOTS, body, 0)

    @pl.when(b == 0)
    def _():
        start_all(0, 0)

    @pl.when(b + 1 < nb)
    def _():
        start_all(b + 1, (b + 1) % 2)

    slot = b % 2

    def wait_body(idx, _):
        ck, cv = copies(b, idx, slot)
        ck.wait()
        cv.wait()
        return 0
    lax.fori_loop(0, N_SLOTS, wait_body, 0)

    q8f = _pad_rows8(q_ref[0])
    lo, hi = _lane_group_masks((8, KV_W))

    def new_scores(k_new):
        prod = q8f * k_new
        return [jnp.sum(jnp.where(lo, prod, 0.0), axis=-1, keepdims=True) * ATTN_SCALE,
                jnp.sum(jnp.where(hi, prod, 0.0), axis=-1, keepdims=True) * ATTN_SCALE]

    info = info_ref[0].astype(F32)
    lane1 = lax.broadcasted_iota(jnp.int32, (1, LANES), 1)
    n_keys = SEL_TOPN * PAGE_SIZE
    tok_half = (lax.broadcasted_iota(jnp.int32, (8, n_keys), 1) % PAGE_SIZE) // SEL_BLOCK
    kts, vts, masks, inc = [], [], [], []
    for g in range(NSA_GROUPS):
        kts.append(jnp.concatenate([kbuf[slot, g * SEL_TOPN + k] for k in range(SEL_TOPN)], axis=1).astype(BF16))
        vts.append(jnp.concatenate([vbuf[slot, g * SEL_TOPN + k] for k in range(SEL_TOPN)], axis=1).astype(BF16))
        idx8 = jnp.concatenate([info[g:g + 1]] * 8, axis=0).astype(BF16)
        val8 = jnp.concatenate([info[2 + g:3 + g]] * 8, axis=0).astype(BF16)
        jl = jnp.dot(idx8, eexp_ref[...], preferred_element_type=F32)
        vl = jnp.dot(val8, eexp_ref[...], preferred_element_type=F32)
        masks.append((vl > 0.5) & (jl < n_past_blk - 0.5)
                     & (jl.astype(jnp.int32) % blk_per_page == tok_half))
        is_new = (info[g:g + 1] > n_past_blk - 0.5) & (info[2 + g:3 + g] > 0.5) & (lane1 < SEL_TOPN)
        inc.append(jnp.sum(jnp.where(is_new, 1.0, 0.0), axis=-1, keepdims=True))
    o_s = _decode_attend(q8f, kts, vts, masks, new_scores(sknew_ref[0]), inc, svnew_ref[0])

    wk_old = wink_ref[0]
    wv_old = winv_ref[0]
    widx = lax.broadcasted_iota(jnp.int32, (1, wbuf), 1)
    wmask = widx > wbuf - WINDOW
    always = [jnp.ones((1, 1), F32)] * NSA_GROUPS
    o_w = _decode_attend(q8f, [wk_old.astype(BF16)] * 2, [wv_old.astype(BF16)] * 2, [wmask] * 2,
                         new_scores(wknew_ref[0]), always, wvnew_ref[0])
    eye = (lax.broadcasted_iota(jnp.int32, (KV_W, KV_W), 0) == lax.broadcasted_iota(jnp.int32, (KV_W, KV_W), 1))
    for new_ref, old, out_ref in ((wknew_ref, wk_old, swk_ref), (wvnew_ref, wv_old, swv_ref)):
        col = jnp.sum(jnp.where(eye, new_ref[0], 0.0), axis=1, keepdims=True)
        out_ref[0] = jnp.where(widx == wbuf - 1, col, pltpu.roll(old, wbuf - 1, 1))

    gate8 = jnp.concatenate([gate_ref[0]] * 8, axis=0)
    rr = lax.broadcasted_iota(jnp.int32, (8, LANES), 0)
    ll = lax.broadcasted_iota(jnp.int32, (8, LANES), 1)
    o = jnp.zeros((8, KV_W), F32)
    for c, ob in enumerate((oc_ref[0], o_s, o_w)):
        base = c * NSA_HEADS + rr * NSA_GROUPS
        gsel = jnp.where((ll >= base) & (ll < base + NSA_GROUPS), gate8, 0.0)
        gh, gl = _split2(gsel)
        gx = (jnp.dot(gh, e2_ref[...], preferred_element_type=F32)
              + jnp.dot(gl, e2_ref[...], preferred_element_type=F32))
        o = o + gx * ob
    o_ref[0] = o[0:NSA_REP]


def _nsa_sample_b(sel_flat, pt_flat, qrot3, gate3, oc, info, pool_sk, pool_sv, sknew, svnew,
                  wink, winv, wknew, wvnew, nb, n_pages):
    wbuf = wink.shape[2]
    n_past_blk = n_pages * (PAGE_SIZE // SEL_BLOCK)
    e = np.zeros((LANES, SEL_TOPN * PAGE_SIZE), np.float32)
    for k in range(SEL_TOPN):
        e[k, k * PAGE_SIZE:(k + 1) * PAGE_SIZE] = 1.0
    eexp = jnp.asarray(e, BF16)
    e2 = np.zeros((LANES, LANES), np.float32)
    for j in range(LANES):
        e2[j, (j % NSA_GROUPS) * HEAD_DIM:(j % NSA_GROUPS + 1) * HEAD_DIM] = 1.0
    e2 = jnp.asarray(e2, BF16)
    per_b = lambda r, w: pl.BlockSpec((1, r, w), lambda b, s, p: (b, 0, 0))
    const = lambda a: pl.BlockSpec(a.shape, lambda b, s, p: (0, 0))
    anyspec = pl.BlockSpec(memory_space=pl.ANY)
    kern = functools.partial(_nsa_sample_b_kernel, nb=nb, n_pages=n_pages, n_past_blk=n_past_blk, wbuf=wbuf)
    r3 = lambda a: a.reshape(nb, 1, KV_W)
    return pl.pallas_call(
        kern,
        grid_spec=pltpu.PrefetchScalarGridSpec(
            num_scalar_prefetch=2,
            grid=(nb,),
            in_specs=[per_b(NSA_REP, LANES), per_b(1, LANES), per_b(8, LANES), per_b(8, LANES),
                      anyspec, anyspec, per_b(1, KV_W), per_b(1, KV_W),
                      per_b(KV_W, wbuf), per_b(KV_W, wbuf), per_b(1, KV_W), per_b(1, KV_W),
                      const(eexp), const(e2)],
            out_specs=[per_b(NSA_REP, LANES), per_b(KV_W, wbuf), per_b(KV_W, wbuf)],
            scratch_shapes=[pltpu.VMEM((2, N_SLOTS, KV_W, PAGE_SIZE), F32),
                            pltpu.VMEM((2, N_SLOTS, KV_W, PAGE_SIZE), F32),
                            pltpu.SemaphoreType.DMA((2,))]),
        out_shape=[jax.ShapeDtypeStruct((nb, NSA_REP, LANES), F32),
                   jax.ShapeDtypeStruct((nb, KV_W, wbuf), F32),
                   jax.ShapeDtypeStruct((nb, KV_W, wbuf), F32)],
        compiler_params=_cparams(("arbitrary",)),
        name="nsa_sample_attend",
    )(sel_flat, pt_flat, qrot3, gate3, oc, info, _pages_transposed(pool_sk), _pages_transposed(pool_sv),
      r3(sknew), r3(svnew), wink, winv, r3(wknew), r3(wvnew), eexp, e2)


DIFF_PAGES_PER_STEP = 8


def _diff_sample_kernel(pt_ref, lam_ref, dq_ref, dknew_ref, dvnew_ref, g_ref, pk_ref, pv_ref, o_ref,
                        kbuf, vbuf, sem, m_ref, l_ref, acc_ref, *, nb, n_pages, lam_init):
    b = pl.program_id(0)
    c = pl.program_id(1)
    nc = n_pages // DIFF_PAGES_PER_STEP
    step = b * nc + c

    def copies(st, p, slot):
        bb = st // nc
        cc = st % nc
        page = pt_ref[bb * n_pages + cc * DIFF_PAGES_PER_STEP + p]
        dst = pl.ds(p * PAGE_SIZE * DIFF_HEADS, PAGE_SIZE * DIFF_HEADS)
        return (pltpu.make_async_copy(pk_ref.at[page], kbuf.at[slot, p], sem.at[slot]),
                pltpu.make_async_copy(pv_ref.at[page], vbuf.at[slot, dst, :], sem.at[slot]))

    def start_all(st, slot):
        for p in range(DIFF_PAGES_PER_STEP):
            ck, cv = copies(st, p, slot)
            ck.start()
            cv.start()

    @pl.when(step == 0)
    def _():
        start_all(0, 0)

    @pl.when(step + 1 < nb * nc)
    def _():
        start_all(step + 1, (step + 1) % 2)

    slot = step % 2
    for p in range(DIFF_PAGES_PER_STEP):
        ck, cv = copies(step, p, slot)
        ck.wait()
        cv.wait()

    @pl.when(c == 0)
    def _():
        m_ref[...] = jnp.full(m_ref.shape, NEG_INF, F32)
        l_ref[...] = jnp.zeros(l_ref.shape, F32)
        acc_ref[...] = jnp.zeros(acc_ref.shape, F32)

    rr = lax.broadcasted_iota(jnp.int32, (8, DIFF_WIDTH), 0)
    ll = lax.broadcasted_iota(jnp.int32, (8, DIFF_WIDTH), 1)
    own = (ll // HEAD_DIM) == rr
    q8f = jnp.where(own, jnp.concatenate([dq_ref[0]] * 8, axis=0), 0.0)
    q8 = q8f.astype(BF16)
    s = jnp.concatenate([jnp.dot(q8, kbuf[slot, p].astype(BF16), preferred_element_type=F32)
                         for p in range(DIFF_PAGES_PER_STEP)], axis=1) * ATTN_SCALE
    m_old = m_ref[...]
    m_new = jnp.maximum(m_old, jnp.max(s, axis=-1, keepdims=True))
    alpha = jnp.exp(m_old - m_new)
    p = jnp.exp(s - m_new)
    l_ref[...] = alpha * l_ref[...] + jnp.sum(p, axis=-1, keepdims=True)
    m_ref[...] = m_new
    pb = p.astype(BF16)
    keys = DIFF_PAGES_PER_STEP * PAGE_SIZE
    pv = jnp.concatenate([jnp.dot(pb, vbuf[slot, pl.ds(h, keys, stride=DIFF_HEADS), :].astype(BF16),
                                  preferred_element_type=F32) for h in range(DIFF_HEADS)], axis=1)
    acc_ref[...] = alpha * acc_ref[...] + pv

    @pl.when(c == nc - 1)
    def _():
        s_new = jnp.sum(q8f * dknew_ref[0], axis=-1, keepdims=True) * ATTN_SCALE
        m_old = m_ref[...]
        m_new = jnp.maximum(m_old, s_new)
        alpha = jnp.exp(m_old - m_new)
        p_new = jnp.exp(s_new - m_new)
        l = alpha * l_ref[...] + p_new
        o = (alpha * acc_ref[...] + p_new * dvnew_ref[0]) / l
        lam = _lambda_value(lam_ref, lam_init)
        for h in range(DIFF_HEADS):
            cs = slice(h * DIFF_VDIM, (h + 1) * DIFF_VDIM)
            o_ref[0, :, cs] = _diff_merge_rows(o[2 * h:2 * h + 1, cs], o[2 * h + 1:2 * h + 2, cs], lam,
                                               g_ref[...], lam_init)


def _diff_sample(pt_flat, lam4, dq, dknew, dvnew, subln_g, pool_k, pool_v, nb, n_pages, lam_init):
    n_pool = pool_k.shape[0]
    nc = n_pages // DIFF_PAGES_PER_STEP
    keys = DIFF_PAGES_PER_STEP * PAGE_SIZE
    per_b = pl.BlockSpec((1, 1, DIFF_WIDTH), lambda b, c, pt: (b, 0, 0))
    anyspec = pl.BlockSpec(memory_space=pl.ANY)
    kern = functools.partial(_diff_sample_kernel, nb=nb, n_pages=n_pages, lam_init=lam_init)
    r3 = lambda a: a.reshape(nb, 1, DIFF_WIDTH)
    return pl.pallas_call(
        kern,
        grid_spec=pltpu.PrefetchScalarGridSpec(
            num_scalar_prefetch=1,
            grid=(nb, nc),
            in_specs=[pl.BlockSpec((4, HEAD_DIM), lambda b, c, pt: (0, 0)), per_b, per_b, per_b,
                      pl.BlockSpec((1, DIFF_VDIM), lambda b, c, pt: (0, 0)), anyspec, anyspec],
            out_specs=per_b,
            scratch_shapes=[pltpu.VMEM((2, DIFF_PAGES_PER_STEP, DIFF_WIDTH, PAGE_SIZE), F32),
                            pltpu.VMEM((2, keys * DIFF_HEADS, DIFF_VDIM), F32),
                            pltpu.SemaphoreType.DMA((2,)),
                            pltpu.VMEM((8, 1), F32), pltpu.VMEM((8, 1), F32), pltpu.VMEM((8, DIFF_WIDTH), F32)]),
        out_shape=jax.ShapeDtypeStruct((nb, 1, DIFF_WIDTH), F32),
        compiler_params=_cparams(("arbitrary", "arbitrary")),
        name="diff_sample",
    )(pt_flat, lam4, r3(dq), r3(dknew), r3(dvnew), subln_g,
      jnp.transpose(pool_k, (0, 2, 3, 4, 1)).reshape(n_pool, DIFF_WIDTH, PAGE_SIZE),
      pool_v.reshape(n_pool, PAGE_SIZE * DIFF_HEADS, DIFF_VDIM))


def _prep_w_in(w):
    splits = (NSA_WIDTH, KV_W, KV_W, KV_W, KV_W, KV_W, KV_W, NSA_HEADS * 3, DIFF_WIDTH, DIFF_WIDTH, DIFF_WIDTH)
    cuts = [int(c) for c in np.cumsum(splits)[:-1]]
    q, ck, cv, sk, sv, wk, wv, gate, dq, dk, dv = jnp.split(w, cuts, axis=1)
    q = q.reshape(D_MODEL, NSA_GROUPS, NSA_REP, HEAD_DIM).transpose(0, 2, 1, 3).reshape(D_MODEL, NSA_WIDTH)
    gate = gate.reshape(D_MODEL, NSA_GROUPS, NSA_REP, 3).transpose(0, 3, 2, 1).reshape(D_MODEL, NSA_HEADS * 3)
    gate = jnp.pad(gate, ((0, 0), (0, LANES - NSA_HEADS * 3)))
    return jnp.concatenate([q, ck, cv, sk, sv, wk, wv, dq, dk, dv, gate], axis=1).astype(BF16)


def _prep_w_out(w):
    won = w[:NSA_WIDTH].reshape(NSA_GROUPS, NSA_REP, HEAD_DIM, D_MODEL).transpose(1, 0, 2, 3)
    return won.reshape(NSA_WIDTH, D_MODEL).astype(BF16), w[NSA_WIDTH:].astype(BF16)


def _prep_ffn(p):
    won, wod = _prep_w_out(p["w_out"][0])
    return (won, wod, p["ffn_norm"], p["w_up"][0].astype(BF16), p["conv_w"][0], p["conv_b"],
            p["w_down"][0].astype(BF16), p["final_norm"].reshape(1, D_MODEL))


def _rope_tables(pos):
    half = HEAD_DIM // 2
    inv = 1.0 / (ROPE_THETA ** (jnp.arange(half, dtype=F32) / half))
    ang = pos.astype(F32)[:, None] * inv[None, :]
    cos, sin = jnp.cos(ang), jnp.sin(ang)
    return jnp.tile(cos, (1, 4)), jnp.tile(jnp.concatenate([-sin, sin], axis=1), (1, 2))


def _prompt_group(xp, attn_norm, w_in_r, cw_k, cw_v, lam4, subln_g, ffn_ws, lam_init):
    B, T, _ = xp.shape
    x2d = xp.reshape(B * T, D_MODEL)
    tm = min(256, T)
    cos_t, sin_t = _rope_tables(jnp.arange(T, dtype=jnp.int32))
    qraw, qrot, ck, cv, sk, sv, wk, wv, gate, dq, dk, dv = _project(x2d, attn_norm, w_in_r, cos_t, sin_t, tm)
    kc = _compress_prompt(ck.reshape(B, T, KV_W), cw_k)
    vc = _compress_prompt(cv.reshape(B, T, KV_W), cw_v)
    o_nsa = _nsa_prompt(qraw, qrot, gate, kc, vc, sk, sv, wk, wv, B, T, tq=256, tk=256)
    o_diff = _diff_prompt(lam4, dq, dk, dv, subln_g, B, T, tq=min(512, T), tk=512, lam_init=lam_init)
    y, conv = _ffn_seq(x2d, o_nsa, o_diff, ffn_ws, B, T, tm)
    kv = lambda a: a.reshape(1, B, T, NSA_GROUPS, HEAD_DIM)
    wb = min(WINDOW, T)
    return (y.reshape(B, T, D_MODEL), kv(ck), kv(cv), kv(sk), kv(sv),
            dk.reshape(1, B, T, DIFF_HEADS, 2, HEAD_DIM), dv.reshape(1, B, T, DIFF_HEADS, DIFF_VDIM),
            kv(wk)[:, :, T - wb:], kv(wv)[:, :, T - wb:], conv[None])


def _sample_group(xs, caches, page_table, attn_norm, w_in_r, cw_k, cw_v, lam4, subln_g, ffn_ws, lam_init):
    pool_ck, pool_cv, pool_sk, pool_sv, pool_dk, pool_dv, win_k, win_v, conv_state = caches
    nb = xs.shape[0]
    n_pages = page_table.shape[1]
    past = n_pages * PAGE_SIZE
    t_pad = -(-(past + 1) // SEL_BLOCK) * SEL_BLOCK
    n_cmp = t_pad // CMP_STRIDE - CMP_BLOCK // CMP_STRIDE + 1
    n_sel = t_pad // SEL_BLOCK
    rows = -(-(t_pad // CMP_STRIDE) // 8) * 8
    rows_out = -(-rows // LANES) * LANES
    pt_flat = page_table.reshape(-1)
    x2d = xs.reshape(nb, D_MODEL)
    pos = jnp.full((nb,), past, dtype=jnp.int32)
    cos_t, sin_t = _rope_tables(pos)
    qraw, qrot, ck, cv, sk, sv, wk, wv, gate, dq, dk, dv = _project(x2d, attn_norm, w_in_r, cos_t, sin_t, nb)
    kc = _compress_sample(pool_ck, pt_flat, ck, cw_k, nb, n_pages, rows, rows_out)
    vc = _compress_sample(pool_cv, pt_flat, cv, cw_v, nb, n_pages, rows, rows_out)
    o_c, info = _nsa_sample_a(qraw.reshape(nb, NSA_REP, LANES), kc, vc, past, n_cmp, n_sel)
    sel_flat = info[:, 0:NSA_GROUPS, 0:SEL_TOPN].reshape(-1)
    wbuf = win_k.shape[1]
    o_nsa, swk, swv = _nsa_sample_b(sel_flat, pt_flat, qrot.reshape(nb, NSA_REP, LANES),
                                    gate.reshape(nb, 1, LANES), o_c, info, pool_sk, pool_sv, sk, sv,
                                    _pages_transposed(win_k), _pages_transposed(win_v), wk, wv, nb, n_pages)
    from_t = lambda a: jnp.transpose(a.reshape(nb, NSA_GROUPS, HEAD_DIM, wbuf), (0, 3, 1, 2))[None]
    o_diff = _diff_sample(pt_flat, lam4, dq, dk, dv, subln_g, pool_dk, pool_dv, nb, n_pages, lam_init)
    y, u = _ffn_step(x2d, o_nsa.reshape(nb, NSA_WIDTH), o_diff.reshape(nb, DIFF_WIDTH), ffn_ws,
                     conv_state[:, 0], conv_state[:, 1])
    kv = lambda a: a.reshape(1, nb, 1, NSA_GROUPS, HEAD_DIM)
    return (y.reshape(nb, 1, D_MODEL), kv(ck), kv(cv), kv(sk), kv(sv),
            dk.reshape(1, nb, 1, DIFF_HEADS, 2, HEAD_DIM), dv.reshape(1, nb, 1, DIFF_HEADS, DIFF_VDIM),
            from_t(swk), from_t(swv),
            jnp.stack([conv_state[:, 1], u], axis=1)[None])


def kernel(x_prompt, x_sample, cache_cmp_k, cache_cmp_v, cache_sel_k, cache_sel_v, cache_diff_k, cache_diff_v,
           cache_win_k, cache_win_v, state_ffn_conv, page_table, attn_norm, w_in, cmp_pos_k, cmp_w1_k, cmp_w2_k,
           cmp_pos_v, cmp_w1_v, cmp_w2_v, lambda_q1, lambda_k1, lambda_q2, lambda_k2, subln_g, w_out, ffn_norm,
           w_up, conv_w, conv_b, w_down, final_norm):
    assert w_in.shape[0] == 1 and x_sample.shape[1] == 1, "one layer, one new token per sequence"
    lam_init = 0.8 - 0.6 * math.exp(0.0)
    w_in_r = _prep_w_in(w_in[0])
    cw_k = _compress_weights(cmp_pos_k[0], cmp_w1_k[0], cmp_w2_k[0])
    cw_v = _compress_weights(cmp_pos_v[0], cmp_w1_v[0], cmp_w2_v[0])
    lam4 = jnp.stack([lambda_q1[0], lambda_k1[0], lambda_q2[0], lambda_k2[0]])
    ffn_ws = _prep_ffn(dict(w_out=w_out, ffn_norm=ffn_norm, w_up=w_up, conv_w=conv_w, conv_b=conv_b,
                            w_down=w_down, final_norm=final_norm))
    p = _prompt_group(x_prompt, attn_norm, w_in_r, cw_k, cw_v, lam4, subln_g, ffn_ws, lam_init)
    caches = (cache_cmp_k[0], cache_cmp_v[0], cache_sel_k[0], cache_sel_v[0], cache_diff_k[0], cache_diff_v[0],
              cache_win_k[0], cache_win_v[0], state_ffn_conv[0])
    s = _sample_group(x_sample, caches, page_table, attn_norm, w_in_r, cw_k, cw_v, lam4, subln_g, ffn_ws,
                      lam_init)
    return (p[0], s[0]) + tuple(p[1:]) + tuple(s[1:])
```

```python
import functools
import math

import numpy as np
import jax
import jax.numpy as jnp
from jax import lax
from jax.experimental import pallas as pl
from jax.experimental.pallas import tpu as pltpu

F32 = jnp.float32
BF16 = jnp.bfloat16

D_MODEL = 1024
HEAD_DIM = 64
NSA_WIDTH = 512
NSA_HEADS = 8
NSA_GROUPS = 2
NSA_REP = 4
KV_W = NSA_GROUPS * HEAD_DIM
CMP_BLOCK = 32
CMP_STRIDE = 16
CMP_HIDDEN = 128
SEL_BLOCK = 64
SEL_TOPN = 16
WINDOW = 512
FORCE_BONUS = 1.0e4
DIFF_WIDTH = 512
DIFF_VDIM = 128
DIFF_HEADS = 4
D_FF = 2816
CONV_W = 3
ROPE_THETA = 10000.0
RMS_EPS = 1e-6
ATTN_SCALE = 1.0 / math.sqrt(HEAD_DIM)
SCALE_LOG2E = ATTN_SCALE * math.log2(math.e)
PAGE_SIZE = 128

LANES = 128
VMEM_LIMIT = 48 * 1024 * 1024
NEG_INF = float("-inf")

C_Q = 0
C_CK, C_CV, C_SK, C_SV, C_WK, C_WV = 512, 640, 768, 896, 1024, 1152
C_DQ, C_DK, C_DV = 1280, 1792, 2304
C_GATE = 2816
W_IN_COLS = 2944


def _cparams(sem):
    return pltpu.CompilerParams(dimension_semantics=sem, vmem_limit_bytes=VMEM_LIMIT)


def _rope128(v, cos, sin_signed):
    lane = lax.broadcasted_iota(jnp.int32, v.shape, 1)
    first = (lane % HEAD_DIM) < (HEAD_DIM // 2)
    partner = jnp.where(first, pltpu.roll(v, LANES - HEAD_DIM // 2, 1), pltpu.roll(v, HEAD_DIM // 2, 1))
    return v * cos + partner * sin_signed


def _proj_kernel(x_ref, g_ref, w_ref, cos_ref, sin_ref,
                 qraw_ref, qrot_ref, ck_ref, cv_ref, sk_ref, sv_ref, wk_ref, wv_ref,
                 gate_ref, dq_ref, dk_ref, dv_ref, *t_refs):
    x = x_ref[...]
    h = x * lax.rsqrt(jnp.mean(x * x, axis=-1, keepdims=True) + RMS_EPS) * g_ref[...]
    z = jnp.dot(h.astype(BF16), w_ref[...], preferred_element_type=F32)
    cos = cos_ref[...]
    sin = sin_ref[...]

    def put(ref, col, width, rope, t_ref=None):
        for c in range(width // LANES):
            v = z[:, col + c * LANES: col + (c + 1) * LANES]
            v = _rope128(v, cos, sin) if rope else v
            ref[:, c * LANES:(c + 1) * LANES] = v
            if t_ref is not None:
                t_ref[0, c * LANES:(c + 1) * LANES, :] = v.T

    tr = list(t_refs) if t_refs else [None] * 7
    put(qraw_ref, C_Q, NSA_WIDTH, False)
    put(qrot_ref, C_Q, NSA_WIDTH, True)
    put(ck_ref, C_CK, KV_W, False, tr[0])
    put(cv_ref, C_CV, KV_W, False, tr[1])
    put(sk_ref, C_SK, KV_W, True, tr[2])
    put(sv_ref, C_SV, KV_W, False, tr[3])
    put(wk_ref, C_WK, KV_W, True, tr[4])
    put(wv_ref, C_WV, KV_W, False, tr[5])
    put(dq_ref, C_DQ, DIFF_WIDTH, True)
    put(dk_ref, C_DK, DIFF_WIDTH, True, tr[6])
    put(dv_ref, C_DV, DIFF_WIDTH, False)
    gate_ref[...] = jax.nn.sigmoid(z[:, C_GATE:C_GATE + LANES])


def _project(x2d, norm_g, w_in_r, cos_t, sin_t, tm, transposed_batch=0):
    M = x2d.shape[0]
    Tt = cos_t.shape[0]
    nt = Tt // tm
    widths = (NSA_WIDTH, NSA_WIDTH, KV_W, KV_W, KV_W, KV_W, KV_W, KV_W, LANES,
              DIFF_WIDTH, DIFF_WIDTH, DIFF_WIDTH)
    row = lambda i: (i, 0)
    out_specs = [pl.BlockSpec((tm, w), row) for w in widths]
    out_shape = [jax.ShapeDtypeStruct((M, w), F32) for w in widths]
    if transposed_batch:
        for w in (KV_W,) * 6 + (DIFF_WIDTH,):
            out_specs.append(pl.BlockSpec((1, w, tm), lambda i: (i // nt, 0, i % nt)))
            out_shape.append(jax.ShapeDtypeStruct((transposed_batch, w, Tt), F32))
    return pl.pallas_call(
        _proj_kernel,
        grid=(M // tm,),
        in_specs=[pl.BlockSpec((tm, D_MODEL), row),
                  pl.BlockSpec((1, D_MODEL), lambda i: (0, 0)),
                  pl.BlockSpec((D_MODEL, W_IN_COLS), lambda i: (0, 0)),
                  pl.BlockSpec((tm, LANES), lambda i: (i % nt, 0)),
                  pl.BlockSpec((tm, LANES), lambda i: (i % nt, 0))],
        out_specs=out_specs,
        out_shape=out_shape,
        compiler_params=_cparams(("arbitrary",)),
        name="in_proj",
    )(x2d, norm_g, w_in_r, cos_t, sin_t)


def _gelu(x):
    return 0.5 * x * (1.0 + jnp.tanh(math.sqrt(2.0 / math.pi) * (x + 0.044715 * (x * x * x))))


def _compress_rows(c, w1cat, posb, w1n, w2bd):
    r = jnp.dot(c.astype(BF16), w1cat, preferred_element_type=F32)
    return _compress_finish(r, posb, w1n, w2bd)


def _compress_finish(r, posb, w1n, w2bd):
    R = r.shape[0]
    bias = jnp.dot(posb, w1n, preferred_element_type=F32)[0:1]
    bias2 = jnp.concatenate([bias, bias], axis=1)
    hid = r[:, :2 * CMP_HIDDEN] + pltpu.roll(r[:, 2 * CMP_HIDDEN:], R - 1, 0) + bias2
    return jnp.dot(_gelu(hid).astype(BF16), w2bd, preferred_element_type=F32)


def _compress_kernel(c_ref, w1cat_ref, posb_ref, w1n_ref, w2bd_ref, o_ref):
    o_ref[0] = _compress_rows(c_ref[0], w1cat_ref[...], posb_ref[...], w1n_ref[...], w2bd_ref[...])


def _compress_weights(pos, w1, w2):
    w1r = w1.reshape(2, CMP_STRIDE, HEAD_DIM, CMP_HIDDEN)
    eye = jnp.eye(NSA_GROUPS, dtype=w1.dtype)
    big = jnp.einsum('isdh,gk->isgdkh', w1r, eye).reshape(2, CMP_STRIDE * KV_W, NSA_GROUPS * CMP_HIDDEN)
    w1cat = jnp.concatenate([big[0], big[1]], axis=1).astype(BF16)
    w2bd = jnp.einsum('hd,gk->ghkd', w2, eye).reshape(NSA_GROUPS * CMP_HIDDEN, KV_W).astype(BF16)
    posb = jnp.broadcast_to(pos.reshape(1, CMP_BLOCK * HEAD_DIM), (8, CMP_BLOCK * HEAD_DIM)).astype(BF16)
    return w1cat, posb, w1.astype(BF16), w2bd


def _compress_prompt(x, cw):
    B, T, _ = x.shape
    R = T // CMP_STRIDE
    c = x.reshape(B, R, CMP_STRIDE * KV_W)
    full = lambda a: pl.BlockSpec(a.shape, lambda b: (0,) * a.ndim)
    return pl.pallas_call(
        _compress_kernel,
        grid=(B,),
        in_specs=[pl.BlockSpec((1, R, CMP_STRIDE * KV_W), lambda b: (b, 0, 0))] + [full(a) for a in cw],
        out_specs=pl.BlockSpec((1, R, KV_W), lambda b: (b, 0, 0)),
        out_shape=jax.ShapeDtypeStruct((B, R, KV_W), F32),
        compiler_params=_cparams(("arbitrary",)),
        name="compress_prompt",
    )(c, *cw)


def _lane_group_masks(shape):
    lane = lax.broadcasted_iota(jnp.int32, shape, 1)
    lo = lane < HEAD_DIM
    return lo, jnp.logical_not(lo)


def _block_diag_rows(x, dtype):
    lo, hi = _lane_group_masks(x.shape)
    return jnp.concatenate([jnp.where(lo, x, 0.0), jnp.where(hi, x, 0.0)], axis=0).astype(dtype)


def _dot_nt(a, b):
    return lax.dot_general(a, b, (((1,), (1,)), ((), ())), preferred_element_type=F32)


SUM_ROWS = 16


def _values_t(v_bd):
    n = v_bd.shape[0] // NSA_GROUPS
    row = lax.broadcasted_iota(jnp.int32, (SUM_ROWS, v_bd.shape[0]), 0)
    col = lax.broadcasted_iota(jnp.int32, (SUM_ROWS, v_bd.shape[0]), 1)
    ones = jnp.where(row == col // n, 1.0, 0.0)
    return jnp.concatenate([v_bd.T, ones], axis=0).astype(BF16)


def _flash_step_t(s_t, mask_t, m_ref, acc_ref, v_t, tk):
    ps, alphas = [], []
    for g in range(NSA_GROUPS):
        sg = s_t[g * tk:(g + 1) * tk]
        if mask_t is not None:
            sg = jnp.where(mask_t, sg, NEG_INF)
        m_old = m_ref[g]
        m_new = jnp.maximum(m_old, jnp.max(sg, axis=0, keepdims=True))
        m_safe = jnp.where(m_new == NEG_INF, 0.0, m_new)
        alphas.append(jnp.exp2(m_old - m_safe))
        ps.append(jnp.exp2(sg - m_safe).astype(BF16))
        m_ref[g] = m_new
        rs = slice(g * HEAD_DIM, (g + 1) * HEAD_DIM)
        acc_ref[rs, :] = alphas[g] * acc_ref[rs, :]
    srow = lax.broadcasted_iota(jnp.int32, (SUM_ROWS, 1), 0)
    acc_ref[KV_W:, :] = jnp.where(srow == 0, alphas[0], alphas[1]) * acc_ref[KV_W:, :]
    acc_ref[...] = acc_ref[...] + jnp.dot(v_t, jnp.concatenate(ps, axis=0), preferred_element_type=F32)


def _finish_t(acc_ref):
    parts = [acc_ref[g * HEAD_DIM:(g + 1) * HEAD_DIM, :]
             * (1.0 / jnp.maximum(acc_ref[KV_W + g:KV_W + g + 1, :], 1e-30)) for g in range(NSA_GROUPS)]
    return jnp.concatenate(parts, axis=0)


def _reset(m_ref, acc_ref):
    m_ref[...] = jnp.full(m_ref.shape, NEG_INF, F32)
    acc_ref[...] = jnp.zeros(acc_ref.shape, F32)


def _queries_t(q, tq):
    return jnp.concatenate([(q[:, r * LANES:(r + 1) * LANES] * SCALE_LOG2E).T for r in range(NSA_REP)],
                           axis=1).astype(BF16)


def _outputs_from_t(o_t, tq):
    return jnp.concatenate([o_t[:, r * tq:(r + 1) * tq].T for r in range(NSA_REP)], axis=1)


def _split2(x):
    hi = x.astype(BF16)
    return hi, (x - hi.astype(F32)).astype(BF16)


def _select_blocks(score_t, q_pos_row, n_sel):
    n_pad = score_t.shape[0] // NSA_GROUPS
    tq = score_t.shape[1]
    out = []
    blk = lax.broadcasted_iota(jnp.int32, (n_pad, tq), 0)
    cur = q_pos_row // SEL_BLOCK
    forced = (blk == 0) | (blk == cur) | (blk == cur - 1)
    ok = (blk * SEL_BLOCK <= q_pos_row) & (blk < n_sel)
    for g in range(NSA_GROUPS):
        sc = score_t[g * n_pad:(g + 1) * n_pad]
        sc = jnp.where(ok, sc + jnp.where(forced, FORCE_BONUS, 0.0), NEG_INF)
        n_tiles = n_pad // 8
        tiles = [sc[8 * t:8 * t + 8] for t in range(n_tiles)]
        cnts = [jnp.zeros((8, tq), F32) for _ in range(n_tiles)]
        sub = lax.broadcasted_iota(jnp.int32, (8, tq), 0)
        for i in range(n_pad):
            row = sc[i:i + 1]
            ti, ri = divmod(i, 8)
            for t in range(n_tiles):
                if t > ti:
                    ahead = jnp.where(row >= tiles[t], 1.0, 0.0)
                elif t < ti:
                    ahead = jnp.where(row > tiles[t], 1.0, 0.0)
                else:
                    ahead = jnp.where(sub > ri, jnp.where(row >= tiles[t], 1.0, 0.0),
                                      jnp.where(row > tiles[t], 1.0, 0.0))
                cnts[t] = cnts[t] + ahead
        cnt = jnp.concatenate(cnts, axis=0)
        out.append(jnp.where((cnt < float(SEL_TOPN)) & ok, 1.0, 0.0))
    return jnp.concatenate(out, axis=0)


def _nsa_prompt_kernel(qraw_ref, qrot_ref, gate_ref, kc_ref, vc_ref, sk_ref, sv_ref, wk_ref, wv_ref,
                       ovt_ref, gexp_ref, o_ref,
                       skbd, svt, wkbd, wvt, kcbd, vct, sel_ref, m_ref, acc_ref, *, tq, tk, seq):
    i = pl.program_id(1)
    n_kt = seq // tk
    n_sel = seq // SEL_BLOCK
    n_cmp = kc_ref.shape[1]
    n_pad = sel_ref.shape[0] // NSA_GROUPS
    L = NSA_REP * tq
    blk_per_tile = tk // SEL_BLOCK

    @pl.when(i == 0)
    def _():
        def fill(j, _):
            rows = pl.ds(pl.multiple_of(j * tk, tk), tk)
            skbd[j] = _block_diag_rows(sk_ref[0, rows, :], BF16)
            wkbd[j] = _block_diag_rows(wk_ref[0, rows, :], BF16)
            svt[j] = _values_t(_block_diag_rows(sv_ref[0, rows, :], F32))
            wvt[j] = _values_t(_block_diag_rows(wv_ref[0, rows, :], F32))
            return 0
        lax.fori_loop(0, n_kt, fill, 0)
        kcbd[...] = _block_diag_rows(kc_ref[0], BF16)
        vct[...] = _block_diag_rows(vc_ref[0], F32).T.astype(BF16)

    q0 = i * tq
    q_pos_row = q0 + lax.broadcasted_iota(jnp.int32, (1, tq), 1)
    q_pos = jnp.concatenate([q_pos_row] * NSA_REP, axis=1)

    s = jnp.dot(kcbd[...], _queries_t(qraw_ref[...], tq), preferred_element_type=F32)
    kidx = lax.broadcasted_iota(jnp.int32, (n_cmp, 1), 0)
    cmask = (kidx * CMP_STRIDE + (CMP_BLOCK - 1) <= q_pos) & (kidx < n_cmp - 1)
    pcs = []
    for g in range(NSA_GROUPS):
        sg = jnp.where(cmask, s[g * n_cmp:(g + 1) * n_cmp], NEG_INF)
        m = jnp.max(sg, axis=0, keepdims=True)
        m = jnp.where(m == NEG_INF, 0.0, m)
        p = jnp.exp2(sg - m)
        pcs.append(p * (1.0 / jnp.maximum(jnp.sum(p, axis=0, keepdims=True), 1e-30)))
    pc = jnp.concatenate(pcs, axis=0)
    o_c = jnp.dot(vct[...], pc.astype(BF16), preferred_element_type=F32)
    imp = pc[:, 0:tq] + pc[:, tq:2 * tq] + pc[:, 2 * tq:3 * tq] + pc[:, 3 * tq:4 * tq]
    ih, il = _split2(imp)
    score = (jnp.dot(ovt_ref[...], ih, preferred_element_type=F32)
             + jnp.dot(ovt_ref[...], il, preferred_element_type=F32))
    sel = _select_blocks(score, q_pos_row, n_sel)
    for jt in range(n_pad // blk_per_tile):
        for g in range(NSA_GROUPS):
            src = g * n_pad + jt * blk_per_tile
            dst = (jt * NSA_GROUPS + g) * blk_per_tile
            sel_ref[dst:dst + blk_per_tile, :] = sel[src:src + blk_per_tile]

    qr = _queries_t(qrot_ref[...], tq)
    kiota = lax.broadcasted_iota(jnp.int32, (tk, 1), 0)

    n_bias = NSA_GROUPS * blk_per_tile

    def sel_step(causal):
        def step(j, _):
            s = jnp.dot(skbd[j], qr, preferred_element_type=F32)
            sel8 = sel_ref[pl.ds(pl.multiple_of(j * n_bias, n_bias), n_bias), :]
            parts = [jnp.broadcast_to(sel8[r:r + 1], (SEL_BLOCK, tq)) for r in range(n_bias)]
            m1 = jnp.concatenate(parts, axis=0) > 0.5
            s = jnp.where(jnp.concatenate([m1] * NSA_REP, axis=1), s, NEG_INF)
            mask = ((j * tk + kiota) <= q_pos) if causal else None
            _flash_step_t(s, mask, m_ref, acc_ref, svt[j], tk)
            return 0
        return step

    _reset(m_ref, acc_ref)
    n_full = q0 // tk
    lax.fori_loop(0, n_full, sel_step(False), 0)
    lax.fori_loop(n_full, (q0 + tq - 1) // tk + 1, sel_step(True), 0)
    o_s = _finish_t(acc_ref)

    _reset(m_ref, acc_ref)

    def win_step(j, _):
        s = jnp.dot(wkbd[j], qr, preferred_element_type=F32)
        kpos = j * tk + kiota
        _flash_step_t(s, (kpos <= q_pos) & (kpos > q_pos - WINDOW), m_ref, acc_ref, wvt[j], tk)
        return 0

    lax.fori_loop(jnp.maximum(q0 - WINDOW + 1, 0) // tk, (q0 + tq - 1) // tk + 1, win_step, 0)
    o_w = _finish_t(acc_ref)

    gh, gl = _split2(gate_ref[...])
    gx = (jnp.dot(gh, gexp_ref[...], preferred_element_type=F32)
          + jnp.dot(gl, gexp_ref[...], preferred_element_type=F32))
    o = (gx[:, 0:NSA_WIDTH] * _outputs_from_t(o_c, tq)
         + gx[:, NSA_WIDTH:2 * NSA_WIDTH] * _outputs_from_t(o_s, tq)
         + gx[:, 2 * NSA_WIDTH:3 * NSA_WIDTH] * _outputs_from_t(o_w, tq))
    o_ref[...] = o


def _overlap_matrix(n_cmp_rows, n_sel_pad):
    cs = np.arange(n_cmp_rows)[:, None] * CMP_STRIDE
    ss = np.arange(n_sel_pad)[None, :] * SEL_BLOCK
    ov = ((cs < ss + SEL_BLOCK) & (cs + CMP_BLOCK > ss)).astype(np.float32)
    z = np.zeros_like(ov)
    return jnp.asarray(np.block([[ov, z], [z, ov]]), BF16)


def _gate_expand_matrix():
    e = np.zeros((LANES, 3 * NSA_WIDTH), np.float32)
    for c in range(3):
        for r in range(NSA_REP):
            for g in range(NSA_GROUPS):
                col = c * NSA_WIDTH + r * LANES + g * HEAD_DIM
                e[c * NSA_HEADS + r * NSA_GROUPS + g, col:col + HEAD_DIM] = 1.0
    return jnp.asarray(e, BF16)


def _nsa_prompt(qraw, qrot, gate, kc, vc, sk, sv, wk, wv, B, T, tq, tk):
    nq = T // tq
    n_cmp = kc.shape[1]
    n_sel_pad = max(T // SEL_BLOCK, HEAD_DIM)
    ovt = _overlap_matrix(n_cmp, n_sel_pad).T
    gexp = _gate_expand_matrix()
    qspec = lambda w: pl.BlockSpec((tq, w), lambda b, i: (b * nq + i, 0))
    kvspec = lambda n: pl.BlockSpec((1, n, KV_W), lambda b, i: (b, 0, 0))
    const = lambda a: pl.BlockSpec(a.shape, lambda b, i: (0, 0))
    L = NSA_REP * tq
    n_kt = T // tk
    kern = functools.partial(_nsa_prompt_kernel, tq=tq, tk=tk, seq=T)
    kv3 = lambda a: a.reshape(B, T, KV_W)
    return pl.pallas_call(
        kern,
        grid=(B, nq),
        in_specs=[qspec(NSA_WIDTH), qspec(NSA_WIDTH), qspec(LANES), kvspec(n_cmp), kvspec(n_cmp),
                  kvspec(T), kvspec(T), kvspec(T), kvspec(T), const(ovt), const(gexp)],
        out_specs=qspec(NSA_WIDTH),
        out_shape=jax.ShapeDtypeStruct((B * T, NSA_WIDTH), F32),
        scratch_shapes=[pltpu.VMEM((n_kt, 2 * tk, KV_W), BF16),
                        pltpu.VMEM((n_kt, KV_W + SUM_ROWS, 2 * tk), BF16),
                        pltpu.VMEM((n_kt, 2 * tk, KV_W), BF16),
                        pltpu.VMEM((n_kt, KV_W + SUM_ROWS, 2 * tk), BF16),
                        pltpu.VMEM((2 * n_cmp, KV_W), BF16), pltpu.VMEM((KV_W, 2 * n_cmp), BF16),
                        pltpu.VMEM((NSA_GROUPS * n_sel_pad, tq), F32),
                        pltpu.VMEM((NSA_GROUPS, 1, L), F32),
                        pltpu.VMEM((KV_W + SUM_ROWS, L), F32)],
        compiler_params=_cparams(("arbitrary", "arbitrary")),
        name="nsa_prompt",
    )(qraw, qrot, gate, kc, vc, kv3(sk), kv3(sv), kv3(wk), kv3(wv), ovt, gexp)


def _lambda_value(lam_ref, lam_init):
    a = jnp.sum(lam_ref[0:1] * lam_ref[1:2], axis=-1, keepdims=True)
    b = jnp.sum(lam_ref[2:3] * lam_ref[3:4], axis=-1, keepdims=True)
    return jnp.exp(a) - jnp.exp(b) + lam_init


def _diff_merge_rows(o0, o1, lam, g, lam_init):
    od = o0 - lam * o1
    od = od * lax.rsqrt(jnp.mean(od * od, axis=-1, keepdims=True) + RMS_EPS) * g
    return od * (1.0 - lam_init)


def _diff_prompt_kernel(lam_ref, dq_ref, dk_ref, dv_ref, gcol_ref, o_ref,
                        kbf, vbf, m_ref, acc_ref, *, tq, tk, lam_init):
    i = pl.program_id(2)
    n_kt = kbf.shape[0] // tk

    @pl.when(i == 0)
    def _():
        kbf[...] = dk_ref[...].astype(BF16)
        ones = jnp.where(lax.broadcasted_iota(jnp.int32, (SUM_ROWS, tk), 0) == 0, 1.0, 0.0)

        def fill(j, _):
            vt = dv_ref[pl.ds(pl.multiple_of(j * tk, tk), tk), :].T
            vbf[j] = jnp.concatenate([vt, ones], axis=0).astype(BF16)
            return 0
        lax.fori_loop(0, n_kt, fill, 0)

    q = dq_ref[...] * SCALE_LOG2E
    lo, hi = _lane_group_masks(q.shape)
    q2t = jnp.concatenate([jnp.where(lo, q, 0.0), jnp.where(hi, q, 0.0)], axis=0).T.astype(BF16)
    q0 = i * tq
    q_pos = q0 + lax.broadcasted_iota(jnp.int32, (1, 2 * tq), 1) % tq
    kiota = lax.broadcasted_iota(jnp.int32, (tk, 1), 0)
    m_ref[...] = jnp.full(m_ref.shape, NEG_INF, F32)
    acc_ref[...] = jnp.zeros(acc_ref.shape, F32)

    def scores(j):
        k0 = pl.multiple_of(j * tk, tk)
        return jnp.dot(kbf[pl.ds(k0, tk), :], q2t, preferred_element_type=F32)

    def consume(j, s, causal):
        if causal:
            s = jnp.where((j * tk + kiota) <= q_pos, s, NEG_INF)
        m_old = m_ref[...]
        m_new = jnp.maximum(m_old, jnp.max(s, axis=0, keepdims=True))
        alpha = jnp.exp2(m_old - m_new)
        p = jnp.exp2(s - m_new).astype(BF16)
        m_ref[...] = m_new
        acc_ref[...] = alpha * acc_ref[...] + jnp.dot(vbf[j], p, preferred_element_type=F32)

    def make_step(causal):
        def step(j, _):
            consume(j, scores(j), causal)
            return 0
        return step

    n_full = q0 // tk
    lax.fori_loop(0, n_full, make_step(False), 0)
    lax.fori_loop(n_full, (q0 + tq - 1) // tk + 1, make_step(True), 0)
    o = acc_ref[0:DIFF_VDIM, :] * (1.0 / acc_ref[DIFF_VDIM:DIFF_VDIM + 1, :])
    lam = _lambda_value(lam_ref, lam_init)
    od = o[:, :tq] - lam * o[:, tq:]
    od = od * lax.rsqrt(jnp.mean(od * od, axis=0, keepdims=True) + RMS_EPS) * gcol_ref[...]
    o_ref[...] = (od * (1.0 - lam_init)).T


def _diff_prompt(lam4, dq, dk, dv, subln_g, B, T, tq, tk, lam_init):
    nq = T // tq
    kern = functools.partial(_diff_prompt_kernel, tq=tq, tk=tk, lam_init=lam_init)
    return pl.pallas_call(
        kern,
        grid=(B, DIFF_HEADS, nq),
        in_specs=[pl.BlockSpec((4, HEAD_DIM), lambda b, h, i: (0, 0)),
                  pl.BlockSpec((tq, DIFF_VDIM), lambda b, h, i: (b * nq + i, h)),
                  pl.BlockSpec((T, DIFF_VDIM), lambda b, h, i: (b, h)),
                  pl.BlockSpec((T, DIFF_VDIM), lambda b, h, i: (b, h)),
                  pl.BlockSpec((DIFF_VDIM, 1), lambda b, h, i: (0, 0))],
        out_specs=pl.BlockSpec((tq, DIFF_VDIM), lambda b, h, i: (b * nq + i, h)),
        out_shape=jax.ShapeDtypeStruct((B * T, DIFF_WIDTH), F32),
        scratch_shapes=[pltpu.VMEM((T, DIFF_VDIM), BF16), pltpu.VMEM((T // tk, DIFF_VDIM + SUM_ROWS, tk), BF16),
                        pltpu.VMEM((1, 2 * tq), F32), pltpu.VMEM((DIFF_VDIM + SUM_ROWS, 2 * tq), F32)],
        compiler_params=_cparams(("arbitrary", "arbitrary", "arbitrary")),
        name="diff_prompt",
    )(lam4, dq, dk, dv, subln_g.reshape(DIFF_VDIM, 1))


FF_CHUNK = 1408


def _rms(x, g):
    return x * lax.rsqrt(jnp.mean(x * x, axis=-1, keepdims=True) + RMS_EPS) * g


def _ffn_seq_kernel(x_ref, on_ref, od_ref, won_ref, wod_ref, g2_ref, wup_ref, cw_ref, cb_ref, wdn_ref,
                    gf_ref, y_ref, conv_ref, carry_ref, *, tm, tiles_per_seq):
    i = pl.program_id(0)
    first = (i % tiles_per_seq) == 0
    x1 = (x_ref[...] + jnp.dot(on_ref[...].astype(BF16), won_ref[...], preferred_element_type=F32)
          + jnp.dot(od_ref[...].astype(BF16), wod_ref[...], preferred_element_type=F32))
    h = _rms(x1, g2_ref[...]).astype(BF16)
    rowi = lax.broadcasted_iota(jnp.int32, (tm, 1), 0)
    f = jnp.zeros((tm, D_MODEL), F32)
    for k in range(D_FF // FF_CHUNK):
        halves = []
        for part in range(2):
            c0 = part * D_FF + k * FF_CHUNK
            cs = slice(c0, c0 + FF_CHUNK)
            u = jnp.dot(h, wup_ref[:, cs], preferred_element_type=F32)
            pm2 = jnp.where(first, 0.0, carry_ref[6:7, cs])
            pm1 = jnp.where(first, 0.0, carry_ref[7:8, cs])
            u1 = jnp.where(rowi == 0, pm1, pltpu.roll(u, 1, 0))
            u2 = jnp.where(rowi == 0, pm2, jnp.where(rowi == 1, pm1, pltpu.roll(u, 2, 0)))
            halves.append(cb_ref[:, cs] + u2 * cw_ref[0:1, cs] + u1 * cw_ref[1:2, cs] + u * cw_ref[2:3, cs])
            carry_ref[6:8, cs] = u[tm - 2:tm]
            conv_ref[0, :, cs] = u[tm - 2:tm]
        act = (jax.nn.silu(halves[0]) * halves[1]).astype(BF16)
        f = f + jnp.dot(act, wdn_ref[k * FF_CHUNK:(k + 1) * FF_CHUNK, :], preferred_element_type=F32)
    y_ref[...] = _rms(x1 + f, gf_ref[...])


def _ffn_step_kernel(x_ref, on_ref, od_ref, won_ref, wod_ref, g2_ref, wup_ref, cw_ref, cb_ref, wdn_ref,
                     gf_ref, p0_ref, p1_ref, y_ref, u_ref):
    x1 = (x_ref[...] + jnp.dot(on_ref[...].astype(BF16), won_ref[...], preferred_element_type=F32)
          + jnp.dot(od_ref[...].astype(BF16), wod_ref[...], preferred_element_type=F32))
    h = _rms(x1, g2_ref[...]).astype(BF16)
    f = jnp.zeros(x1.shape, F32)
    for k in range(D_FF // FF_CHUNK):
        halves = []
        for part in range(2):
            c0 = part * D_FF + k * FF_CHUNK
            cs = slice(c0, c0 + FF_CHUNK)
            u = jnp.dot(h, wup_ref[:, cs], preferred_element_type=F32)
            u_ref[:, cs] = u
            halves.append(cb_ref[:, cs] + p0_ref[:, cs] * cw_ref[0:1, cs] + p1_ref[:, cs] * cw_ref[1:2, cs]
                          + u * cw_ref[2:3, cs])
        act = (jax.nn.silu(halves[0]) * halves[1]).astype(BF16)
        f = f + jnp.dot(act, wdn_ref[k * FF_CHUNK:(k + 1) * FF_CHUNK, :], preferred_element_type=F32)
    y_ref[...] = _rms(x1 + f, gf_ref[...])


def _const_spec(a, ngrid):
    return pl.BlockSpec(a.shape, lambda *idx: (0,) * a.ndim, pipeline_mode=pl.Buffered(1))


def _ffn_weights_specs(ws):
    return [_const_spec(a, 1) for a in ws]


def _ffn_seq(x2d, o_nsa, o_diff, ws, B, T, tm):
    M = x2d.shape[0]
    tps = T // tm
    row = lambda w: pl.BlockSpec((tm, w), lambda i: (i, 0))
    kern = functools.partial(_ffn_seq_kernel, tm=tm, tiles_per_seq=tps)
    return pl.pallas_call(
        kern,
        grid=(M // tm,),
        in_specs=[row(D_MODEL), row(NSA_WIDTH), row(DIFF_WIDTH)] + _ffn_weights_specs(ws),
        out_specs=[row(D_MODEL), pl.BlockSpec((1, CONV_W - 1, 2 * D_FF), lambda i: (i // tps, 0, 0))],
        out_shape=[jax.ShapeDtypeStruct((M, D_MODEL), F32),
                   jax.ShapeDtypeStruct((B, CONV_W - 1, 2 * D_FF), F32)],
        scratch_shapes=[pltpu.VMEM((8, 2 * D_FF), F32)],
        compiler_params=_cparams(("arbitrary",)),
        name="ffn_prompt",
    )(x2d, o_nsa, o_diff, *ws)


def _ffn_step(x2d, o_nsa, o_diff, ws, p0, p1):
    M = x2d.shape[0]
    full = lambda a: pl.BlockSpec(a.shape, lambda i: (0,) * a.ndim)
    ins = (x2d, o_nsa, o_diff) + tuple(ws) + (p0, p1)
    return pl.pallas_call(
        _ffn_step_kernel,
        grid=(1,),
        in_specs=[full(a) for a in ins],
        out_specs=[pl.BlockSpec((M, D_MODEL), lambda i: (0, 0)), pl.BlockSpec((M, 2 * D_FF), lambda i: (0, 0))],
        out_shape=[jax.ShapeDtypeStruct((M, D_MODEL), F32), jax.ShapeDtypeStruct((M, 2 * D_FF), F32)],
        compiler_params=_cparams(("arbitrary",)),
        name="ffn_sample",
    )(*ins)


ROWS_PER_PAGE = PAGE_SIZE // CMP_STRIDE


def _compress_sample_kernel(pt_ref, pool_ref, tail_ref, w1cat_ref, posb_ref, w1n_ref, w2bd_ref, o_ref,
                            pbuf, xs, r_ref, sem, *, n_pages, nb, rows):
    b = pl.program_id(0)
    past = n_pages * PAGE_SIZE

    def page_copy(bb, n, slot):
        return pltpu.make_async_copy(pool_ref.at[pt_ref[bb * n_pages + n]], pbuf.at[slot, n], sem.at[slot])

    def start_all(bb, slot):
        def body(n, _):
            page_copy(bb, n, slot).start()
            return 0
        lax.fori_loop(0, n_pages, body, 0)

    @pl.when(b == 0)
    def _():
        start_all(0, 0)

    @pl.when(b + 1 < nb)
    def _():
        start_all(b + 1, (b + 1) % 2)

    slot = b % 2

    def wait_body(n, _):
        page_copy(b, n, slot).wait()
        return 0
    lax.fori_loop(0, n_pages, wait_body, 0)

    xs[past:past + 8, :] = tail_ref[0]
    xs[past + 8:, :] = jnp.zeros((xs.shape[0] - past - 8, KV_W), F32)

    def hidden_rows(r0, n):
        acc = None
        for s in range(0, CMP_STRIDE, 2):
            lhs = jnp.concatenate([xs[pl.ds(r0 * CMP_STRIDE + s, n, stride=CMP_STRIDE), :],
                                   xs[pl.ds(r0 * CMP_STRIDE + s + 1, n, stride=CMP_STRIDE), :]],
                                  axis=1).astype(BF16)
            part = jnp.dot(lhs, w1cat_ref[s * KV_W:(s + 2) * KV_W, :], preferred_element_type=F32)
            acc = part if acc is None else acc + part
        r_ref[r0:r0 + n, :] = acc

    n_chunks = 4
    pages_pc = n_pages // n_chunks
    rows_pc = pages_pc * ROWS_PER_PAGE
    for c in range(n_chunks):
        for n in range(c * pages_pc, (c + 1) * pages_pc):
            xs[n * PAGE_SIZE:(n + 1) * PAGE_SIZE, :] = pbuf[slot, n].T
        hidden_rows(c * rows_pc, rows_pc)
    hidden_rows(n_chunks * rows_pc, rows - n_chunks * rows_pc)
    o_ref[0, 0:rows, :] = _compress_finish(r_ref[...], posb_ref[...], w1n_ref[...], w2bd_ref[...])
    o_ref[0, rows:, :] = jnp.zeros((o_ref.shape[1] - rows, KV_W), F32)


def _compress_sample(pool, pt_flat, new_row, cw, nb, n_pages, rows, rows_out):
    pool_t = _pages_transposed(pool)
    tail = jnp.zeros((nb, 8, KV_W), F32).at[:, 0, :].set(new_row)
    kern = functools.partial(_compress_sample_kernel, n_pages=n_pages, nb=nb, rows=rows)
    full = lambda a: pl.BlockSpec(a.shape, lambda b, pt: (0,) * a.ndim)
    return pl.pallas_call(
        kern,
        grid_spec=pltpu.PrefetchScalarGridSpec(
            num_scalar_prefetch=1,
            grid=(nb,),
            in_specs=[pl.BlockSpec(memory_space=pl.ANY),
                      pl.BlockSpec((1, 8, KV_W), lambda b, pt: (b, 0, 0))] + [full(a) for a in cw],
            out_specs=pl.BlockSpec((1, rows_out, KV_W), lambda b, pt: (b, 0, 0)),
            scratch_shapes=[pltpu.VMEM((2, n_pages, KV_W, PAGE_SIZE), F32),
                            pltpu.VMEM((rows * CMP_STRIDE, KV_W), F32),
                            pltpu.VMEM((rows, 4 * CMP_HIDDEN), F32), pltpu.SemaphoreType.DMA((2,))]),
        out_shape=jax.ShapeDtypeStruct((nb, rows_out, KV_W), F32),
        compiler_params=_cparams(("arbitrary",)),
        name="compress_sample",
    )(pt_flat, pool_t, tail, *cw)


def _pages_transposed(pool):
    n_pool = pool.shape[0]
    return jnp.transpose(pool, (0, 2, 3, 1)).reshape(n_pool, KV_W, pool.shape[1])


def _pad_rows8(x):
    return jnp.concatenate([x, jnp.zeros((8 - x.shape[0], x.shape[1]), x.dtype)], axis=0)


def _nsa_sample_a_kernel(q_ref, kc_ref, vc_ref, ovbd_ref, oc_ref, info_ref, *, n_cmp, n_sel, q_pos):
    rk = kc_ref.shape[1]
    n_pad = ovbd_ref.shape[1] // NSA_GROUPS
    q8 = _pad_rows8(q_ref[0]).astype(BF16)
    kcbd = _block_diag_rows(kc_ref[0], BF16)
    vcbd = _block_diag_rows(vc_ref[0], BF16)
    s = _dot_nt(q8, kcbd)
    kidx = lax.broadcasted_iota(jnp.int32, s.shape, 1) % rk
    cmask = (kidx * CMP_STRIDE + (CMP_BLOCK - 1) <= q_pos) & (kidx < n_cmp)
    s = jnp.where(cmask, s * ATTN_SCALE, NEG_INF)
    pcs = []
    for g in range(NSA_GROUPS):
        sg = s[:, g * rk:(g + 1) * rk]
        m = jnp.max(sg, axis=-1, keepdims=True)
        m = jnp.where(m == NEG_INF, 0.0, m)
        p = jnp.exp(sg - m)
        pcs.append(p / jnp.maximum(jnp.sum(p, axis=-1, keepdims=True), 1e-30))
    pc = jnp.concatenate(pcs, axis=1)
    oc_ref[0] = jnp.dot(pc.astype(BF16), vcbd, preferred_element_type=F32)
    imp = jnp.sum(pc[0:NSA_REP], axis=0, keepdims=True)
    imp8 = jnp.concatenate([imp] * 8, axis=0)
    ih, il = _split2(imp8)
    score = (jnp.dot(ih, ovbd_ref[...], preferred_element_type=F32)
             + jnp.dot(il, ovbd_ref[...], preferred_element_type=F32))[0:1]

    blk_r = lax.broadcasted_iota(jnp.int32, (1, n_pad), 1)
    blk_c = lax.broadcasted_iota(jnp.int32, (n_pad, 1), 0)
    cur = q_pos // SEL_BLOCK
    forced_r = (blk_r == 0) | (blk_r == cur) | (blk_r == cur - 1)
    ok_r = (blk_r * SEL_BLOCK <= q_pos) & (blk_r < n_sel)
    ok_c = (blk_c * SEL_BLOCK <= q_pos) & (blk_c < n_sel)
    ii = lax.broadcasted_iota(jnp.int32, (n_pad, n_pad), 1)
    jj = lax.broadcasted_iota(jnp.int32, (n_pad, n_pad), 0)
    kk = lax.broadcasted_iota(jnp.int32, (n_pad, LANES), 1)
    jf = lax.broadcasted_iota(jnp.int32, (n_pad, LANES), 0).astype(F32)
    rows_out = []
    for g in range(NSA_GROUPS):
        sr = score[:, g * n_pad:(g + 1) * n_pad]
        sr = jnp.where(ok_r, sr + jnp.where(forced_r, FORCE_BONUS, 0.0), NEG_INF)
        sc = jnp.sum(jnp.where(ii == jj, sr, 0.0), axis=1, keepdims=True)
        ahead = jnp.where(ii < jj, jnp.where(sr >= sc, 1.0, 0.0), jnp.where(sr > sc, 1.0, 0.0))
        cnt = jnp.sum(ahead, axis=1, keepdims=True)
        chosen = (cnt < float(SEL_TOPN)) & ok_c
        hit = chosen & (cnt == kk.astype(F32))
        rows_out.append((jnp.sum(jnp.where(hit, jf, 0.0), axis=0, keepdims=True),
                         jnp.sum(jnp.where(hit, 1.0, 0.0), axis=0, keepdims=True)))
    info = jnp.concatenate([rows_out[0][0], rows_out[1][0], rows_out[0][1], rows_out[1][1],
                            jnp.zeros((4, LANES), F32)], axis=0)
    info_ref[0] = info.astype(jnp.int32)


def _nsa_sample_a(qraw3, kc, vc, q_pos, n_cmp, n_sel):
    nb, rk, _ = kc.shape
    n_pad = -(-n_sel // LANES) * LANES
    ovbd = _overlap_matrix(rk, n_pad)
    kern = functools.partial(_nsa_sample_a_kernel, n_cmp=n_cmp, n_sel=n_sel, q_pos=q_pos)
    return pl.pallas_call(
        kern,
        grid=(nb,),
        in_specs=[pl.BlockSpec((1, NSA_REP, LANES), lambda b: (b, 0, 0)),
                  pl.BlockSpec((1, rk, KV_W), lambda b: (b, 0, 0)),
                  pl.BlockSpec((1, rk, KV_W), lambda b: (b, 0, 0)),
                  pl.BlockSpec(ovbd.shape, lambda b: (0, 0))],
        out_specs=[pl.BlockSpec((1, 8, LANES), lambda b: (b, 0, 0)),
                   pl.BlockSpec((1, 8, LANES), lambda b: (b, 0, 0))],
        out_shape=[jax.ShapeDtypeStruct((nb, 8, LANES), F32), jax.ShapeDtypeStruct((nb, 8, LANES), jnp.int32)],
        compiler_params=_cparams(("arbitrary",)),
        name="nsa_sample_select",
    )(qraw3, kc, vc, ovbd)


N_SLOTS = NSA_GROUPS * SEL_TOPN


def _decode_attend(q8f, kts, vts, masks, s_new, inc_new, v_new):
    lo, hi = _lane_group_masks((8, KV_W))
    outs = []
    for g, lanes in enumerate((lo, hi)):
        qg = jnp.where(lanes, q8f, 0.0).astype(BF16)
        sg = jnp.dot(qg, kts[g], preferred_element_type=F32) * ATTN_SCALE
        sg = jnp.where(masks[g], sg, NEG_INF)
        sn = jnp.where(inc_new[g] > 0.5, s_new[g], NEG_INF)
        m = jnp.maximum(jnp.max(sg, axis=-1, keepdims=True), sn)
        m = jnp.where(m == NEG_INF, 0.0, m)
        p = jnp.exp(sg - m)
        pnew = jnp.exp(sn - m)
        l = jnp.sum(p, axis=-1, keepdims=True) + pnew
        og = _dot_nt(p.astype(BF16), vts[g]) + pnew * v_new
        outs.append(og / jnp.maximum(l, 1e-30))
    return jnp.where(lo, outs[0], outs[1])


def _nsa_sample_b_kernel(sel_ref, pt_ref, q_ref, gate_ref, oc_ref, info_ref, psk_ref, psv_ref,
                         sknew_ref, svnew_ref, wink_ref, winv_ref, wknew_ref, wvnew_ref, eexp_ref, e2_ref,
                         o_ref, swk_ref, swv_ref, kbuf, vbuf, sem, *, nb, n_pages, n_past_blk, wbuf):
    b = pl.program_id(0)
    blk_per_page = PAGE_SIZE // SEL_BLOCK

    def copies(bb, idx, slot):
        j = jnp.minimum(sel_ref[bb * N_SLOTS + idx], n_past_blk - 1)
        page = pt_ref[bb * n_pages + j // blk_per_page]
        return (pltpu.make_async_copy(psk_ref.at[page], kbuf.at[slot, idx], sem.at[slot]),
                pltpu.make_async_copy(psv_ref.at[page], vbuf.at[slot, idx], sem.at[slot]))

    def start_all(bb, slot):
        def body(idx, _):
            ck, cv = copies(bb, idx, slot)
            ck.start()
            cv.start()
            return 0
        lax.fori_loop(0, N_SLOTS, body, 0)

    @pl.when(b == 0)
    def _():
        start_all(0, 0)

    @pl.when(b + 1 < nb)
    def _():
        start_all(b + 1, (b + 1) % 2)

    slot = b % 2

    def wait_body(idx, _):
        ck, cv = copies(b, idx, slot)
        ck.wait()
        cv.wait()
        return 0
    lax.fori_loop(0, N_SLOTS, wait_body, 0)

    q8f = _pad_rows8(q_ref[0])
    lo, hi = _lane_group_masks((8, KV_W))

    def new_scores(k_new):
        prod = q8f * k_new
        return [jnp.sum(jnp.where(lo, prod, 0.0), axis=-1, keepdims=True) * ATTN_SCALE,
                jnp.sum(jnp.where(hi, prod, 0.0), axis=-1, keepdims=True) * ATTN_SCALE]

    info = info_ref[0].astype(F32)
    lane1 = lax.broadcasted_iota(jnp.int32, (1, LANES), 1)
    n_keys = SEL_TOPN * PAGE_SIZE
    tok_half = (lax.broadcasted_iota(jnp.int32, (8, n_keys), 1) % PAGE_SIZE) // SEL_BLOCK
    kts, vts, masks, inc = [], [], [], []
    for g in range(NSA_GROUPS):
        kts.append(jnp.concatenate([kbuf[slot, g * SEL_TOPN + k] for k in range(SEL_TOPN)], axis=1).astype(BF16))
        vts.append(jnp.concatenate([vbuf[slot, g * SEL_TOPN + k] for k in range(SEL_TOPN)], axis=1).astype(BF16))
        idx8 = jnp.concatenate([info[g:g + 1]] * 8, axis=0).astype(BF16)
        val8 = jnp.concatenate([info[2 + g:3 + g]] * 8, axis=0).astype(BF16)
        jl = jnp.dot(idx8, eexp_ref[...], preferred_element_type=F32)
        vl = jnp.dot(val8, eexp_ref[...], preferred_element_type=F32)
        masks.append((vl > 0.5) & (jl < n_past_blk - 0.5)
                     & (jl.astype(jnp.int32) % blk_per_page == tok_half))
        is_new = (info[g:g + 1] > n_past_blk - 0.5) & (info[2 + g:3 + g] > 0.5) & (lane1 < SEL_TOPN)
        inc.append(jnp.sum(jnp.where(is_new, 1.0, 0.0), axis=-1, keepdims=True))
    o_s = _decode_attend(q8f, kts, vts, masks, new_scores(sknew_ref[0]), inc, svnew_ref[0])

    wk_old = wink_ref[0]
    wv_old = winv_ref[0]
    widx = lax.broadcasted_iota(jnp.int32, (1, wbuf), 1)
    wmask = widx > wbuf - WINDOW
    always = [jnp.ones((1, 1), F32)] * NSA_GROUPS
    o_w = _decode_attend(q8f, [wk_old.astype(BF16)] * 2, [wv_old.astype(BF16)] * 2, [wmask] * 2,
                         new_scores(wknew_ref[0]), always, wvnew_ref[0])
    eye = (lax.broadcasted_iota(jnp.int32, (KV_W, KV_W), 0) == lax.broadcasted_iota(jnp.int32, (KV_W, KV_W), 1))
    for new_ref, old, out_ref in ((wknew_ref, wk_old, swk_ref), (wvnew_ref, wv_old, swv_ref)):
        col = jnp.sum(jnp.where(eye, new_ref[0], 0.0), axis=1, keepdims=True)
        out_ref[0] = jnp.where(widx == wbuf - 1, col, pltpu.roll(old, wbuf - 1, 1))

    gate8 = jnp.concatenate([gate_ref[0]] * 8, axis=0)
    rr = lax.broadcasted_iota(jnp.int32, (8, LANES), 0)
    ll = lax.broadcasted_iota(jnp.int32, (8, LANES), 1)
    o = jnp.zeros((8, KV_W), F32)
    for c, ob in enumerate((oc_ref[0], o_s, o_w)):
        base = c * NSA_HEADS + rr * NSA_GROUPS
        gsel = jnp.where((ll >= base) & (ll < base + NSA_GROUPS), gate8, 0.0)
        gh, gl = _split2(gsel)
        gx = (jnp.dot(gh, e2_ref[...], preferred_element_type=F32)
              + jnp.dot(gl, e2_ref[...], preferred_element_type=F32))
        o = o + gx * ob
    o_ref[0] = o[0:NSA_REP]


def _nsa_sample_b(sel_flat, pt_flat, qrot3, gate3, oc, info, pool_sk, pool_sv, sknew, svnew,
                  wink, winv, wknew, wvnew, nb, n_pages):
    wbuf = wink.shape[2]
    n_past_blk = n_pages * (PAGE_SIZE // SEL_BLOCK)
    e = np.zeros((LANES, SEL_TOPN * PAGE_SIZE), np.float32)
    for k in range(SEL_TOPN):
        e[k, k * PAGE_SIZE:(k + 1) * PAGE_SIZE] = 1.0
    eexp = jnp.asarray(e, BF16)
    e2 = np.zeros((LANES, LANES), np.float32)
    for j in range(LANES):
        e2[j, (j % NSA_GROUPS) * HEAD_DIM:(j % NSA_GROUPS + 1) * HEAD_DIM] = 1.0
    e2 = jnp.asarray(e2, BF16)
    per_b = lambda r, w: pl.BlockSpec((1, r, w), lambda b, s, p: (b, 0, 0))
    const = lambda a: pl.BlockSpec(a.shape, lambda b, s, p: (0, 0))
    anyspec = pl.BlockSpec(memory_space=pl.ANY)
    kern = functools.partial(_nsa_sample_b_kernel, nb=nb, n_pages=n_pages, n_past_blk=n_past_blk, wbuf=wbuf)
    r3 = lambda a: a.reshape(nb, 1, KV_W)
    return pl.pallas_call(
        kern,
        grid_spec=pltpu.PrefetchScalarGridSpec(
            num_scalar_prefetch=2,
            grid=(nb,),
            in_specs=[per_b(NSA_REP, LANES), per_b(1, LANES), per_b(8, LANES), per_b(8, LANES),
                      anyspec, anyspec, per_b(1, KV_W), per_b(1, KV_W),
                      per_b(KV_W, wbuf), per_b(KV_W, wbuf), per_b(1, KV_W), per_b(1, KV_W),
                      const(eexp), const(e2)],
            out_specs=[per_b(NSA_REP, LANES), per_b(KV_W, wbuf), per_b(KV_W, wbuf)],
            scratch_shapes=[pltpu.VMEM((2, N_SLOTS, KV_W, PAGE_SIZE), F32),
                            pltpu.VMEM((2, N_SLOTS, KV_W, PAGE_SIZE), F32),
                            pltpu.SemaphoreType.DMA((2,))]),
        out_shape=[jax.ShapeDtypeStruct((nb, NSA_REP, LANES), F32),
                   jax.ShapeDtypeStruct((nb, KV_W, wbuf), F32),
                   jax.ShapeDtypeStruct((nb, KV_W, wbuf), F32)],
        compiler_params=_cparams(("arbitrary",)),
        name="nsa_sample_attend",
    )(sel_flat, pt_flat, qrot3, gate3, oc, info, _pages_transposed(pool_sk), _pages_transposed(pool_sv),
      r3(sknew), r3(svnew), wink, winv, r3(wknew), r3(wvnew), eexp, e2)


DIFF_PAGES_PER_STEP = 8


def _diff_sample_kernel(pt_ref, lam_ref, dq_ref, dknew_ref, dvnew_ref, g_ref, pk_ref, pv_ref, o_ref,
                        kbuf, vbuf, sem, m_ref, l_ref, acc_ref, *, nb, n_pages, lam_init):
    b = pl.program_id(0)
    c = pl.program_id(1)
    nc = n_pages // DIFF_PAGES_PER_STEP
    step = b * nc + c

    def copies(st, p, slot):
        bb = st // nc
        cc = st % nc
        page = pt_ref[bb * n_pages + cc * DIFF_PAGES_PER_STEP + p]
        dst = pl.ds(p * PAGE_SIZE * DIFF_HEADS, PAGE_SIZE * DIFF_HEADS)
        return (pltpu.make_async_copy(pk_ref.at[page], kbuf.at[slot, p], sem.at[slot]),
                pltpu.make_async_copy(pv_ref.at[page], vbuf.at[slot, dst, :], sem.at[slot]))

    def start_all(st, slot):
        for p in range(DIFF_PAGES_PER_STEP):
            ck, cv = copies(st, p, slot)
            ck.start()
            cv.start()

    @pl.when(step == 0)
    def _():
        start_all(0, 0)

    @pl.when(step + 1 < nb * nc)
    def _():
        start_all(step + 1, (step + 1) % 2)

    slot = step % 2
    for p in range(DIFF_PAGES_PER_STEP):
        ck, cv = copies(step, p, slot)
        ck.wait()
        cv.wait()

    @pl.when(c == 0)
    def _():
        m_ref[...] = jnp.full(m_ref.shape, NEG_INF, F32)
        l_ref[...] = jnp.zeros(l_ref.shape, F32)
        acc_ref[...] = jnp.zeros(acc_ref.shape, F32)

    rr = lax.broadcasted_iota(jnp.int32, (8, DIFF_WIDTH), 0)
    ll = lax.broadcasted_iota(jnp.int32, (8, DIFF_WIDTH), 1)
    own = (ll // HEAD_DIM) == rr
    q8f = jnp.where(own, jnp.concatenate([dq_ref[0]] * 8, axis=0), 0.0)
    q8 = q8f.astype(BF16)
    s = jnp.concatenate([jnp.dot(q8, kbuf[slot, p].astype(BF16), preferred_element_type=F32)
                         for p in range(DIFF_PAGES_PER_STEP)], axis=1) * ATTN_SCALE
    m_old = m_ref[...]
    m_new = jnp.maximum(m_old, jnp.max(s, axis=-1, keepdims=True))
    alpha = jnp.exp(m_old - m_new)
    p = jnp.exp(s - m_new)
    l_ref[...] = alpha * l_ref[...] + jnp.sum(p, axis=-1, keepdims=True)
    m_ref[...] = m_new
    pb = p.astype(BF16)
    keys = DIFF_PAGES_PER_STEP * PAGE_SIZE
    pv = jnp.concatenate([jnp.dot(pb, vbuf[slot, pl.ds(h, keys, stride=DIFF_HEADS), :].astype(BF16),
                                  preferred_element_type=F32) for h in range(DIFF_HEADS)], axis=1)
    acc_ref[...] = alpha * acc_ref[...] + pv

    @pl.when(c == nc - 1)
    def _():
        s_new = jnp.sum(q8f * dknew_ref[0], axis=-1, keepdims=True) * ATTN_SCALE
        m_old = m_ref[...]
        m_new = jnp.maximum(m_old, s_new)
        alpha = jnp.exp(m_old - m_new)
        p_new = jnp.exp(s_new - m_new)
        l = alpha * l_ref[...] + p_new
        o = (alpha * acc_ref[...] + p_new * dvnew_ref[0]) / l
        lam = _lambda_value(lam_ref, lam_init)
        for h in range(DIFF_HEADS):
            cs = slice(h * DIFF_VDIM, (h + 1) * DIFF_VDIM)
            o_ref[0, :, cs] = _diff_merge_rows(o[2 * h:2 * h + 1, cs], o[2 * h + 1:2 * h + 2, cs], lam,
                                               g_ref[...], lam_init)


def _diff_sample(pt_flat, lam4, dq, dknew, dvnew, subln_g, pool_k, pool_v, nb, n_pages, lam_init):
    n_pool = pool_k.shape[0]
    nc = n_pages // DIFF_PAGES_PER_STEP
    keys = DIFF_PAGES_PER_STEP * PAGE_SIZE
    per_b = pl.BlockSpec((1, 1, DIFF_WIDTH), lambda b, c, pt: (b, 0, 0))
    anyspec = pl.BlockSpec(memory_space=pl.ANY)
    kern = functools.partial(_diff_sample_kernel, nb=nb, n_pages=n_pages, lam_init=lam_init)
    r3 = lambda a: a.reshape(nb, 1, DIFF_WIDTH)
    return pl.pallas_call(
        kern,
        grid_spec=pltpu.PrefetchScalarGridSpec(
            num_scalar_prefetch=1,
            grid=(nb, nc),
            in_specs=[pl.BlockSpec((4, HEAD_DIM), lambda b, c, pt: (0, 0)), per_b, per_b, per_b,
                      pl.BlockSpec((1, DIFF_VDIM), lambda b, c, pt: (0, 0)), anyspec, anyspec],
            out_specs=per_b,
            scratch_shapes=[pltpu.VMEM((2, DIFF_PAGES_PER_STEP, DIFF_WIDTH, PAGE_SIZE), F32),
                            pltpu.VMEM((2, keys * DIFF_HEADS, DIFF_VDIM), F32),
                            pltpu.SemaphoreType.DMA((2,)),
                            pltpu.VMEM((8, 1), F32), pltpu.VMEM((8, 1), F32), pltpu.VMEM((8, DIFF_WIDTH), F32)]),
        out_shape=jax.ShapeDtypeStruct((nb, 1, DIFF_WIDTH), F32),
        compiler_params=_cparams(("arbitrary", "arbitrary")),
        name="diff_sample",
    )(pt_flat, lam4, r3(dq), r3(dknew), r3(dvnew), subln_g,
      jnp.transpose(pool_k, (0, 2, 3, 4, 1)).reshape(n_pool, DIFF_WIDTH, PAGE_SIZE),
      pool_v.reshape(n_pool, PAGE_SIZE * DIFF_HEADS, DIFF_VDIM))


def _prep_w_in(w):
    splits = (NSA_WIDTH, KV_W, KV_W, KV_W, KV_W, KV_W, KV_W, NSA_HEADS * 3, DIFF_WIDTH, DIFF_WIDTH, DIFF_WIDTH)
    cuts = [int(c) for c in np.cumsum(splits)[:-1]]
    q, ck, cv, sk, sv, wk, wv, gate, dq, dk, dv = jnp.split(w, cuts, axis=1)
    q = q.reshape(D_MODEL, NSA_GROUPS, NSA_REP, HEAD_DIM).transpose(0, 2, 1, 3).reshape(D_MODEL, NSA_WIDTH)
    gate = gate.reshape(D_MODEL, NSA_GROUPS, NSA_REP, 3).transpose(0, 3, 2, 1).reshape(D_MODEL, NSA_HEADS * 3)
    gate = jnp.pad(gate, ((0, 0), (0, LANES - NSA_HEADS * 3)))
    return jnp.concatenate([q, ck, cv, sk, sv, wk, wv, dq, dk, dv, gate], axis=1).astype(BF16)


def _prep_w_out(w):
    won = w[:NSA_WIDTH].reshape(NSA_GROUPS, NSA_REP, HEAD_DIM, D_MODEL).transpose(1, 0, 2, 3)
    return won.reshape(NSA_WIDTH, D_MODEL).astype(BF16), w[NSA_WIDTH:].astype(BF16)


def _prep_ffn(p):
    won, wod = _prep_w_out(p["w_out"][0])
    return (won, wod, p["ffn_norm"], p["w_up"][0].astype(BF16), p["conv_w"][0], p["conv_b"],
            p["w_down"][0].astype(BF16), p["final_norm"].reshape(1, D_MODEL))


def _rope_tables(pos):
    half = HEAD_DIM // 2
    inv = 1.0 / (ROPE_THETA ** (jnp.arange(half, dtype=F32) / half))
    ang = pos.astype(F32)[:, None] * inv[None, :]
    cos, sin = jnp.cos(ang), jnp.sin(ang)
    return jnp.tile(cos, (1, 4)), jnp.tile(jnp.concatenate([-sin, sin], axis=1), (1, 2))


def _prompt_group(xp, attn_norm, w_in_r, cw_k, cw_v, lam4, subln_g, ffn_ws, lam_init):
    B, T, _ = xp.shape
    x2d = xp.reshape(B * T, D_MODEL)
    tm = min(256, T)
    cos_t, sin_t = _rope_tables(jnp.arange(T, dtype=jnp.int32))
    (qraw, qrot, ck, cv, sk, sv, wk, wv, gate, dq, dk, dv,
     ck_t, cv_t, sk_t, sv_t, wk_t, wv_t, dk_t) = _project(x2d, attn_norm, w_in_r, cos_t, sin_t, tm,
                                                          transposed_batch=B)
    kc = _compress_prompt(ck.reshape(B, T, KV_W), cw_k)
    vc = _compress_prompt(cv.reshape(B, T, KV_W), cw_v)
    o_nsa = _nsa_prompt(qraw, qrot, gate, kc, vc, sk, sv, wk, wv, B, T, tq=256, tk=256)
    o_diff = _diff_prompt(lam4, dq, dk, dv, subln_g, B, T, tq=min(512, T), tk=512, lam_init=lam_init)
    y, conv = _ffn_seq(x2d, o_nsa, o_diff, ffn_ws, B, T, tm)
    kv = lambda a: jnp.transpose(a.reshape(B, NSA_GROUPS, HEAD_DIM, a.shape[2]), (0, 3, 1, 2))[None]
    wb = min(WINDOW, T)
    dk_out = jnp.transpose(dk_t.reshape(B, DIFF_HEADS, 2, HEAD_DIM, T), (0, 4, 1, 2, 3))[None]
    return (y.reshape(B, T, D_MODEL), kv(ck_t), kv(cv_t), kv(sk_t), kv(sv_t),
            dk_out, dv.reshape(1, B, T, DIFF_HEADS, DIFF_VDIM),
            kv(wk_t[:, :, T - wb:]), kv(wv_t[:, :, T - wb:]), conv[None])


def _sample_group(xs, caches, page_table, attn_norm, w_in_r, cw_k, cw_v, lam4, subln_g, ffn_ws, lam_init):
    pool_ck, pool_cv, pool_sk, pool_sv, pool_dk, pool_dv, win_k, win_v, conv_state = caches
    nb = xs.shape[0]
    n_pages = page_table.shape[1]
    past = n_pages * PAGE_SIZE
    t_pad = -(-(past + 1) // SEL_BLOCK) * SEL_BLOCK
    n_cmp = t_pad // CMP_STRIDE - CMP_BLOCK // CMP_STRIDE + 1
    n_sel = t_pad // SEL_BLOCK
    rows = -(-(t_pad // CMP_STRIDE) // 8) * 8
    rows_out = -(-rows // LANES) * LANES
    pt_flat = page_table.reshape(-1)
    x2d = xs.reshape(nb, D_MODEL)
    pos = jnp.full((nb,), past, dtype=jnp.int32)
    cos_t, sin_t = _rope_tables(pos)
    qraw, qrot, ck, cv, sk, sv, wk, wv, gate, dq, dk, dv = _project(x2d, attn_norm, w_in_r, cos_t, sin_t, nb)
    kc = _compress_sample(pool_ck, pt_flat, ck, cw_k, nb, n_pages, rows, rows_out)
    vc = _compress_sample(pool_cv, pt_flat, cv, cw_v, nb, n_pages, rows, rows_out)
    o_c, info = _nsa_sample_a(qraw.reshape(nb, NSA_REP, LANES), kc, vc, past, n_cmp, n_sel)
    sel_flat = info[:, 0:NSA_GROUPS, 0:SEL_TOPN].reshape(-1)
    wbuf = win_k.shape[1]
    o_nsa, swk, swv = _nsa_sample_b(sel_flat, pt_flat, qrot.reshape(nb, NSA_REP, LANES),
                                    gate.reshape(nb, 1, LANES), o_c, info, pool_sk, pool_sv, sk, sv,
                                    _pages_transposed(win_k), _pages_transposed(win_v), wk, wv, nb, n_pages)
    from_t = lambda a: jnp.transpose(a.reshape(nb, NSA_GROUPS, HEAD_DIM, wbuf), (0, 3, 1, 2))[None]
    o_diff = _diff_sample(pt_flat, lam4, dq, dk, dv, subln_g, pool_dk, pool_dv, nb, n_pages, lam_init)
    y, u = _ffn_step(x2d, o_nsa.reshape(nb, NSA_WIDTH), o_diff.reshape(nb, DIFF_WIDTH), ffn_ws,
                     conv_state[:, 0], conv_state[:, 1])
    kv = lambda a: a.reshape(1, nb, 1, NSA_GROUPS, HEAD_DIM)
    return (y.reshape(nb, 1, D_MODEL), kv(ck), kv(cv), kv(sk), kv(sv),
            dk.reshape(1, nb, 1, DIFF_HEADS, 2, HEAD_DIM), dv.reshape(1, nb, 1, DIFF_HEADS, DIFF_VDIM),
            from_t(swk), from_t(swv),
            jnp.stack([conv_state[:, 1], u], axis=1)[None])


def kernel(x_prompt, x_sample, cache_cmp_k, cache_cmp_v, cache_sel_k, cache_sel_v, cache_diff_k, cache_diff_v,
           cache_win_k, cache_win_v, state_ffn_conv, page_table, attn_norm, w_in, cmp_pos_k, cmp_w1_k, cmp_w2_k,
           cmp_pos_v, cmp_w1_v, cmp_w2_v, lambda_q1, lambda_k1, lambda_q2, lambda_k2, subln_g, w_out, ffn_norm,
           w_up, conv_w, conv_b, w_down, final_norm):
    assert w_in.shape[0] == 1 and x_sample.shape[1] == 1, "one layer, one new token per sequence"
    lam_init = 0.8 - 0.6 * math.exp(0.0)
    w_in_r = _prep_w_in(w_in[0])
    cw_k = _compress_weights(cmp_pos_k[0], cmp_w1_k[0], cmp_w2_k[0])
    cw_v = _compress_weights(cmp_pos_v[0], cmp_w1_v[0], cmp_w2_v[0])
    lam4 = jnp.stack([lambda_q1[0], lambda_k1[0], lambda_q2[0], lambda_k2[0]])
    ffn_ws = _prep_ffn(dict(w_out=w_out, ffn_norm=ffn_norm, w_up=w_up, conv_w=conv_w, conv_b=conv_b,
                            w_down=w_down, final_norm=final_norm))
    p = _prompt_group(x_prompt, attn_norm, w_in_r, cw_k, cw_v, lam4, subln_g, ffn_ws, lam_init)
    caches = (cache_cmp_k[0], cache_cmp_v[0], cache_sel_k[0], cache_sel_v[0], cache_diff_k[0], cache_diff_v[0],
              cache_win_k[0], cache_win_v[0], state_ffn_conv[0])
    s = _sample_group(x_sample, caches, page_table, attn_norm, w_in_r, cw_k, cw_v, lam4, subln_g, ffn_ws,
                      lam_init)
    return (p[0], s[0]) + tuple(p[1:]) + tuple(s[1:])
```

```python
import functools
import math

import numpy as np
import jax
import jax.numpy as jnp
from jax import lax
from jax.experimental import pallas as pl
from jax.experimental.pallas import tpu as pltpu

F32 = jnp.float32
BF16 = jnp.bfloat16

D_MODEL = 1024
HEAD_DIM = 64
NSA_WIDTH = 512
NSA_HEADS = 8
NSA_GROUPS = 2
NSA_REP = 4
KV_W = NSA_GROUPS * HEAD_DIM
CMP_BLOCK = 32
CMP_STRIDE = 16
CMP_HIDDEN = 128
SEL_BLOCK = 64
SEL_TOPN = 16
WINDOW = 512
FORCE_BONUS = 1.0e4
DIFF_WIDTH = 512
DIFF_VDIM = 128
DIFF_HEADS = 4
D_FF = 2816
CONV_W = 3
ROPE_THETA = 10000.0
RMS_EPS = 1e-6
ATTN_SCALE = 1.0 / math.sqrt(HEAD_DIM)
SCALE_LOG2E = ATTN_SCALE * math.log2(math.e)
PAGE_SIZE = 128

LANES = 128
VMEM_LIMIT = 48 * 1024 * 1024
NEG_INF = float("-inf")

C_Q = 0
C_CK, C_CV, C_SK, C_SV, C_WK, C_WV = 512, 640, 768, 896, 1024, 1152
C_DQ, C_DK, C_DV = 1280, 1792, 2304
C_GATE = 2816
W_IN_COLS = 2944


def _cparams(sem):
    return pltpu.CompilerParams(dimension_semantics=sem, vmem_limit_bytes=VMEM_LIMIT)


def _rope128(v, cos, sin_signed):
    lane = lax.broadcasted_iota(jnp.int32, v.shape, 1)
    first = (lane % HEAD_DIM) < (HEAD_DIM // 2)
    partner = jnp.where(first, pltpu.roll(v, LANES - HEAD_DIM // 2, 1), pltpu.roll(v, HEAD_DIM // 2, 1))
    return v * cos + partner * sin_signed


def _proj_kernel(x_ref, g_ref, w_ref, cos_ref, sin_ref,
                 qraw_ref, qrot_ref, ck_ref, cv_ref, sk_ref, sv_ref, wk_ref, wv_ref,
                 gate_ref, dq_ref, dk_ref, dv_ref, *t_refs):
    x = x_ref[...]
    h = x * lax.rsqrt(jnp.mean(x * x, axis=-1, keepdims=True) + RMS_EPS) * g_ref[...]
    z = jnp.dot(h.astype(BF16), w_ref[...], preferred_element_type=F32)
    cos = cos_ref[...]
    sin = sin_ref[...]

    def put(ref, col, width, rope, t_ref=None):
        for c in range(width // LANES):
            v = z[:, col + c * LANES: col + (c + 1) * LANES]
            v = _rope128(v, cos, sin) if rope else v
            ref[:, c * LANES:(c + 1) * LANES] = v
            if t_ref is not None:
                t_ref[0, c * LANES:(c + 1) * LANES, :] = v.T

    tr = list(t_refs) if t_refs else [None] * 7
    put(qraw_ref, C_Q, NSA_WIDTH, False)
    put(qrot_ref, C_Q, NSA_WIDTH, True)
    put(ck_ref, C_CK, KV_W, False, tr[0])
    put(cv_ref, C_CV, KV_W, False, tr[1])
    put(sk_ref, C_SK, KV_W, True, tr[2])
    put(sv_ref, C_SV, KV_W, False, tr[3])
    put(wk_ref, C_WK, KV_W, True, tr[4])
    put(wv_ref, C_WV, KV_W, False, tr[5])
    put(dq_ref, C_DQ, DIFF_WIDTH, True)
    put(dk_ref, C_DK, DIFF_WIDTH, True, tr[6])
    put(dv_ref, C_DV, DIFF_WIDTH, False)
    gate_ref[...] = jax.nn.sigmoid(z[:, C_GATE:C_GATE + LANES])


def _project(x2d, norm_g, w_in_r, cos_t, sin_t, tm, transposed_batch=0):
    M = x2d.shape[0]
    Tt = cos_t.shape[0]
    nt = Tt // tm
    widths = (NSA_WIDTH, NSA_WIDTH, KV_W, KV_W, KV_W, KV_W, KV_W, KV_W, LANES,
              DIFF_WIDTH, DIFF_WIDTH, DIFF_WIDTH)
    row = lambda i: (i, 0)
    out_specs = [pl.BlockSpec((tm, w), row) for w in widths]
    out_shape = [jax.ShapeDtypeStruct((M, w), F32) for w in widths]
    if transposed_batch:
        for w in (KV_W,) * 6 + (DIFF_WIDTH,):
            out_specs.append(pl.BlockSpec((1, w, tm), lambda i: (i // nt, 0, i % nt)))
            out_shape.append(jax.ShapeDtypeStruct((transposed_batch, w, Tt), F32))
    return pl.pallas_call(
        _proj_kernel,
        grid=(M // tm,),
        in_specs=[pl.BlockSpec((tm, D_MODEL), row),
                  pl.BlockSpec((1, D_MODEL), lambda i: (0, 0)),
                  pl.BlockSpec((D_MODEL, W_IN_COLS), lambda i: (0, 0)),
                  pl.BlockSpec((tm, LANES), lambda i: (i % nt, 0)),
                  pl.BlockSpec((tm, LANES), lambda i: (i % nt, 0))],
        out_specs=out_specs,
        out_shape=out_shape,
        compiler_params=_cparams(("arbitrary",)),
        name="in_proj",
    )(x2d, norm_g, w_in_r, cos_t, sin_t)


def _gelu(x):
    return 0.5 * x * (1.0 + jnp.tanh(math.sqrt(2.0 / math.pi) * (x + 0.044715 * (x * x * x))))


def _compress_rows(c, w1cat, posb, w1n, w2bd):
    r = jnp.dot(c.astype(BF16), w1cat, preferred_element_type=F32)
    return _compress_finish(r, posb, w1n, w2bd)


def _compress_finish(r, posb, w1n, w2bd):
    R = r.shape[0]
    bias = jnp.dot(posb, w1n, preferred_element_type=F32)[0:1]
    bias2 = jnp.concatenate([bias, bias], axis=1)
    hid = r[:, :2 * CMP_HIDDEN] + pltpu.roll(r[:, 2 * CMP_HIDDEN:], R - 1, 0) + bias2
    return jnp.dot(_gelu(hid).astype(BF16), w2bd, preferred_element_type=F32)


def _compress_kernel(c_ref, w1cat_ref, posb_ref, w1n_ref, w2bd_ref, o_ref):
    o_ref[0] = _compress_rows(c_ref[0], w1cat_ref[...], posb_ref[...], w1n_ref[...], w2bd_ref[...])


def _compress_weights(pos, w1, w2):
    w1r = w1.reshape(2, CMP_STRIDE, HEAD_DIM, CMP_HIDDEN)
    eye = jnp.eye(NSA_GROUPS, dtype=w1.dtype)
    big = jnp.einsum('isdh,gk->isgdkh', w1r, eye).reshape(2, CMP_STRIDE * KV_W, NSA_GROUPS * CMP_HIDDEN)
    w1cat = jnp.concatenate([big[0], big[1]], axis=1).astype(BF16)
    w2bd = jnp.einsum('hd,gk->ghkd', w2, eye).reshape(NSA_GROUPS * CMP_HIDDEN, KV_W).astype(BF16)
    posb = jnp.broadcast_to(pos.reshape(1, CMP_BLOCK * HEAD_DIM), (8, CMP_BLOCK * HEAD_DIM)).astype(BF16)
    return w1cat, posb, w1.astype(BF16), w2bd


def _compress_prompt(x, cw):
    B, T, _ = x.shape
    R = T // CMP_STRIDE
    c = x.reshape(B, R, CMP_STRIDE * KV_W)
    full = lambda a: pl.BlockSpec(a.shape, lambda b: (0,) * a.ndim)
    return pl.pallas_call(
        _compress_kernel,
        grid=(B,),
        in_specs=[pl.BlockSpec((1, R, CMP_STRIDE * KV_W), lambda b: (b, 0, 0))] + [full(a) for a in cw],
        out_specs=pl.BlockSpec((1, R, KV_W), lambda b: (b, 0, 0)),
        out_shape=jax.ShapeDtypeStruct((B, R, KV_W), F32),
        compiler_params=_cparams(("arbitrary",)),
        name="compress_prompt",
    )(c, *cw)


def _lane_group_masks(shape):
    lane = lax.broadcasted_iota(jnp.int32, shape, 1)
    lo = lane < HEAD_DIM
    return lo, jnp.logical_not(lo)


def _block_diag_rows(x, dtype):
    lo, hi = _lane_group_masks(x.shape)
    return jnp.concatenate([jnp.where(lo, x, 0.0), jnp.where(hi, x, 0.0)], axis=0).astype(dtype)


def _dot_nt(a, b):
    return lax.dot_general(a, b, (((1,), (1,)), ((), ())), preferred_element_type=F32)


SUM_ROWS = 16


def _values_t(v_bd):
    n = v_bd.shape[0] // NSA_GROUPS
    row = lax.broadcasted_iota(jnp.int32, (SUM_ROWS, v_bd.shape[0]), 0)
    col = lax.broadcasted_iota(jnp.int32, (SUM_ROWS, v_bd.shape[0]), 1)
    ones = jnp.where(row == col // n, 1.0, 0.0)
    return jnp.concatenate([v_bd.T, ones], axis=0).astype(BF16)


def _flash_step_t(s_t, mask_t, m_ref, acc_ref, v_t, tk):
    ps, alphas = [], []
    for g in range(NSA_GROUPS):
        sg = s_t[g * tk:(g + 1) * tk]
        if mask_t is not None:
            sg = jnp.where(mask_t, sg, NEG_INF)
        m_old = m_ref[g]
        m_new = jnp.maximum(m_old, jnp.max(sg, axis=0, keepdims=True))
        m_safe = jnp.where(m_new == NEG_INF, 0.0, m_new)
        alphas.append(jnp.exp2(m_old - m_safe))
        ps.append(jnp.exp2(sg - m_safe).astype(BF16))
        m_ref[g] = m_new
        rs = slice(g * HEAD_DIM, (g + 1) * HEAD_DIM)
        acc_ref[rs, :] = alphas[g] * acc_ref[rs, :]
    srow = lax.broadcasted_iota(jnp.int32, (SUM_ROWS, 1), 0)
    acc_ref[KV_W:, :] = jnp.where(srow == 0, alphas[0], alphas[1]) * acc_ref[KV_W:, :]
    acc_ref[...] = acc_ref[...] + jnp.dot(v_t, jnp.concatenate(ps, axis=0), preferred_element_type=F32)


def _finish_t(acc_ref):
    parts = [acc_ref[g * HEAD_DIM:(g + 1) * HEAD_DIM, :]
             * (1.0 / jnp.maximum(acc_ref[KV_W + g:KV_W + g + 1, :], 1e-30)) for g in range(NSA_GROUPS)]
    return jnp.concatenate(parts, axis=0)


def _reset(m_ref, acc_ref):
    m_ref[...] = jnp.full(m_ref.shape, NEG_INF, F32)
    acc_ref[...] = jnp.zeros(acc_ref.shape, F32)


def _queries_t(q, tq):
    return jnp.concatenate([(q[:, r * LANES:(r + 1) * LANES] * SCALE_LOG2E).T for r in range(NSA_REP)],
                           axis=1).astype(BF16)


def _outputs_from_t(o_t, tq):
    return jnp.concatenate([o_t[:, r * tq:(r + 1) * tq].T for r in range(NSA_REP)], axis=1)


def _split2(x):
    hi = x.astype(BF16)
    return hi, (x - hi.astype(F32)).astype(BF16)


def _select_blocks(score_t, q_pos_row, n_sel):
    n_pad = score_t.shape[0] // NSA_GROUPS
    tq = score_t.shape[1]
    out = []
    blk = lax.broadcasted_iota(jnp.int32, (n_pad, tq), 0)
    cur = q_pos_row // SEL_BLOCK
    forced = (blk == 0) | (blk == cur) | (blk == cur - 1)
    ok = (blk * SEL_BLOCK <= q_pos_row) & (blk < n_sel)
    for g in range(NSA_GROUPS):
        sc = score_t[g * n_pad:(g + 1) * n_pad]
        sc = jnp.where(ok, sc + jnp.where(forced, FORCE_BONUS, 0.0), NEG_INF)
        n_tiles = n_pad // 8
        tiles = [sc[8 * t:8 * t + 8] for t in range(n_tiles)]
        cnts = [jnp.zeros((8, tq), F32) for _ in range(n_tiles)]
        sub = lax.broadcasted_iota(jnp.int32, (8, tq), 0)
        for i in range(n_pad):
            row = sc[i:i + 1]
            ti, ri = divmod(i, 8)
            for t in range(n_tiles):
                if t > ti:
                    ahead = jnp.where(row >= tiles[t], 1.0, 0.0)
                elif t < ti:
                    ahead = jnp.where(row > tiles[t], 1.0, 0.0)
                else:
                    ahead = jnp.where(sub > ri, jnp.where(row >= tiles[t], 1.0, 0.0),
                                      jnp.where(row > tiles[t], 1.0, 0.0))
                cnts[t] = cnts[t] + ahead
        cnt = jnp.concatenate(cnts, axis=0)
        out.append(jnp.where((cnt < float(SEL_TOPN)) & ok, 1.0, 0.0))
    return jnp.concatenate(out, axis=0)


def _nsa_prompt_kernel(qraw_ref, qrot_ref, gate_ref, kc_ref, vc_ref, sk_ref, sv_ref, wk_ref, wv_ref,
                       ovt_ref, gexp_ref, o_ref,
                       skbd, svt, wkbd, wvt, kcbd, vct, sel_ref, m_ref, acc_ref, s0_ref, s1_ref, *, tq, tk, seq):
    i = pl.program_id(1)
    n_kt = seq // tk
    n_sel = seq // SEL_BLOCK
    n_cmp = kc_ref.shape[1]
    n_pad = sel_ref.shape[0] // NSA_GROUPS
    L = NSA_REP * tq
    blk_per_tile = tk // SEL_BLOCK

    @pl.when(i == 0)
    def _():
        def fill(j, _):
            rows = pl.ds(pl.multiple_of(j * tk, tk), tk)
            skbd[j] = _block_diag_rows(sk_ref[0, rows, :], BF16)
            wkbd[j] = _block_diag_rows(wk_ref[0, rows, :], BF16)
            svt[j] = _values_t(_block_diag_rows(sv_ref[0, rows, :], F32))
            wvt[j] = _values_t(_block_diag_rows(wv_ref[0, rows, :], F32))
            return 0
        lax.fori_loop(0, n_kt, fill, 0)
        kcbd[...] = _block_diag_rows(kc_ref[0], BF16)
        vct[...] = _block_diag_rows(vc_ref[0], F32).T.astype(BF16)

    q0 = i * tq
    q_pos_row = q0 + lax.broadcasted_iota(jnp.int32, (1, tq), 1)
    q_pos = jnp.concatenate([q_pos_row] * NSA_REP, axis=1)

    s = jnp.dot(kcbd[...], _queries_t(qraw_ref[...], tq), preferred_element_type=F32)
    kidx = lax.broadcasted_iota(jnp.int32, (n_cmp, 1), 0)
    cmask = (kidx * CMP_STRIDE + (CMP_BLOCK - 1) <= q_pos) & (kidx < n_cmp - 1)
    pcs = []
    for g in range(NSA_GROUPS):
        sg = jnp.where(cmask, s[g * n_cmp:(g + 1) * n_cmp], NEG_INF)
        m = jnp.max(sg, axis=0, keepdims=True)
        m = jnp.where(m == NEG_INF, 0.0, m)
        p = jnp.exp2(sg - m)
        pcs.append(p * (1.0 / jnp.maximum(jnp.sum(p, axis=0, keepdims=True), 1e-30)))
    pc = jnp.concatenate(pcs, axis=0)
    o_c = jnp.dot(vct[...], pc.astype(BF16), preferred_element_type=F32)
    imp = pc[:, 0:tq] + pc[:, tq:2 * tq] + pc[:, 2 * tq:3 * tq] + pc[:, 3 * tq:4 * tq]
    ih, il = _split2(imp)
    score = (jnp.dot(ovt_ref[...], ih, preferred_element_type=F32)
             + jnp.dot(ovt_ref[...], il, preferred_element_type=F32))
    sel = _select_blocks(score, q_pos_row, n_sel)
    for jt in range(n_pad // blk_per_tile):
        for g in range(NSA_GROUPS):
            src = g * n_pad + jt * blk_per_tile
            dst = (jt * NSA_GROUPS + g) * blk_per_tile
            sel_ref[dst:dst + blk_per_tile, :] = sel[src:src + blk_per_tile]

    qr = _queries_t(qrot_ref[...], tq)
    kiota = lax.broadcasted_iota(jnp.int32, (tk, 1), 0)

    n_bias = NSA_GROUPS * blk_per_tile

    def sel_scores(j):
        return jnp.dot(skbd[j], qr, preferred_element_type=F32)

    def sel_consume(j, s, causal):
        sel8 = sel_ref[pl.ds(pl.multiple_of(j * n_bias, n_bias), n_bias), :]
        parts = [jnp.broadcast_to(sel8[r:r + 1], (SEL_BLOCK, tq)) for r in range(n_bias)]
        m1 = jnp.concatenate(parts, axis=0) > 0.5
        s = jnp.where(jnp.concatenate([m1] * NSA_REP, axis=1), s, NEG_INF)
        mask = ((j * tk + kiota) <= q_pos) if causal else None
        _flash_step_t(s, mask, m_ref, acc_ref, svt[j], tk)

    _reset(m_ref, acc_ref)
    n_full = q0 // tk
    s0_ref[...] = sel_scores(0)

    def pair(t, _):
        j = 2 * t
        s1_ref[...] = sel_scores(j + 1)
        sel_consume(j, s0_ref[...], False)
        s0_ref[...] = sel_scores(j + 2)
        sel_consume(j + 1, s1_ref[...], False)
        return 0

    lax.fori_loop(0, n_full // 2, pair, 0)

    @pl.when(n_full % 2 == 1)
    def _():
        s1_ref[...] = sel_scores(n_full)
        sel_consume(n_full - 1, s0_ref[...], False)
        sel_consume(n_full, s1_ref[...], True)

    @pl.when(n_full % 2 == 0)
    def _():
        sel_consume(n_full, s0_ref[...], True)
    o_s = _finish_t(acc_ref)

    _reset(m_ref, acc_ref)
    n_win = (WINDOW + tk - 2) // tk + 1

    def win_scores(w):
        return jnp.dot(wkbd[jnp.maximum(n_full - w, 0)], qr, preferred_element_type=F32)

    def win_consume(w, s):
        jw = n_full - w
        kpos = jnp.where(jw >= 0, jw, 2 * n_kt) * tk + kiota
        mask = (kpos <= q_pos) & (kpos > q_pos - WINDOW)
        _flash_step_t(s, mask, m_ref, acc_ref, wvt[jnp.maximum(jw, 0)], tk)

    bufs = (s0_ref, s1_ref)
    bufs[0][...] = win_scores(n_win - 1)
    for k, w in enumerate(range(n_win - 1, -1, -1)):
        if w > 0:
            bufs[(k + 1) % 2][...] = win_scores(w - 1)
        win_consume(w, bufs[k % 2][...])
    o_w = _finish_t(acc_ref)

    gh, gl = _split2(gate_ref[...])
    gx = (jnp.dot(gh, gexp_ref[...], preferred_element_type=F32)
          + jnp.dot(gl, gexp_ref[...], preferred_element_type=F32))
    o = (gx[:, 0:NSA_WIDTH] * _outputs_from_t(o_c, tq)
         + gx[:, NSA_WIDTH:2 * NSA_WIDTH] * _outputs_from_t(o_s, tq)
         + gx[:, 2 * NSA_WIDTH:3 * NSA_WIDTH] * _outputs_from_t(o_w, tq))
    o_ref[...] = o


def _overlap_matrix(n_cmp_rows, n_sel_pad):
    cs = np.arange(n_cmp_rows)[:, None] * CMP_STRIDE
    ss = np.arange(n_sel_pad)[None, :] * SEL_BLOCK
    ov = ((cs < ss + SEL_BLOCK) & (cs + CMP_BLOCK > ss)).astype(np.float32)
    z = np.zeros_like(ov)
    return jnp.asarray(np.block([[ov, z], [z, ov]]), BF16)


def _gate_expand_matrix():
    e = np.zeros((LANES, 3 * NSA_WIDTH), np.float32)
    for c in range(3):
        for r in range(NSA_REP):
            for g in range(NSA_GROUPS):
                col = c * NSA_WIDTH + r * LANES + g * HEAD_DIM
                e[c * NSA_HEADS + r * NSA_GROUPS + g, col:col + HEAD_DIM] = 1.0
    return jnp.asarray(e, BF16)


def _nsa_prompt(qraw, qrot, gate, kc, vc, sk, sv, wk, wv, B, T, tq, tk):
    nq = T // tq
    n_cmp = kc.shape[1]
    n_sel_pad = max(T // SEL_BLOCK, HEAD_DIM)
    assert tq == tk, "the kernel's tile schedule has exactly one diagonal tile per query tile"
    ovt = _overlap_matrix(n_cmp, n_sel_pad).T
    gexp = _gate_expand_matrix()
    qspec = lambda w: pl.BlockSpec((tq, w), lambda b, i: (b * nq + i, 0))
    kvspec = lambda n: pl.BlockSpec((1, n, KV_W), lambda b, i: (b, 0, 0))
    const = lambda a: pl.BlockSpec(a.shape, lambda b, i: (0, 0))
    L = NSA_REP * tq
    n_kt = T // tk
    kern = functools.partial(_nsa_prompt_kernel, tq=tq, tk=tk, seq=T)
    kv3 = lambda a: a.reshape(B, T, KV_W)
    return pl.pallas_call(
        kern,
        grid=(B, nq),
        in_specs=[qspec(NSA_WIDTH), qspec(NSA_WIDTH), qspec(LANES), kvspec(n_cmp), kvspec(n_cmp),
                  kvspec(T), kvspec(T), kvspec(T), kvspec(T), const(ovt), const(gexp)],
        out_specs=qspec(NSA_WIDTH),
        out_shape=jax.ShapeDtypeStruct((B * T, NSA_WIDTH), F32),
        scratch_shapes=[pltpu.VMEM((n_kt, 2 * tk, KV_W), BF16),
                        pltpu.VMEM((n_kt, KV_W + SUM_ROWS, 2 * tk), BF16),
                        pltpu.VMEM((n_kt, 2 * tk, KV_W), BF16),
                        pltpu.VMEM((n_kt, KV_W + SUM_ROWS, 2 * tk), BF16),
                        pltpu.VMEM((2 * n_cmp, KV_W), BF16), pltpu.VMEM((KV_W, 2 * n_cmp), BF16),
                        pltpu.VMEM((NSA_GROUPS * n_sel_pad, tq), F32),
                        pltpu.VMEM((NSA_GROUPS, 1, L), F32),
                        pltpu.VMEM((KV_W + SUM_ROWS, L), F32),
                        pltpu.VMEM((2 * tk, L), F32), pltpu.VMEM((2 * tk, L), F32)],
        compiler_params=_cparams(("arbitrary", "arbitrary")),
        name="nsa_prompt",
    )(qraw, qrot, gate, kc, vc, kv3(sk), kv3(sv), kv3(wk), kv3(wv), ovt, gexp)


def _lambda_value(lam_ref, lam_init):
    a = jnp.sum(lam_ref[0:1] * lam_ref[1:2], axis=-1, keepdims=True)
    b = jnp.sum(lam_ref[2:3] * lam_ref[3:4], axis=-1, keepdims=True)
    return jnp.exp(a) - jnp.exp(b) + lam_init


def _diff_merge_rows(o0, o1, lam, g, lam_init):
    od = o0 - lam * o1
    od = od * lax.rsqrt(jnp.mean(od * od, axis=-1, keepdims=True) + RMS_EPS) * g
    return od * (1.0 - lam_init)


def _diff_prompt_kernel(lam_ref, dq_ref, dk_ref, dv_ref, gcol_ref, o_ref,
                        kbf, vbf, m_ref, acc_ref, s0_ref, s1_ref, *, tq, tk, lam_init):
    i = pl.program_id(2)
    n_kt = kbf.shape[0] // tk

    @pl.when(i == 0)
    def _():
        kbf[...] = dk_ref[...].astype(BF16)
        ones = jnp.where(lax.broadcasted_iota(jnp.int32, (SUM_ROWS, tk), 0) == 0, 1.0, 0.0)

        def fill(j, _):
            vt = dv_ref[pl.ds(pl.multiple_of(j * tk, tk), tk), :].T
            vbf[j] = jnp.concatenate([vt, ones], axis=0).astype(BF16)
            return 0
        lax.fori_loop(0, n_kt, fill, 0)

    q = dq_ref[...] * SCALE_LOG2E
    lo, hi = _lane_group_masks(q.shape)
    q2t = jnp.concatenate([jnp.where(lo, q, 0.0), jnp.where(hi, q, 0.0)], axis=0).T.astype(BF16)
    q0 = i * tq
    q_pos = q0 + lax.broadcasted_iota(jnp.int32, (1, 2 * tq), 1) % tq
    kiota = lax.broadcasted_iota(jnp.int32, (tk, 1), 0)
    m_ref[...] = jnp.full(m_ref.shape, NEG_INF, F32)
    acc_ref[...] = jnp.zeros(acc_ref.shape, F32)

    def scores(j):
        k0 = pl.multiple_of(j * tk, tk)
        return jnp.dot(kbf[pl.ds(k0, tk), :], q2t, preferred_element_type=F32)

    def consume(j, s, causal):
        if causal:
            s = jnp.where((j * tk + kiota) <= q_pos, s, NEG_INF)
        m_old = m_ref[...]
        m_new = jnp.maximum(m_old, jnp.max(s, axis=0, keepdims=True))
        alpha = jnp.exp2(m_old - m_new)
        p = jnp.exp2(s - m_new).astype(BF16)
        m_ref[...] = m_new
        acc_ref[...] = alpha * acc_ref[...] + jnp.dot(vbf[j], p, preferred_element_type=F32)

    n_full = q0 // tk
    s0_ref[...] = scores(0)

    def pair(t, _):
        j = 2 * t
        s1_ref[...] = scores(j + 1)
        consume(j, s0_ref[...], False)
        s0_ref[...] = scores(j + 2)
        consume(j + 1, s1_ref[...], False)
        return 0

    lax.fori_loop(0, n_full // 2, pair, 0)

    @pl.when(n_full % 2 == 1)
    def _():
        s1_ref[...] = scores(n_full)
        consume(n_full - 1, s0_ref[...], False)
        consume(n_full, s1_ref[...], True)

    @pl.when(n_full % 2 == 0)
    def _():
        consume(n_full, s0_ref[...], True)
    o = acc_ref[0:DIFF_VDIM, :] * (1.0 / acc_ref[DIFF_VDIM:DIFF_VDIM + 1, :])
    lam = _lambda_value(lam_ref, lam_init)
    od = o[:, :tq] - lam * o[:, tq:]
    od = od * lax.rsqrt(jnp.mean(od * od, axis=0, keepdims=True) + RMS_EPS) * gcol_ref[...]
    o_ref[...] = (od * (1.0 - lam_init)).T


def _diff_prompt(lam4, dq, dk, dv, subln_g, B, T, tq, tk, lam_init):
    assert tq == tk, "the kernel's tile schedule has exactly one diagonal tile per query tile"
    nq = T // tq
    kern = functools.partial(_diff_prompt_kernel, tq=tq, tk=tk, lam_init=lam_init)
    return pl.pallas_call(
        kern,
        grid=(B, DIFF_HEADS, nq),
        in_specs=[pl.BlockSpec((4, HEAD_DIM), lambda b, h, i: (0, 0)),
                  pl.BlockSpec((tq, DIFF_VDIM), lambda b, h, i: (b * nq + i, h)),
                  pl.BlockSpec((T, DIFF_VDIM), lambda b, h, i: (b, h)),
                  pl.BlockSpec((T, DIFF_VDIM), lambda b, h, i: (b, h)),
                  pl.BlockSpec((DIFF_VDIM, 1), lambda b, h, i: (0, 0))],
        out_specs=pl.BlockSpec((tq, DIFF_VDIM), lambda b, h, i: (b * nq + i, h)),
        out_shape=jax.ShapeDtypeStruct((B * T, DIFF_WIDTH), F32),
        scratch_shapes=[pltpu.VMEM((T, DIFF_VDIM), BF16), pltpu.VMEM((T // tk, DIFF_VDIM + SUM_ROWS, tk), BF16),
                        pltpu.VMEM((1, 2 * tq), F32), pltpu.VMEM((DIFF_VDIM + SUM_ROWS, 2 * tq), F32),
                        pltpu.VMEM((tk, 2 * tq), F32), pltpu.VMEM((tk, 2 * tq), F32)],
        compiler_params=_cparams(("arbitrary", "arbitrary", "arbitrary")),
        name="diff_prompt",
    )(lam4, dq, dk, dv, subln_g.reshape(DIFF_VDIM, 1))


FF_CHUNK = 1408


def _rms(x, g):
    return x * lax.rsqrt(jnp.mean(x * x, axis=-1, keepdims=True) + RMS_EPS) * g


def _ffn_seq_kernel(x_ref, on_ref, od_ref, won_ref, wod_ref, g2_ref, wup_ref, cw_ref, cb_ref, wdn_ref,
                    gf_ref, y_ref, conv_ref, carry_ref, *, tm, tiles_per_seq):
    i = pl.program_id(0)
    first = (i % tiles_per_seq) == 0
    x1 = (x_ref[...] + jnp.dot(on_ref[...].astype(BF16), won_ref[...], preferred_element_type=F32)
          + jnp.dot(od_ref[...].astype(BF16), wod_ref[...], preferred_element_type=F32))
    h = _rms(x1, g2_ref[...]).astype(BF16)
    rowi = lax.broadcasted_iota(jnp.int32, (tm, 1), 0)
    f = jnp.zeros((tm, D_MODEL), F32)
    for k in range(D_FF // FF_CHUNK):
        halves = []
        for part in range(2):
            c0 = part * D_FF + k * FF_CHUNK
            cs = slice(c0, c0 + FF_CHUNK)
            u = jnp.dot(h, wup_ref[:, cs], preferred_element_type=F32)
            pm2 = jnp.where(first, 0.0, carry_ref[6:7, cs])
            pm1 = jnp.where(first, 0.0, carry_ref[7:8, cs])
            u1 = jnp.where(rowi == 0, pm1, pltpu.roll(u, 1, 0))
            u2 = jnp.where(rowi == 0, pm2, jnp.where(rowi == 1, pm1, pltpu.roll(u, 2, 0)))
            halves.append(cb_ref[:, cs] + u2 * cw_ref[0:1, cs] + u1 * cw_ref[1:2, cs] + u * cw_ref[2:3, cs])
            carry_ref[6:8, cs] = u[tm - 2:tm]
            conv_ref[0, :, cs] = u[tm - 2:tm]
        act = (jax.nn.silu(halves[0]) * halves[1]).astype(BF16)
        f = f + jnp.dot(act, wdn_ref[k * FF_CHUNK:(k + 1) * FF_CHUNK, :], preferred_element_type=F32)
    y_ref[...] = _rms(x1 + f, gf_ref[...])


def _ffn_step_kernel(x_ref, on_ref, od_ref, won_ref, wod_ref, g2_ref, wup_ref, cw_ref, cb_ref, wdn_ref,
                     gf_ref, p0_ref, p1_ref, y_ref, u_ref):
    x1 = (x_ref[...] + jnp.dot(on_ref[...].astype(BF16), won_ref[...], preferred_element_type=F32)
          + jnp.dot(od_ref[...].astype(BF16), wod_ref[...], preferred_element_type=F32))
    h = _rms(x1, g2_ref[...]).astype(BF16)
    f = jnp.zeros(x1.shape, F32)
    for k in range(D_FF // FF_CHUNK):
        halves = []
        for part in range(2):
            c0 = part * D_FF + k * FF_CHUNK
            cs = slice(c0, c0 + FF_CHUNK)
            u = jnp.dot(h, wup_ref[:, cs], preferred_element_type=F32)
            u_ref[:, cs] = u
            halves.append(cb_ref[:, cs] + p0_ref[:, cs] * cw_ref[0:1, cs] + p1_ref[:, cs] * cw_ref[1:2, cs]
                          + u * cw_ref[2:3, cs])
        act = (jax.nn.silu(halves[0]) * halves[1]).astype(BF16)
        f = f + jnp.dot(act, wdn_ref[k * FF_CHUNK:(k + 1) * FF_CHUNK, :], preferred_element_type=F32)
    y_ref[...] = _rms(x1 + f, gf_ref[...])


def _const_spec(a, ngrid):
    return pl.BlockSpec(a.shape, lambda *idx: (0,) * a.ndim, pipeline_mode=pl.Buffered(1))


def _ffn_weights_specs(ws):
    return [_const_spec(a, 1) for a in ws]


def _ffn_seq(x2d, o_nsa, o_diff, ws, B, T, tm):
    M = x2d.shape[0]
    tps = T // tm
    row = lambda w: pl.BlockSpec((tm, w), lambda i: (i, 0))
    kern = functools.partial(_ffn_seq_kernel, tm=tm, tiles_per_seq=tps)
    return pl.pallas_call(
        kern,
        grid=(M // tm,),
        in_specs=[row(D_MODEL), row(NSA_WIDTH), row(DIFF_WIDTH)] + _ffn_weights_specs(ws),
        out_specs=[row(D_MODEL), pl.BlockSpec((1, CONV_W - 1, 2 * D_FF), lambda i: (i // tps, 0, 0))],
        out_shape=[jax.ShapeDtypeStruct((M, D_MODEL), F32),
                   jax.ShapeDtypeStruct((B, CONV_W - 1, 2 * D_FF), F32)],
        scratch_shapes=[pltpu.VMEM((8, 2 * D_FF), F32)],
        compiler_params=_cparams(("arbitrary",)),
        name="ffn_prompt",
    )(x2d, o_nsa, o_diff, *ws)


def _ffn_step(x2d, o_nsa, o_diff, ws, p0, p1):
    M = x2d.shape[0]
    full = lambda a: pl.BlockSpec(a.shape, lambda i: (0,) * a.ndim)
    ins = (x2d, o_nsa, o_diff) + tuple(ws) + (p0, p1)
    return pl.pallas_call(
        _ffn_step_kernel,
        grid=(1,),
        in_specs=[full(a) for a in ins],
        out_specs=[pl.BlockSpec((M, D_MODEL), lambda i: (0, 0)), pl.BlockSpec((M, 2 * D_FF), lambda i: (0, 0))],
        out_shape=[jax.ShapeDtypeStruct((M, D_MODEL), F32), jax.ShapeDtypeStruct((M, 2 * D_FF), F32)],
        compiler_params=_cparams(("arbitrary",)),
        name="ffn_sample",
    )(*ins)


ROWS_PER_PAGE = PAGE_SIZE // CMP_STRIDE


def _compress_sample_kernel(pt_ref, pool_ref, tail_ref, w1cat_ref, posb_ref, w1n_ref, w2bd_ref, o_ref,
                            pbuf, xs, r_ref, sem, *, n_pages, nb, rows):
    b = pl.program_id(0)
    past = n_pages * PAGE_SIZE

    def page_copy(bb, n, slot):
        return pltpu.make_async_copy(pool_ref.at[pt_ref[bb * n_pages + n]], pbuf.at[slot, n], sem.at[slot])

    def start_all(bb, slot):
        def body(n, _):
            page_copy(bb, n, slot).start()
            return 0
        lax.fori_loop(0, n_pages, body, 0)

    @pl.when(b == 0)
    def _():
        start_all(0, 0)

    @pl.when(b + 1 < nb)
    def _():
        start_all(b + 1, (b + 1) % 2)

    slot = b % 2

    def wait_body(n, _):
        page_copy(b, n, slot).wait()
        return 0
    lax.fori_loop(0, n_pages, wait_body, 0)

    xs[past:past + 8, :] = tail_ref[0]
    xs[past + 8:, :] = jnp.zeros((xs.shape[0] - past - 8, KV_W), F32)

    def hidden_rows(r0, n):
        acc = None
        for s in range(0, CMP_STRIDE, 2):
            lhs = jnp.concatenate([xs[pl.ds(r0 * CMP_STRIDE + s, n, stride=CMP_STRIDE), :],
                                   xs[pl.ds(r0 * CMP_STRIDE + s + 1, n, stride=CMP_STRIDE), :]],
                                  axis=1).astype(BF16)
            part = jnp.dot(lhs, w1cat_ref[s * KV_W:(s + 2) * KV_W, :], preferred_element_type=F32)
            acc = part if acc is None else acc + part
        r_ref[r0:r0 + n, :] = acc

    n_chunks = 4
    pages_pc = n_pages // n_chunks
    rows_pc = pages_pc * ROWS_PER_PAGE
    for c in range(n_chunks):
        for n in range(c * pages_pc, (c + 1) * pages_pc):
            xs[n * PAGE_SIZE:(n + 1) * PAGE_SIZE, :] = pbuf[slot, n].T
        hidden_rows(c * rows_pc, rows_pc)
    hidden_rows(n_chunks * rows_pc, rows - n_chunks * rows_pc)
    o_ref[0, 0:rows, :] = _compress_finish(r_ref[...], posb_ref[...], w1n_ref[...], w2bd_ref[...])
    o_ref[0, rows:, :] = jnp.zeros((o_ref.shape[1] - rows, KV_W), F32)


def _compress_sample(pool, pt_flat, new_row, cw, nb, n_pages, rows, rows_out):
    pool_t = _pages_transposed(pool)
    tail = jnp.zeros((nb, 8, KV_W), F32).at[:, 0, :].set(new_row)
    kern = functools.partial(_compress_sample_kernel, n_pages=n_pages, nb=nb, rows=rows)
    full = lambda a: pl.BlockSpec(a.shape, lambda b, pt: (0,) * a.ndim)
    return pl.pallas_call(
        kern,
        grid_spec=pltpu.PrefetchScalarGridSpec(
            num_scalar_prefetch=1,
            grid=(nb,),
            in_specs=[pl.BlockSpec(memory_space=pl.ANY),
                      pl.BlockSpec((1, 8, KV_W), lambda b, pt: (b, 0, 0))] + [full(a) for a in cw],
            out_specs=pl.BlockSpec((1, rows_out, KV_W), lambda b, pt: (b, 0, 0)),
            scratch_shapes=[pltpu.VMEM((2, n_pages, KV_W, PAGE_SIZE), F32),
                            pltpu.VMEM((rows * CMP_STRIDE, KV_W), F32),
                            pltpu.VMEM((rows, 4 * CMP_HIDDEN), F32), pltpu.SemaphoreType.DMA((2,))]),
        out_shape=jax.ShapeDtypeStruct((nb, rows_out, KV_W), F32),
        compiler_params=_cparams(("arbitrary",)),
        name="compress_sample",
    )(pt_flat, pool_t, tail, *cw)


def _pages_transposed(pool):
    n_pool = pool.shape[0]
    return jnp.transpose(pool, (0, 2, 3, 1)).reshape(n_pool, KV_W, pool.shape[1])


def _pad_rows8(x):
    return jnp.concatenate([x, jnp.zeros((8 - x.shape[0], x.shape[1]), x.dtype)], axis=0)


def _nsa_sample_a_kernel(q_ref, kc_ref, vc_ref, ovbd_ref, oc_ref, info_ref, *, n_cmp, n_sel, q_pos):
    rk = kc_ref.shape[1]
    n_pad = ovbd_ref.shape[1] // NSA_GROUPS
    q8 = _pad_rows8(q_ref[0]).astype(BF16)
    kcbd = _block_diag_rows(kc_ref[0], BF16)
    vcbd = _block_diag_rows(vc_ref[0], BF16)
    s = _dot_nt(q8, kcbd)
    kidx = lax.broadcasted_iota(jnp.int32, s.shape, 1) % rk
    cmask = (kidx * CMP_STRIDE + (CMP_BLOCK - 1) <= q_pos) & (kidx < n_cmp)
    s = jnp.where(cmask, s * ATTN_SCALE, NEG_INF)
    pcs = []
    for g in range(NSA_GROUPS):
        sg = s[:, g * rk:(g + 1) * rk]
        m = jnp.max(sg, axis=-1, keepdims=True)
        m = jnp.where(m == NEG_INF, 0.0, m)
        p = jnp.exp(sg - m)
        pcs.append(p / jnp.maximum(jnp.sum(p, axis=-1, keepdims=True), 1e-30))
    pc = jnp.concatenate(pcs, axis=1)
    oc_ref[0] = jnp.dot(pc.astype(BF16), vcbd, preferred_element_type=F32)
    imp = jnp.sum(pc[0:NSA_REP], axis=0, keepdims=True)
    imp8 = jnp.concatenate([imp] * 8, axis=0)
    ih, il = _split2(imp8)
    score = (jnp.dot(ih, ovbd_ref[...], preferred_element_type=F32)
             + jnp.dot(il, ovbd_ref[...], preferred_element_type=F32))[0:1]

    blk_r = lax.broadcasted_iota(jnp.int32, (1, n_pad), 1)
    blk_c = lax.broadcasted_iota(jnp.int32, (n_pad, 1), 0)
    cur = q_pos // SEL_BLOCK
    forced_r = (blk_r == 0) | (blk_r == cur) | (blk_r == cur - 1)
    ok_r = (blk_r * SEL_BLOCK <= q_pos) & (blk_r < n_sel)
    ok_c = (blk_c * SEL_BLOCK <= q_pos) & (blk_c < n_sel)
    ii = lax.broadcasted_iota(jnp.int32, (n_pad, n_pad), 1)
    jj = lax.broadcasted_iota(jnp.int32, (n_pad, n_pad), 0)
    kk = lax.broadcasted_iota(jnp.int32, (n_pad, LANES), 1)
    jf = lax.broadcasted_iota(jnp.int32, (n_pad, LANES), 0).astype(F32)
    rows_out = []
    for g in range(NSA_GROUPS):
        sr = score[:, g * n_pad:(g + 1) * n_pad]
        sr = jnp.where(ok_r, sr + jnp.where(forced_r, FORCE_BONUS, 0.0), NEG_INF)
        sc = jnp.sum(jnp.where(ii == jj, sr, 0.0), axis=1, keepdims=True)
        ahead = jnp.where(ii < jj, jnp.where(sr >= sc, 1.0, 0.0), jnp.where(sr > sc, 1.0, 0.0))
        cnt = jnp.sum(ahead, axis=1, keepdims=True)
        chosen = (cnt < float(SEL_TOPN)) & ok_c
        hit = chosen & (cnt == kk.astype(F32))
        rows_out.append((jnp.sum(jnp.where(hit, jf, 0.0), axis=0, keepdims=True),
                         jnp.sum(jnp.where(hit, 1.0, 0.0), axis=0, keepdims=True)))
    info = jnp.concatenate([rows_out[0][0], rows_out[1][0], rows_out[0][1], rows_out[1][1],
                            jnp.zeros((4, LANES), F32)], axis=0)
    info_ref[0] = info.astype(jnp.int32)


def _nsa_sample_a(qraw3, kc, vc, q_pos, n_cmp, n_sel):
    nb, rk, _ = kc.shape
    n_pad = -(-n_sel // LANES) * LANES
    ovbd = _overlap_matrix(rk, n_pad)
    kern = functools.partial(_nsa_sample_a_kernel, n_cmp=n_cmp, n_sel=n_sel, q_pos=q_pos)
    return pl.pallas_call(
        kern,
        grid=(nb,),
        in_specs=[pl.BlockSpec((1, NSA_REP, LANES), lambda b: (b, 0, 0)),
                  pl.BlockSpec((1, rk, KV_W), lambda b: (b, 0, 0)),
                  pl.BlockSpec((1, rk, KV_W), lambda b: (b, 0, 0)),
                  pl.BlockSpec(ovbd.shape, lambda b: (0, 0))],
        out_specs=[pl.BlockSpec((1, 8, LANES), lambda b: (b, 0, 0)),
                   pl.BlockSpec((1, 8, LANES), lambda b: (b, 0, 0))],
        out_shape=[jax.ShapeDtypeStruct((nb, 8, LANES), F32), jax.ShapeDtypeStruct((nb, 8, LANES), jnp.int32)],
        compiler_params=_cparams(("arbitrary",)),
        name="nsa_sample_select",
    )(qraw3, kc, vc, ovbd)


N_SLOTS = NSA_GROUPS * SEL_TOPN


def _decode_attend(q8f, kts, vts, masks, s_new, inc_new, v_new):
    lo, hi = _lane_group_masks((8, KV_W))
    outs = []
    for g, lanes in enumerate((lo, hi)):
        qg = jnp.where(lanes, q8f, 0.0).astype(BF16)
        sg = jnp.dot(qg, kts[g], preferred_element_type=F32) * ATTN_SCALE
        sg = jnp.where(masks[g], sg, NEG_INF)
        sn = jnp.where(inc_new[g] > 0.5, s_new[g], NEG_INF)
        m = jnp.maximum(jnp.max(sg, axis=-1, keepdims=True), sn)
        m = jnp.where(m == NEG_INF, 0.0, m)
        p = jnp.exp(sg - m)
        pnew = jnp.exp(sn - m)
        l = jnp.sum(p, axis=-1, keepdims=True) + pnew
        og = _dot_nt(p.astype(BF16), vts[g]) + pnew * v_new
        outs.append(og / jnp.maximum(l, 1e-30))
    return jnp.where(lo, outs[0], outs[1])


def _nsa_sample_b_kernel(sel_ref, pt_ref, q_ref, gate_ref, oc_ref, info_ref, psk_ref, psv_ref,
                         sknew_ref, svnew_ref, wink_ref, winv_ref, wknew_ref, wvnew_ref, eexp_ref, e2_ref,
                         o_ref, swk_ref, swv_ref, kbuf, vbuf, sem, *, nb, n_pages, n_past_blk, wbuf):
    b = pl.program_id(0)
    blk_per_page = PAGE_SIZE // SEL_BLOCK

    def copies(bb, idx, slot):
        j = jnp.minimum(sel_ref[bb * N_SLOTS + idx], n_past_blk - 1)
        page = pt_ref[bb * n_pages + j // blk_per_page]
        return (pltpu.make_async_copy(psk_ref.at[page], kbuf.at[slot, idx], sem.at[slot]),
                pltpu.make_async_copy(psv_ref.at[page], vbuf.at[slot, idx], sem.at[slot]))

    def start_all(bb, slot):
        def body(idx, _):
            ck, cv = copies(bb, idx, slot)
            ck.start()
            cv.start()
            return 0
        lax.fori_loop(0, N_SLOTS, body, 0)

    @pl.when(b == 0)
    def _():
        start_all(0, 0)

    @pl.when(b + 1 < nb)
    def _():
        start_all(b + 1, (b + 1) % 2)

    slot = b % 2

    def wait_body(idx, _):
        ck, cv = copies(b, idx, slot)
        ck.wait()
        cv.wait()
        return 0
    lax.fori_loop(0, N_SLOTS, wait_body, 0)

    q8f = _pad_rows8(q_ref[0])
    lo, hi = _lane_group_masks((8, KV_W))

    def new_scores(k_new):
        prod = q8f * k_new
        return [jnp.sum(jnp.where(lo, prod, 0.0), axis=-1, keepdims=True) * ATTN_SCALE,
                jnp.sum(jnp.where(hi, prod, 0.0), axis=-1, keepdims=True) * ATTN_SCALE]

    info = info_ref[0].astype(F32)
    lane1 = lax.broadcasted_iota(jnp.int32, (1, LANES), 1)
    n_keys = SEL_TOPN * PAGE_SIZE
    tok_half = (lax.broadcasted_iota(jnp.int32, (8, n_keys), 1) % PAGE_SIZE) // SEL_BLOCK
    kts, vts, masks, inc = [], [], [], []
    for g in range(NSA_GROUPS):
        kts.append(jnp.concatenate([kbuf[slot, g * SEL_TOPN + k] for k in range(SEL_TOPN)], axis=1).astype(BF16))
        vts.append(jnp.concatenate([vbuf[slot, g * SEL_TOPN + k] for k in range(SEL_TOPN)], axis=1).astype(BF16))
        idx8 = jnp.concatenate([info[g:g + 1]] * 8, axis=0).astype(BF16)
        val8 = jnp.concatenate([info[2 + g:3 + g]] * 8, axis=0).astype(BF16)
        jl = jnp.dot(idx8, eexp_ref[...], preferred_element_type=F32)
        vl = jnp.dot(val8, eexp_ref[...], preferred_element_type=F32)
        masks.append((vl > 0.5) & (jl < n_past_blk - 0.5)
                     & (jl.astype(jnp.int32) % blk_per_page == tok_half))
        is_new = (info[g:g + 1] > n_past_blk - 0.5) & (info[2 + g:3 + g] > 0.5) & (lane1 < SEL_TOPN)
        inc.append(jnp.sum(jnp.where(is_new, 1.0, 0.0), axis=-1, keepdims=True))
    o_s = _decode_attend(q8f, kts, vts, masks, new_scores(sknew_ref[0]), inc, svnew_ref[0])

    wk_old = wink_ref[0]
    wv_old = winv_ref[0]
    widx = lax.broadcasted_iota(jnp.int32, (1, wbuf), 1)
    wmask = widx > wbuf - WINDOW
    always = [jnp.ones((1, 1), F32)] * NSA_GROUPS
    o_w = _decode_attend(q8f, [wk_old.astype(BF16)] * 2, [wv_old.astype(BF16)] * 2, [wmask] * 2,
                         new_scores(wknew_ref[0]), always, wvnew_ref[0])
    eye = (lax.broadcasted_iota(jnp.int32, (KV_W, KV_W), 0) == lax.broadcasted_iota(jnp.int32, (KV_W, KV_W), 1))
    for new_ref, old, out_ref in ((wknew_ref, wk_old, swk_ref), (wvnew_ref, wv_old, swv_ref)):
        col = jnp.sum(jnp.where(eye, new_ref[0], 0.0), axis=1, keepdims=True)
        out_ref[0] = jnp.where(widx == wbuf - 1, col, pltpu.roll(old, wbuf - 1, 1))

    gate8 = jnp.concatenate([gate_ref[0]] * 8, axis=0)
    rr = lax.broadcasted_iota(jnp.int32, (8, LANES), 0)
    ll = lax.broadcasted_iota(jnp.int32, (8, LANES), 1)
    o = jnp.zeros((8, KV_W), F32)
    for c, ob in enumerate((oc_ref[0], o_s, o_w)):
        base = c * NSA_HEADS + rr * NSA_GROUPS
        gsel = jnp.where((ll >= base) & (ll < base + NSA_GROUPS), gate8, 0.0)
        gh, gl = _split2(gsel)
        gx = (jnp.dot(gh, e2_ref[...], preferred_element_type=F32)
              + jnp.dot(gl, e2_ref[...], preferred_element_type=F32))
        o = o + gx * ob
    o_ref[0] = o[0:NSA_REP]


def _nsa_sample_b(sel_flat, pt_flat, qrot3, gate3, oc, info, pool_sk, pool_sv, sknew, svnew,
                  wink, winv, wknew, wvnew, nb, n_pages):
    wbuf = wink.shape[2]
    n_past_blk = n_pages * (PAGE_SIZE // SEL_BLOCK)
    e = np.zeros((LANES, SEL_TOPN * PAGE_SIZE), np.float32)
    for k in range(SEL_TOPN):
        e[k, k * PAGE_SIZE:(k + 1) * PAGE_SIZE] = 1.0
    eexp = jnp.asarray(e, BF16)
    e2 = np.zeros((LANES, LANES), np.float32)
    for j in range(LANES):
        e2[j, (j % NSA_GROUPS) * HEAD_DIM:(j % NSA_GROUPS + 1) * HEAD_DIM] = 1.0
    e2 = jnp.asarray(e2, BF16)
    per_b = lambda r, w: pl.BlockSpec((1, r, w), lambda b, s, p: (b, 0, 0))
    const = lambda a: pl.BlockSpec(a.shape, lambda b, s, p: (0, 0))
    anyspec = pl.BlockSpec(memory_space=pl.ANY)
    kern = functools.partial(_nsa_sample_b_kernel, nb=nb, n_pages=n_pages, n_past_blk=n_past_blk, wbuf=wbuf)
    r3 = lambda a: a.reshape(nb, 1, KV_W)
    return pl.pallas_call(
        kern,
        grid_spec=pltpu.PrefetchScalarGridSpec(
            num_scalar_prefetch=2,
            grid=(nb,),
            in_specs=[per_b(NSA_REP, LANES), per_b(1, LANES), per_b(8, LANES), per_b(8, LANES),
                      anyspec, anyspec, per_b(1, KV_W), per_b(1, KV_W),
                      per_b(KV_W, wbuf), per_b(KV_W, wbuf), per_b(1, KV_W), per_b(1, KV_W),
                      const(eexp), const(e2)],
            out_specs=[per_b(NSA_REP, LANES), per_b(KV_W, wbuf), per_b(KV_W, wbuf)],
            scratch_shapes=[pltpu.VMEM((2, N_SLOTS, KV_W, PAGE_SIZE), F32),
                            pltpu.VMEM((2, N_SLOTS, KV_W, PAGE_SIZE), F32),
                            pltpu.SemaphoreType.DMA((2,))]),
        out_shape=[jax.ShapeDtypeStruct((nb, NSA_REP, LANES), F32),
                   jax.ShapeDtypeStruct((nb, KV_W, wbuf), F32),
                   jax.ShapeDtypeStruct((nb, KV_W, wbuf), F32)],
        compiler_params=_cparams(("arbitrary",)),
        name="nsa_sample_attend",
    )(sel_flat, pt_flat, qrot3, gate3, oc, info, _pages_transposed(pool_sk), _pages_transposed(pool_sv),
      r3(sknew), r3(svnew), wink, winv, r3(wknew), r3(wvnew), eexp, e2)


DIFF_PAGES_PER_STEP = 8


def _diff_sample_kernel(pt_ref, lam_ref, dq_ref, dknew_ref, dvnew_ref, g_ref, pk_ref, pv_ref, o_ref,
                        kbuf, vbuf, sem, m_ref, l_ref, acc_ref, *, nb, n_pages, lam_init):
    b = pl.program_id(0)
    c = pl.program_id(1)
    nc = n_pages // DIFF_PAGES_PER_STEP
    step = b * nc + c

    def copies(st, p, slot):
        bb = st // nc
        cc = st % nc
        page = pt_ref[bb * n_pages + cc * DIFF_PAGES_PER_STEP + p]
        dst = pl.ds(p * PAGE_SIZE * DIFF_HEADS, PAGE_SIZE * DIFF_HEADS)
        return (pltpu.make_async_copy(pk_ref.at[page], kbuf.at[slot, p], sem.at[slot]),
                pltpu.make_async_copy(pv_ref.at[page], vbuf.at[slot, dst, :], sem.at[slot]))

    def start_all(st, slot):
        for p in range(DIFF_PAGES_PER_STEP):
            ck, cv = copies(st, p, slot)
            ck.start()
            cv.start()

    @pl.when(step == 0)
    def _():
        start_all(0, 0)

    @pl.when(step + 1 < nb * nc)
    def _():
        start_all(step + 1, (step + 1) % 2)

    slot = step % 2
    for p in range(DIFF_PAGES_PER_STEP):
        ck, cv = copies(step, p, slot)
        ck.wait()
        cv.wait()

    @pl.when(c == 0)
    def _():
        m_ref[...] = jnp.full(m_ref.shape, NEG_INF, F32)
        l_ref[...] = jnp.zeros(l_ref.shape, F32)
        acc_ref[...] = jnp.zeros(acc_ref.shape, F32)

    rr = lax.broadcasted_iota(jnp.int32, (8, DIFF_WIDTH), 0)
    ll = lax.broadcasted_iota(jnp.int32, (8, DIFF_WIDTH), 1)
    own = (ll // HEAD_DIM) == rr
    q8f = jnp.where(own, jnp.concatenate([dq_ref[0]] * 8, axis=0), 0.0)
    q8 = q8f.astype(BF16)
    s = jnp.concatenate([jnp.dot(q8, kbuf[slot, p].astype(BF16), preferred_element_type=F32)
                         for p in range(DIFF_PAGES_PER_STEP)], axis=1) * ATTN_SCALE
    m_old = m_ref[...]
    m_new = jnp.maximum(m_old, jnp.max(s, axis=-1, keepdims=True))
    alpha = jnp.exp(m_old - m_new)
    p = jnp.exp(s - m_new)
    l_ref[...] = alpha * l_ref[...] + jnp.sum(p, axis=-1, keepdims=True)
    m_ref[...] = m_new
    pb = p.astype(BF16)
    keys = DIFF_PAGES_PER_STEP * PAGE_SIZE
    pv = jnp.concatenate([jnp.dot(pb, vbuf[slot, pl.ds(h, keys, stride=DIFF_HEADS), :].astype(BF16),
                                  preferred_element_type=F32) for h in range(DIFF_HEADS)], axis=1)
    acc_ref[...] = alpha * acc_ref[...] + pv

    @pl.when(c == nc - 1)
    def _():
        s_new = jnp.sum(q8f * dknew_ref[0], axis=-1, keepdims=True) * ATTN_SCALE
        m_old = m_ref[...]
        m_new = jnp.maximum(m_old, s_new)
        alpha = jnp.exp(m_old - m_new)
        p_new = jnp.exp(s_new - m_new)
        l = alpha * l_ref[...] + p_new
        o = (alpha * acc_ref[...] + p_new * dvnew_ref[0]) / l
        lam = _lambda_value(lam_ref, lam_init)
        for h in range(DIFF_HEADS):
            cs = slice(h * DIFF_VDIM, (h + 1) * DIFF_VDIM)
            o_ref[0, :, cs] = _diff_merge_rows(o[2 * h:2 * h + 1, cs], o[2 * h + 1:2 * h + 2, cs], lam,
                                               g_ref[...], lam_init)


def _diff_sample(pt_flat, lam4, dq, dknew, dvnew, subln_g, pool_k, pool_v, nb, n_pages, lam_init):
    n_pool = pool_k.shape[0]
    nc = n_pages // DIFF_PAGES_PER_STEP
    keys = DIFF_PAGES_PER_STEP * PAGE_SIZE
    per_b = pl.BlockSpec((1, 1, DIFF_WIDTH), lambda b, c, pt: (b, 0, 0))
    anyspec = pl.BlockSpec(memory_space=pl.ANY)
    kern = functools.partial(_diff_sample_kernel, nb=nb, n_pages=n_pages, lam_init=lam_init)
    r3 = lambda a: a.reshape(nb, 1, DIFF_WIDTH)
    return pl.pallas_call(
        kern,
        grid_spec=pltpu.PrefetchScalarGridSpec(
            num_scalar_prefetch=1,
            grid=(nb, nc),
            in_specs=[pl.BlockSpec((4, HEAD_DIM), lambda b, c, pt: (0, 0)), per_b, per_b, per_b,
                      pl.BlockSpec((1, DIFF_VDIM), lambda b, c, pt: (0, 0)), anyspec, anyspec],
            out_specs=per_b,
            scratch_shapes=[pltpu.VMEM((2, DIFF_PAGES_PER_STEP, DIFF_WIDTH, PAGE_SIZE), F32),
                            pltpu.VMEM((2, keys * DIFF_HEADS, DIFF_VDIM), F32),
                            pltpu.SemaphoreType.DMA((2,)),
                            pltpu.VMEM((8, 1), F32), pltpu.VMEM((8, 1), F32), pltpu.VMEM((8, DIFF_WIDTH), F32)]),
        out_shape=jax.ShapeDtypeStruct((nb, 1, DIFF_WIDTH), F32),
        compiler_params=_cparams(("arbitrary", "arbitrary")),
        name="diff_sample",
    )(pt_flat, lam4, r3(dq), r3(dknew), r3(dvnew), subln_g,
      jnp.transpose(pool_k, (0, 2, 3, 4, 1)).reshape(n_pool, DIFF_WIDTH, PAGE_SIZE),
      pool_v.reshape(n_pool, PAGE_SIZE * DIFF_HEADS, DIFF_VDIM))


def _prep_w_in(w):
    splits = (NSA_WIDTH, KV_W, KV_W, KV_W, KV_W, KV_W, KV_W, NSA_HEADS * 3, DIFF_WIDTH, DIFF_WIDTH, DIFF_WIDTH)
    cuts = [int(c) for c in np.cumsum(splits)[:-1]]
    q, ck, cv, sk, sv, wk, wv, gate, dq, dk, dv = jnp.split(w, cuts, axis=1)
    q = q.reshape(D_MODEL, NSA_GROUPS, NSA_REP, HEAD_DIM).transpose(0, 2, 1, 3).reshape(D_MODEL, NSA_WIDTH)
    gate = gate.reshape(D_MODEL, NSA_GROUPS, NSA_REP, 3).transpose(0, 3, 2, 1).reshape(D_MODEL, NSA_HEADS * 3)
    gate = jnp.pad(gate, ((0, 0), (0, LANES - NSA_HEADS * 3)))
    return jnp.concatenate([q, ck, cv, sk, sv, wk, wv, dq, dk, dv, gate], axis=1).astype(BF16)


def _prep_w_out(w):
    won = w[:NSA_WIDTH].reshape(NSA_GROUPS, NSA_REP, HEAD_DIM, D_MODEL).transpose(1, 0, 2, 3)
    return won.reshape(NSA_WIDTH, D_MODEL).astype(BF16), w[NSA_WIDTH:].astype(BF16)


def _prep_ffn(p):
    won, wod = _prep_w_out(p["w_out"][0])
    return (won, wod, p["ffn_norm"], p["w_up"][0].astype(BF16), p["conv_w"][0], p["conv_b"],
            p["w_down"][0].astype(BF16), p["final_norm"].reshape(1, D_MODEL))


def _rope_tables(pos):
    half = HEAD_DIM // 2
    inv = 1.0 / (ROPE_THETA ** (jnp.arange(half, dtype=F32) / half))
    ang = pos.astype(F32)[:, None] * inv[None, :]
    cos, sin = jnp.cos(ang), jnp.sin(ang)
    return jnp.tile(cos, (1, 4)), jnp.tile(jnp.concatenate([-sin, sin], axis=1), (1, 2))


def _prompt_group(xp, attn_norm, w_in_r, cw_k, cw_v, lam4, subln_g, ffn_ws, lam_init):
    B, T, _ = xp.shape
    x2d = xp.reshape(B * T, D_MODEL)
    tm = min(256, T)
    cos_t, sin_t = _rope_tables(jnp.arange(T, dtype=jnp.int32))
    (qraw, qrot, ck, cv, sk, sv, wk, wv, gate, dq, dk, dv,
     ck_t, cv_t, sk_t, sv_t, wk_t, wv_t, dk_t) = _project(x2d, attn_norm, w_in_r, cos_t, sin_t, tm,
                                                          transposed_batch=B)
    kc = _compress_prompt(ck.reshape(B, T, KV_W), cw_k)
    vc = _compress_prompt(cv.reshape(B, T, KV_W), cw_v)
    o_nsa = _nsa_prompt(qraw, qrot, gate, kc, vc, sk, sv, wk, wv, B, T, tq=256, tk=256)
    o_diff = _diff_prompt(lam4, dq, dk, dv, subln_g, B, T, tq=min(512, T), tk=512, lam_init=lam_init)
    y, conv = _ffn_seq(x2d, o_nsa, o_diff, ffn_ws, B, T, tm)
    kv = lambda a: jnp.transpose(a.reshape(B, NSA_GROUPS, HEAD_DIM, a.shape[2]), (0, 3, 1, 2))[None]
    wb = min(WINDOW, T)
    dk_out = jnp.transpose(dk_t.reshape(B, DIFF_HEADS, 2, HEAD_DIM, T), (0, 4, 1, 2, 3))[None]
    return (y.reshape(B, T, D_MODEL), kv(ck_t), kv(cv_t), kv(sk_t), kv(sv_t),
            dk_out, dv.reshape(1, B, T, DIFF_HEADS, DIFF_VDIM),
            kv(wk_t[:, :, T - wb:]), kv(wv_t[:, :, T - wb:]), conv[None])


def _sample_group(xs, caches, page_table, attn_norm, w_in_r, cw_k, cw_v, lam4, subln_g, ffn_ws, lam_init):
    pool_ck, pool_cv, pool_sk, pool_sv, pool_dk, pool_dv, win_k, win_v, conv_state = caches
    nb = xs.shape[0]
    n_pages = page_table.shape[1]
    past = n_pages * PAGE_SIZE
    t_pad = -(-(past + 1) // SEL_BLOCK) * SEL_BLOCK
    n_cmp = t_pad // CMP_STRIDE - CMP_BLOCK // CMP_STRIDE + 1
    n_sel = t_pad // SEL_BLOCK
    rows = -(-(t_pad // CMP_STRIDE) // 8) * 8
    rows_out = -(-rows // LANES) * LANES
    pt_flat = page_table.reshape(-1)
    x2d = xs.reshape(nb, D_MODEL)
    pos = jnp.full((nb,), past, dtype=jnp.int32)
    cos_t, sin_t = _rope_tables(pos)
    qraw, qrot, ck, cv, sk, sv, wk, wv, gate, dq, dk, dv = _project(x2d, attn_norm, w_in_r, cos_t, sin_t, nb)
    kc = _compress_sample(pool_ck, pt_flat, ck, cw_k, nb, n_pages, rows, rows_out)
    vc = _compress_sample(pool_cv, pt_flat, cv, cw_v, nb, n_pages, rows, rows_out)
    o_c, info = _nsa_sample_a(qraw.reshape(nb, NSA_REP, LANES), kc, vc, past, n_cmp, n_sel)
    sel_flat = info[:, 0:NSA_GROUPS, 0:SEL_TOPN].reshape(-1)
    wbuf = win_k.shape[1]
    o_nsa, swk, swv = _nsa_sample_b(sel_flat, pt_flat, qrot.reshape(nb, NSA_REP, LANES),
                                    gate.reshape(nb, 1, LANES), o_c, info, pool_sk, pool_sv, sk, sv,
                                    _pages_transposed(win_k), _pages_transposed(win_v), wk, wv, nb, n_pages)
    from_t = lambda a: jnp.transpose(a.reshape(nb, NSA_GROUPS, HEAD_DIM, wbuf), (0, 3, 1, 2))[None]
    o_diff = _diff_sample(pt_flat, lam4, dq, dk, dv, subln_g, pool_dk, pool_dv, nb, n_pages, lam_init)
    y, u = _ffn_step(x2d, o_nsa.reshape(nb, NSA_WIDTH), o_diff.reshape(nb, DIFF_WIDTH), ffn_ws,
                     conv_state[:, 0], conv_state[:, 1])
    kv = lambda a: a.reshape(1, nb, 1, NSA_GROUPS, HEAD_DIM)
    return (y.reshape(nb, 1, D_MODEL), kv(ck), kv(cv), kv(sk), kv(sv),
            dk.reshape(1, nb, 1, DIFF_HEADS, 2, HEAD_DIM), dv.reshape(1, nb, 1, DIFF_HEADS, DIFF_VDIM),
            from_t(swk), from_t(swv),
            jnp.stack([conv_state[:, 1], u], axis=1)[None])


def kernel(x_prompt, x_sample, cache_cmp_k, cache_cmp_v, cache_sel_k, cache_sel_v, cache_diff_k, cache_diff_v,
           cache_win_k, cache_win_v, state_ffn_conv, page_table, attn_norm, w_in, cmp_pos_k, cmp_w1_k, cmp_w2_k,
           cmp_pos_v, cmp_w1_v, cmp_w2_v, lambda_q1, lambda_k1, lambda_q2, lambda_k2, subln_g, w_out, ffn_norm,
           w_up, conv_w, conv_b, w_down, final_norm):
    assert w_in.shape[0] == 1 and x_sample.shape[1] == 1, "one layer, one new token per sequence"
    lam_init = 0.8 - 0.6 * math.exp(0.0)
    w_in_r = _prep_w_in(w_in[0])
    cw_k = _compress_weights(cmp_pos_k[0], cmp_w1_k[0], cmp_w2_k[0])
    cw_v = _compress_weights(cmp_pos_v[0], cmp_w1_v[0], cmp_w2_v[0])
    lam4 = jnp.stack([lambda_q1[0], lambda_k1[0], lambda_q2[0], lambda_k2[0]])
    ffn_ws = _prep_ffn(dict(w_out=w_out, ffn_norm=ffn_norm, w_up=w_up, conv_w=conv_w, conv_b=conv_b,
                            w_down=w_down, final_norm=final_norm))
    p = _prompt_group(x_prompt, attn_norm, w_in_r, cw_k, cw_v, lam4, subln_g, ffn_ws, lam_init)
    caches = (cache_cmp_k[0], cache_cmp_v[0], cache_sel_k[0], cache_sel_v[0], cache_diff_k[0], cache_diff_v[0],
              cache_win_k[0], cache_win_v[0], state_ffn_conv[0])
    s = _sample_group(x_sample, caches, page_table, attn_norm, w_in_r, cw_k, cw_v, lam4, subln_g, ffn_ws,
                      lam_init)
    return (p[0], s[0]) + tuple(p[1:]) + tuple(s[1:])
```

```python
import functools
import math

import numpy as np
import jax
import jax.numpy as jnp
from jax import lax
from jax.experimental import pallas as pl
from jax.experimental.pallas import tpu as pltpu

F32 = jnp.float32
BF16 = jnp.bfloat16

D_MODEL = 1024
HEAD_DIM = 64
NSA_WIDTH = 512
NSA_HEADS = 8
NSA_GROUPS = 2
NSA_REP = 4
KV_W = NSA_GROUPS * HEAD_DIM
CMP_BLOCK = 32
CMP_STRIDE = 16
CMP_HIDDEN = 128
SEL_BLOCK = 64
SEL_TOPN = 16
WINDOW = 512
FORCE_BONUS = 1.0e4
DIFF_WIDTH = 512
DIFF_VDIM = 128
DIFF_HEADS = 4
D_FF = 2816
CONV_W = 3
ROPE_THETA = 10000.0
RMS_EPS = 1e-6
ATTN_SCALE = 1.0 / math.sqrt(HEAD_DIM)
SCALE_LOG2E = ATTN_SCALE * math.log2(math.e)
PAGE_SIZE = 128

LANES = 128
VMEM_LIMIT = 48 * 1024 * 1024
NEG_INF = float("-inf")

C_Q = 0
C_CK, C_CV, C_SK, C_SV, C_WK, C_WV = 512, 640, 768, 896, 1024, 1152
C_DQ, C_DK, C_DV = 1280, 1792, 2304
C_GATE = 2816
W_IN_COLS = 2944


def _cparams(sem):
    return pltpu.CompilerParams(dimension_semantics=sem, vmem_limit_bytes=VMEM_LIMIT)


def _rope128(v, cos, sin_signed):
    lane = lax.broadcasted_iota(jnp.int32, v.shape, 1)
    first = (lane % HEAD_DIM) < (HEAD_DIM // 2)
    partner = jnp.where(first, pltpu.roll(v, LANES - HEAD_DIM // 2, 1), pltpu.roll(v, HEAD_DIM // 2, 1))
    return v * cos + partner * sin_signed


def _proj_kernel(x_ref, g_ref, w_ref, cos_ref, sin_ref,
                 qraw_ref, qrot_ref, ck_ref, cv_ref, sk_ref, sv_ref, wk_ref, wv_ref,
                 gate_ref, dq_ref, dk_ref, dv_ref, *t_refs):
    x = x_ref[...]
    h = x * lax.rsqrt(jnp.mean(x * x, axis=-1, keepdims=True) + RMS_EPS) * g_ref[...]
    z = jnp.dot(h.astype(BF16), w_ref[...], preferred_element_type=F32)
    cos = cos_ref[...]
    sin = sin_ref[...]

    def put(ref, col, width, rope, t_ref=None):
        for c in range(width // LANES):
            v = z[:, col + c * LANES: col + (c + 1) * LANES]
            v = _rope128(v, cos, sin) if rope else v
            ref[:, c * LANES:(c + 1) * LANES] = v
            if t_ref is not None:
                t_ref[0, c * LANES:(c + 1) * LANES, :] = v.T

    tr = list(t_refs) if t_refs else [None] * 7
    put(qraw_ref, C_Q, NSA_WIDTH, False)
    put(qrot_ref, C_Q, NSA_WIDTH, True)
    put(ck_ref, C_CK, KV_W, False, tr[0])
    put(cv_ref, C_CV, KV_W, False, tr[1])
    put(sk_ref, C_SK, KV_W, True, tr[2])
    put(sv_ref, C_SV, KV_W, False, tr[3])
    put(wk_ref, C_WK, KV_W, True, tr[4])
    put(wv_ref, C_WV, KV_W, False, tr[5])
    put(dq_ref, C_DQ, DIFF_WIDTH, True)
    put(dk_ref, C_DK, DIFF_WIDTH, True, tr[6])
    put(dv_ref, C_DV, DIFF_WIDTH, False)
    gate_ref[...] = jax.nn.sigmoid(z[:, C_GATE:C_GATE + LANES])


def _project(x2d, norm_g, w_in_r, cos_t, sin_t, tm, transposed_batch=0):
    M = x2d.shape[0]
    Tt = cos_t.shape[0]
    nt = Tt // tm
    widths = (NSA_WIDTH, NSA_WIDTH, KV_W, KV_W, KV_W, KV_W, KV_W, KV_W, LANES,
              DIFF_WIDTH, DIFF_WIDTH, DIFF_WIDTH)
    row = lambda i: (i, 0)
    out_specs = [pl.BlockSpec((tm, w), row) for w in widths]
    out_shape = [jax.ShapeDtypeStruct((M, w), F32) for w in widths]
    if transposed_batch:
        for w in (KV_W,) * 6 + (DIFF_WIDTH,):
            out_specs.append(pl.BlockSpec((1, w, tm), lambda i: (i // nt, 0, i % nt)))
            out_shape.append(jax.ShapeDtypeStruct((transposed_batch, w, Tt), F32))
    return pl.pallas_call(
        _proj_kernel,
        grid=(M // tm,),
        in_specs=[pl.BlockSpec((tm, D_MODEL), row),
                  pl.BlockSpec((1, D_MODEL), lambda i: (0, 0)),
                  pl.BlockSpec((D_MODEL, W_IN_COLS), lambda i: (0, 0)),
                  pl.BlockSpec((tm, LANES), lambda i: (i % nt, 0)),
                  pl.BlockSpec((tm, LANES), lambda i: (i % nt, 0))],
        out_specs=out_specs,
        out_shape=out_shape,
        compiler_params=_cparams(("arbitrary",)),
        name="in_proj",
    )(x2d, norm_g, w_in_r, cos_t, sin_t)


def _gelu(x):
    return 0.5 * x * (1.0 + jnp.tanh(math.sqrt(2.0 / math.pi) * (x + 0.044715 * (x * x * x))))


def _compress_rows(c, w1cat, posb, w1n, w2bd):
    r = jnp.dot(c.astype(BF16), w1cat, preferred_element_type=F32)
    return _compress_finish(r, posb, w1n, w2bd)


def _compress_finish(r, posb, w1n, w2bd):
    R = r.shape[0]
    bias = jnp.dot(posb, w1n, preferred_element_type=F32)[0:1]
    bias2 = jnp.concatenate([bias, bias], axis=1)
    hid = r[:, :2 * CMP_HIDDEN] + pltpu.roll(r[:, 2 * CMP_HIDDEN:], R - 1, 0) + bias2
    return jnp.dot(_gelu(hid).astype(BF16), w2bd, preferred_element_type=F32)


def _compress_kernel(c_ref, w1cat_ref, posb_ref, w1n_ref, w2bd_ref, o_ref):
    o_ref[0] = _compress_rows(c_ref[0], w1cat_ref[...], posb_ref[...], w1n_ref[...], w2bd_ref[...])


def _compress_weights(pos, w1, w2):
    w1r = w1.reshape(2, CMP_STRIDE, HEAD_DIM, CMP_HIDDEN)
    eye = jnp.eye(NSA_GROUPS, dtype=w1.dtype)
    big = jnp.einsum('isdh,gk->isgdkh', w1r, eye).reshape(2, CMP_STRIDE * KV_W, NSA_GROUPS * CMP_HIDDEN)
    w1cat = jnp.concatenate([big[0], big[1]], axis=1).astype(BF16)
    w2bd = jnp.einsum('hd,gk->ghkd', w2, eye).reshape(NSA_GROUPS * CMP_HIDDEN, KV_W).astype(BF16)
    posb = jnp.broadcast_to(pos.reshape(1, CMP_BLOCK * HEAD_DIM), (8, CMP_BLOCK * HEAD_DIM)).astype(BF16)
    return w1cat, posb, w1.astype(BF16), w2bd


def _compress_prompt(x, cw):
    B, T, _ = x.shape
    R = T // CMP_STRIDE
    c = x.reshape(B, R, CMP_STRIDE * KV_W)
    full = lambda a: pl.BlockSpec(a.shape, lambda b: (0,) * a.ndim)
    return pl.pallas_call(
        _compress_kernel,
        grid=(B,),
        in_specs=[pl.BlockSpec((1, R, CMP_STRIDE * KV_W), lambda b: (b, 0, 0))] + [full(a) for a in cw],
        out_specs=pl.BlockSpec((1, R, KV_W), lambda b: (b, 0, 0)),
        out_shape=jax.ShapeDtypeStruct((B, R, KV_W), F32),
        compiler_params=_cparams(("arbitrary",)),
        name="compress_prompt",
    )(c, *cw)


def _lane_group_masks(shape):
    lane = lax.broadcasted_iota(jnp.int32, shape, 1)
    lo = lane < HEAD_DIM
    return lo, jnp.logical_not(lo)


def _block_diag_rows(x, dtype):
    lo, hi = _lane_group_masks(x.shape)
    return jnp.concatenate([jnp.where(lo, x, 0.0), jnp.where(hi, x, 0.0)], axis=0).astype(dtype)


def _dot_nt(a, b):
    return lax.dot_general(a, b, (((1,), (1,)), ((), ())), preferred_element_type=F32)


SUM_ROWS = 16


def _values_t(v_bd):
    n = v_bd.shape[0] // NSA_GROUPS
    row = lax.broadcasted_iota(jnp.int32, (SUM_ROWS, v_bd.shape[0]), 0)
    col = lax.broadcasted_iota(jnp.int32, (SUM_ROWS, v_bd.shape[0]), 1)
    ones = jnp.where(row == col // n, 1.0, 0.0)
    return jnp.concatenate([v_bd.T, ones], axis=0).astype(BF16)


def _flash_step_t(s_t, mask_t, m_ref, acc_ref, v_t, tk):
    ps, alphas = [], []
    for g in range(NSA_GROUPS):
        sg = s_t[g * tk:(g + 1) * tk]
        if mask_t is not None:
            sg = jnp.where(mask_t, sg, NEG_INF)
        m_old = m_ref[g]
        m_new = jnp.maximum(m_old, jnp.max(sg, axis=0, keepdims=True))
        m_safe = jnp.where(m_new == NEG_INF, 0.0, m_new)
        alphas.append(jnp.exp2(m_old - m_safe))
        ps.append(jnp.exp2(sg - m_safe).astype(BF16))
        m_ref[g] = m_new
        rs = slice(g * HEAD_DIM, (g + 1) * HEAD_DIM)
        acc_ref[rs, :] = alphas[g] * acc_ref[rs, :]
    srow = lax.broadcasted_iota(jnp.int32, (SUM_ROWS, 1), 0)
    acc_ref[KV_W:, :] = jnp.where(srow == 0, alphas[0], alphas[1]) * acc_ref[KV_W:, :]
    acc_ref[...] = acc_ref[...] + jnp.dot(v_t, jnp.concatenate(ps, axis=0), preferred_element_type=F32)


def _finish_t(acc_ref):
    parts = [acc_ref[g * HEAD_DIM:(g + 1) * HEAD_DIM, :]
             * (1.0 / jnp.maximum(acc_ref[KV_W + g:KV_W + g + 1, :], 1e-30)) for g in range(NSA_GROUPS)]
    return jnp.concatenate(parts, axis=0)


def _reset(m_ref, acc_ref):
    m_ref[...] = jnp.full(m_ref.shape, NEG_INF, F32)
    acc_ref[...] = jnp.zeros(acc_ref.shape, F32)


def _queries_t(q, tq):
    return jnp.concatenate([(q[:, r * LANES:(r + 1) * LANES] * SCALE_LOG2E).T for r in range(NSA_REP)],
                           axis=1).astype(BF16)


def _outputs_from_t(o_t, tq):
    return jnp.concatenate([o_t[:, r * tq:(r + 1) * tq].T for r in range(NSA_REP)], axis=1)


def _split2(x):
    hi = x.astype(BF16)
    return hi, (x - hi.astype(F32)).astype(BF16)


def _select_blocks(score_t, q_pos_row, n_sel):
    n_pad = score_t.shape[0] // NSA_GROUPS
    tq = score_t.shape[1]
    out = []
    blk = lax.broadcasted_iota(jnp.int32, (n_pad, tq), 0)
    cur = q_pos_row // SEL_BLOCK
    forced = (blk == 0) | (blk == cur) | (blk == cur - 1)
    ok = (blk * SEL_BLOCK <= q_pos_row) & (blk < n_sel)
    for g in range(NSA_GROUPS):
        sc = score_t[g * n_pad:(g + 1) * n_pad]
        sc = jnp.where(ok, sc + jnp.where(forced, FORCE_BONUS, 0.0), NEG_INF)
        n_tiles = n_pad // 8
        tiles = [sc[8 * t:8 * t + 8] for t in range(n_tiles)]
        cnts = [jnp.zeros((8, tq), F32) for _ in range(n_tiles)]
        sub = lax.broadcasted_iota(jnp.int32, (8, tq), 0)
        for i in range(n_pad):
            row = sc[i:i + 1]
            ti, ri = divmod(i, 8)
            for t in range(n_tiles):
                if t > ti:
                    ahead = jnp.where(row >= tiles[t], 1.0, 0.0)
                elif t < ti:
                    ahead = jnp.where(row > tiles[t], 1.0, 0.0)
                else:
                    ahead = jnp.where(sub > ri, jnp.where(row >= tiles[t], 1.0, 0.0),
                                      jnp.where(row > tiles[t], 1.0, 0.0))
                cnts[t] = cnts[t] + ahead
        cnt = jnp.concatenate(cnts, axis=0)
        out.append(jnp.where((cnt < float(SEL_TOPN)) & ok, 1.0, 0.0))
    return jnp.concatenate(out, axis=0)


def _nsa_prompt_kernel(qraw_ref, qrot_ref, gate_ref, kc_ref, vc_ref, sk_ref, sv_ref, wk_ref, wv_ref,
                       ovt_ref, gexp_ref, o_ref,
                       skbd, svt, wkbd, wvt, kcbd, vct, sel_ref, m_ref, acc_ref, s0_ref, s1_ref, ow_ref,
                       *, tq, tk, seq):
    i = pl.program_id(1)
    n_kt = seq // tk
    n_sel = seq // SEL_BLOCK
    n_cmp = kc_ref.shape[1]
    n_pad = sel_ref.shape[0] // NSA_GROUPS
    L = NSA_REP * tq
    blk_per_tile = tk // SEL_BLOCK

    @pl.when(i == 0)
    def _():
        def fill(j, _):
            rows = pl.ds(pl.multiple_of(j * tk, tk), tk)
            skbd[j] = _block_diag_rows(sk_ref[0, rows, :], BF16)
            wkbd[j] = _block_diag_rows(wk_ref[0, rows, :], BF16)
            svt[j] = _values_t(_block_diag_rows(sv_ref[0, rows, :], F32))
            wvt[j] = _values_t(_block_diag_rows(wv_ref[0, rows, :], F32))
            return 0
        lax.fori_loop(0, n_kt, fill, 0)
        kcbd[...] = _block_diag_rows(kc_ref[0], BF16)
        vct[...] = _block_diag_rows(vc_ref[0], F32).T.astype(BF16)

    q0 = i * tq
    q_pos_row = q0 + lax.broadcasted_iota(jnp.int32, (1, tq), 1)
    q_pos = jnp.concatenate([q_pos_row] * NSA_REP, axis=1)

    qr = _queries_t(qrot_ref[...], tq)
    kiota = lax.broadcasted_iota(jnp.int32, (tk, 1), 0)
    n_full = q0 // tk

    _reset(m_ref, acc_ref)
    n_win = (WINDOW + tk - 2) // tk + 1

    def win_scores(w):
        return jnp.dot(wkbd[jnp.maximum(n_full - w, 0)], qr, preferred_element_type=F32)

    def win_consume(w, s):
        jw = n_full - w
        kpos = jnp.where(jw >= 0, jw, 2 * n_kt) * tk + kiota
        mask = (kpos <= q_pos) & (kpos > q_pos - WINDOW)
        _flash_step_t(s, mask, m_ref, acc_ref, wvt[jnp.maximum(jw, 0)], tk)

    bufs = (s0_ref, s1_ref)
    bufs[0][...] = win_scores(n_win - 1)
    for k, w in enumerate(range(n_win - 1, -1, -1)):
        if w > 0:
            bufs[(k + 1) % 2][...] = win_scores(w - 1)
        win_consume(w, bufs[k % 2][...])
    ow_ref[...] = _finish_t(acc_ref)

    s = jnp.dot(kcbd[...], _queries_t(qraw_ref[...], tq), preferred_element_type=F32)
    kidx = lax.broadcasted_iota(jnp.int32, (n_cmp, 1), 0)
    cmask = (kidx * CMP_STRIDE + (CMP_BLOCK - 1) <= q_pos) & (kidx < n_cmp - 1)
    pcs = []
    for g in range(NSA_GROUPS):
        sg = jnp.where(cmask, s[g * n_cmp:(g + 1) * n_cmp], NEG_INF)
        m = jnp.max(sg, axis=0, keepdims=True)
        m = jnp.where(m == NEG_INF, 0.0, m)
        p = jnp.exp2(sg - m)
        pcs.append(p * (1.0 / jnp.maximum(jnp.sum(p, axis=0, keepdims=True), 1e-30)))
    pc = jnp.concatenate(pcs, axis=0)
    o_c = jnp.dot(vct[...], pc.astype(BF16), preferred_element_type=F32)
    imp = pc[:, 0:tq] + pc[:, tq:2 * tq] + pc[:, 2 * tq:3 * tq] + pc[:, 3 * tq:4 * tq]
    ih, il = _split2(imp)
    score = (jnp.dot(ovt_ref[...], ih, preferred_element_type=F32)
             + jnp.dot(ovt_ref[...], il, preferred_element_type=F32))
    sel = _select_blocks(score, q_pos_row, n_sel)
    for jt in range(n_pad // blk_per_tile):
        for g in range(NSA_GROUPS):
            src = g * n_pad + jt * blk_per_tile
            dst = (jt * NSA_GROUPS + g) * blk_per_tile
            sel_ref[dst:dst + blk_per_tile, :] = sel[src:src + blk_per_tile]

    n_bias = NSA_GROUPS * blk_per_tile

    def sel_scores(j):
        return jnp.dot(skbd[j], qr, preferred_element_type=F32)

    def sel_consume(j, s, causal):
        sel8 = sel_ref[pl.ds(pl.multiple_of(j * n_bias, n_bias), n_bias), :]
        parts = [jnp.broadcast_to(sel8[r:r + 1], (SEL_BLOCK, tq)) for r in range(n_bias)]
        m1 = jnp.concatenate(parts, axis=0) > 0.5
        s = jnp.where(jnp.concatenate([m1] * NSA_REP, axis=1), s, NEG_INF)
        mask = ((j * tk + kiota) <= q_pos) if causal else None
        _flash_step_t(s, mask, m_ref, acc_ref, svt[j], tk)

    _reset(m_ref, acc_ref)
    s0_ref[...] = sel_scores(0)

    def pair(t, _):
        j = 2 * t
        s1_ref[...] = sel_scores(j + 1)
        sel_consume(j, s0_ref[...], False)
        s0_ref[...] = sel_scores(j + 2)
        sel_consume(j + 1, s1_ref[...], False)
        return 0

    lax.fori_loop(0, n_full // 2, pair, 0)

    @pl.when(n_full % 2 == 1)
    def _():
        s1_ref[...] = sel_scores(n_full)
        sel_consume(n_full - 1, s0_ref[...], False)
        sel_consume(n_full, s1_ref[...], True)

    @pl.when(n_full % 2 == 0)
    def _():
        sel_consume(n_full, s0_ref[...], True)
    o_s = _finish_t(acc_ref)

    gh, gl = _split2(gate_ref[...])
    gx = (jnp.dot(gh, gexp_ref[...], preferred_element_type=F32)
          + jnp.dot(gl, gexp_ref[...], preferred_element_type=F32))
    o = (gx[:, 0:NSA_WIDTH] * _outputs_from_t(o_c, tq)
         + gx[:, NSA_WIDTH:2 * NSA_WIDTH] * _outputs_from_t(o_s, tq)
         + gx[:, 2 * NSA_WIDTH:3 * NSA_WIDTH] * _outputs_from_t(ow_ref[...], tq))
    o_ref[...] = o


def _overlap_matrix(n_cmp_rows, n_sel_pad):
    cs = np.arange(n_cmp_rows)[:, None] * CMP_STRIDE
    ss = np.arange(n_sel_pad)[None, :] * SEL_BLOCK
    ov = ((cs < ss + SEL_BLOCK) & (cs + CMP_BLOCK > ss)).astype(np.float32)
    z = np.zeros_like(ov)
    return jnp.asarray(np.block([[ov, z], [z, ov]]), BF16)


def _gate_expand_matrix():
    e = np.zeros((LANES, 3 * NSA_WIDTH), np.float32)
    for c in range(3):
        for r in range(NSA_REP):
            for g in range(NSA_GROUPS):
                col = c * NSA_WIDTH + r * LANES + g * HEAD_DIM
                e[c * NSA_HEADS + r * NSA_GROUPS + g, col:col + HEAD_DIM] = 1.0
    return jnp.asarray(e, BF16)


def _nsa_prompt(qraw, qrot, gate, kc, vc, sk, sv, wk, wv, B, T, tq, tk):
    nq = T // tq
    n_cmp = kc.shape[1]
    n_sel_pad = max(T // SEL_BLOCK, HEAD_DIM)
    assert tq == tk, "the kernel's tile schedule has exactly one diagonal tile per query tile"
    ovt = _overlap_matrix(n_cmp, n_sel_pad).T
    gexp = _gate_expand_matrix()
    qspec = lambda w: pl.BlockSpec((tq, w), lambda b, i: (b * nq + i, 0))
    kvspec = lambda n: pl.BlockSpec((1, n, KV_W), lambda b, i: (b, 0, 0))
    const = lambda a: pl.BlockSpec(a.shape, lambda b, i: (0, 0))
    L = NSA_REP * tq
    n_kt = T // tk
    kern = functools.partial(_nsa_prompt_kernel, tq=tq, tk=tk, seq=T)
    kv3 = lambda a: a.reshape(B, T, KV_W)
    return pl.pallas_call(
        kern,
        grid=(B, nq),
        in_specs=[qspec(NSA_WIDTH), qspec(NSA_WIDTH), qspec(LANES), kvspec(n_cmp), kvspec(n_cmp),
                  kvspec(T), kvspec(T), kvspec(T), kvspec(T), const(ovt), const(gexp)],
        out_specs=qspec(NSA_WIDTH),
        out_shape=jax.ShapeDtypeStruct((B * T, NSA_WIDTH), F32),
        scratch_shapes=[pltpu.VMEM((n_kt, 2 * tk, KV_W), BF16),
                        pltpu.VMEM((n_kt, KV_W + SUM_ROWS, 2 * tk), BF16),
                        pltpu.VMEM((n_kt, 2 * tk, KV_W), BF16),
                        pltpu.VMEM((n_kt, KV_W + SUM_ROWS, 2 * tk), BF16),
                        pltpu.VMEM((2 * n_cmp, KV_W), BF16), pltpu.VMEM((KV_W, 2 * n_cmp), BF16),
                        pltpu.VMEM((NSA_GROUPS * n_sel_pad, tq), F32),
                        pltpu.VMEM((NSA_GROUPS, 1, L), F32),
                        pltpu.VMEM((KV_W + SUM_ROWS, L), F32),
                        pltpu.VMEM((2 * tk, L), F32), pltpu.VMEM((2 * tk, L), F32),
                        pltpu.VMEM((KV_W, L), F32)],
        compiler_params=_cparams(("arbitrary", "arbitrary")),
        name="nsa_prompt",
    )(qraw, qrot, gate, kc, vc, kv3(sk), kv3(sv), kv3(wk), kv3(wv), ovt, gexp)


def _lambda_value(lam_ref, lam_init):
    a = jnp.sum(lam_ref[0:1] * lam_ref[1:2], axis=-1, keepdims=True)
    b = jnp.sum(lam_ref[2:3] * lam_ref[3:4], axis=-1, keepdims=True)
    return jnp.exp(a) - jnp.exp(b) + lam_init


def _diff_merge_rows(o0, o1, lam, g, lam_init):
    od = o0 - lam * o1
    od = od * lax.rsqrt(jnp.mean(od * od, axis=-1, keepdims=True) + RMS_EPS) * g
    return od * (1.0 - lam_init)


def _diff_prompt_kernel(lam_ref, dq_ref, dk_ref, dv_ref, gcol_ref, o_ref,
                        kbf, vbf, m_ref, acc_ref, s0_ref, s1_ref, *, tq, tk, lam_init):
    i = pl.program_id(2)
    n_kt = kbf.shape[0] // tk

    @pl.when(i == 0)
    def _():
        kbf[...] = dk_ref[...].astype(BF16)
        ones = jnp.where(lax.broadcasted_iota(jnp.int32, (SUM_ROWS, tk), 0) == 0, 1.0, 0.0)

        def fill(j, _):
            vt = dv_ref[pl.ds(pl.multiple_of(j * tk, tk), tk), :].T
            vbf[j] = jnp.concatenate([vt, ones], axis=0).astype(BF16)
            return 0
        lax.fori_loop(0, n_kt, fill, 0)

    q = dq_ref[...] * SCALE_LOG2E
    lo, hi = _lane_group_masks(q.shape)
    q2t = jnp.concatenate([jnp.where(lo, q, 0.0), jnp.where(hi, q, 0.0)], axis=0).T.astype(BF16)
    q0 = i * tq
    q_pos = q0 + lax.broadcasted_iota(jnp.int32, (1, 2 * tq), 1) % tq
    kiota = lax.broadcasted_iota(jnp.int32, (tk, 1), 0)
    m_ref[...] = jnp.full(m_ref.shape, NEG_INF, F32)
    acc_ref[...] = jnp.zeros(acc_ref.shape, F32)

    def scores(j):
        k0 = pl.multiple_of(j * tk, tk)
        return jnp.dot(kbf[pl.ds(k0, tk), :], q2t, preferred_element_type=F32)

    def consume(j, s, causal):
        if causal:
            s = jnp.where((j * tk + kiota) <= q_pos, s, NEG_INF)
        m_old = m_ref[...]
        m_new = jnp.maximum(m_old, jnp.max(s, axis=0, keepdims=True))
        alpha = jnp.exp2(m_old - m_new)
        p = jnp.exp2(s - m_new).astype(BF16)
        m_ref[...] = m_new
        acc_ref[...] = alpha * acc_ref[...] + jnp.dot(vbf[j], p, preferred_element_type=F32)

    n_full = q0 // tk
    s0_ref[...] = scores(0)

    def pair(t, _):
        j = 2 * t
        s1_ref[...] = scores(j + 1)
        consume(j, s0_ref[...], False)
        s0_ref[...] = scores(j + 2)
        consume(j + 1, s1_ref[...], False)
        return 0

    lax.fori_loop(0, n_full // 2, pair, 0)

    @pl.when(n_full % 2 == 1)
    def _():
        s1_ref[...] = scores(n_full)
        consume(n_full - 1, s0_ref[...], False)
        consume(n_full, s1_ref[...], True)

    @pl.when(n_full % 2 == 0)
    def _():
        consume(n_full, s0_ref[...], True)
    o = acc_ref[0:DIFF_VDIM, :] * (1.0 / acc_ref[DIFF_VDIM:DIFF_VDIM + 1, :])
    lam = _lambda_value(lam_ref, lam_init)
    od = o[:, :tq] - lam * o[:, tq:]
    od = od * lax.rsqrt(jnp.mean(od * od, axis=0, keepdims=True) + RMS_EPS) * gcol_ref[...]
    o_ref[...] = (od * (1.0 - lam_init)).T


def _diff_prompt(lam4, dq, dk, dv, subln_g, B, T, tq, tk, lam_init):
    assert tq == tk, "the kernel's tile schedule has exactly one diagonal tile per query tile"
    nq = T // tq
    kern = functools.partial(_diff_prompt_kernel, tq=tq, tk=tk, lam_init=lam_init)
    return pl.pallas_call(
        kern,
        grid=(B, DIFF_HEADS, nq),
        in_specs=[pl.BlockSpec((4, HEAD_DIM), lambda b, h, i: (0, 0)),
                  pl.BlockSpec((tq, DIFF_VDIM), lambda b, h, i: (b * nq + i, h)),
                  pl.BlockSpec((T, DIFF_VDIM), lambda b, h, i: (b, h)),
                  pl.BlockSpec((T, DIFF_VDIM), lambda b, h, i: (b, h)),
                  pl.BlockSpec((DIFF_VDIM, 1), lambda b, h, i: (0, 0))],
        out_specs=pl.BlockSpec((tq, DIFF_VDIM), lambda b, h, i: (b * nq + i, h)),
        out_shape=jax.ShapeDtypeStruct((B * T, DIFF_WIDTH), F32),
        scratch_shapes=[pltpu.VMEM((T, DIFF_VDIM), BF16), pltpu.VMEM((T // tk, DIFF_VDIM + SUM_ROWS, tk), BF16),
                        pltpu.VMEM((1, 2 * tq), F32), pltpu.VMEM((DIFF_VDIM + SUM_ROWS, 2 * tq), F32),
                        pltpu.VMEM((tk, 2 * tq), F32), pltpu.VMEM((tk, 2 * tq), F32)],
        compiler_params=_cparams(("arbitrary", "arbitrary", "arbitrary")),
        name="diff_prompt",
    )(lam4, dq, dk, dv, subln_g.reshape(DIFF_VDIM, 1))


FF_CHUNK = 1408


def _rms(x, g):
    return x * lax.rsqrt(jnp.mean(x * x, axis=-1, keepdims=True) + RMS_EPS) * g


def _ffn_seq_kernel(x_ref, on_ref, od_ref, won_ref, wod_ref, g2_ref, wup_ref, cw_ref, cb_ref, wdn_ref,
                    gf_ref, y_ref, conv_ref, carry_ref, *, tm, tiles_per_seq):
    i = pl.program_id(0)
    first = (i % tiles_per_seq) == 0
    x1 = (x_ref[...] + jnp.dot(on_ref[...].astype(BF16), won_ref[...], preferred_element_type=F32)
          + jnp.dot(od_ref[...].astype(BF16), wod_ref[...], preferred_element_type=F32))
    h = _rms(x1, g2_ref[...]).astype(BF16)
    rowi = lax.broadcasted_iota(jnp.int32, (tm, 1), 0)
    f = jnp.zeros((tm, D_MODEL), F32)
    for k in range(D_FF // FF_CHUNK):
        halves = []
        for part in range(2):
            c0 = part * D_FF + k * FF_CHUNK
            cs = slice(c0, c0 + FF_CHUNK)
            u = jnp.dot(h, wup_ref[:, cs], preferred_element_type=F32)
            pm2 = jnp.where(first, 0.0, carry_ref[6:7, cs])
            pm1 = jnp.where(first, 0.0, carry_ref[7:8, cs])
            u1 = jnp.where(rowi == 0, pm1, pltpu.roll(u, 1, 0))
            u2 = jnp.where(rowi == 0, pm2, jnp.where(rowi == 1, pm1, pltpu.roll(u, 2, 0)))
            halves.append(cb_ref[:, cs] + u2 * cw_ref[0:1, cs] + u1 * cw_ref[1:2, cs] + u * cw_ref[2:3, cs])
            carry_ref[6:8, cs] = u[tm - 2:tm]
            conv_ref[0, :, cs] = u[tm - 2:tm]
        act = (jax.nn.silu(halves[0]) * halves[1]).astype(BF16)
        f = f + jnp.dot(act, wdn_ref[k * FF_CHUNK:(k + 1) * FF_CHUNK, :], preferred_element_type=F32)
    y_ref[...] = _rms(x1 + f, gf_ref[...])


def _ffn_step_kernel(x_ref, on_ref, od_ref, won_ref, wod_ref, g2_ref, wup_ref, cw_ref, cb_ref, wdn_ref,
                     gf_ref, p0_ref, p1_ref, y_ref, u_ref):
    x1 = (x_ref[...] + jnp.dot(on_ref[...].astype(BF16), won_ref[...], preferred_element_type=F32)
          + jnp.dot(od_ref[...].astype(BF16), wod_ref[...], preferred_element_type=F32))
    h = _rms(x1, g2_ref[...]).astype(BF16)
    f = jnp.zeros(x1.shape, F32)
    for k in range(D_FF // FF_CHUNK):
        halves = []
        for part in range(2):
            c0 = part * D_FF + k * FF_CHUNK
            cs = slice(c0, c0 + FF_CHUNK)
            u = jnp.dot(h, wup_ref[:, cs], preferred_element_type=F32)
            u_ref[:, cs] = u
            halves.append(cb_ref[:, cs] + p0_ref[:, cs] * cw_ref[0:1, cs] + p1_ref[:, cs] * cw_ref[1:2, cs]
                          + u * cw_ref[2:3, cs])
        act = (jax.nn.silu(halves[0]) * halves[1]).astype(BF16)
        f = f + jnp.dot(act, wdn_ref[k * FF_CHUNK:(k + 1) * FF_CHUNK, :], preferred_element_type=F32)
    y_ref[...] = _rms(x1 + f, gf_ref[...])


def _const_spec(a, ngrid):
    return pl.BlockSpec(a.shape, lambda *idx: (0,) * a.ndim, pipeline_mode=pl.Buffered(1))


def _ffn_weights_specs(ws):
    return [_const_spec(a, 1) for a in ws]


def _ffn_seq(x2d, o_nsa, o_diff, ws, B, T, tm):
    M = x2d.shape[0]
    tps = T // tm
    row = lambda w: pl.BlockSpec((tm, w), lambda i: (i, 0))
    kern = functools.partial(_ffn_seq_kernel, tm=tm, tiles_per_seq=tps)
    return pl.pallas_call(
        kern,
        grid=(M // tm,),
        in_specs=[row(D_MODEL), row(NSA_WIDTH), row(DIFF_WIDTH)] + _ffn_weights_specs(ws),
        out_specs=[row(D_MODEL), pl.BlockSpec((1, CONV_W - 1, 2 * D_FF), lambda i: (i // tps, 0, 0))],
        out_shape=[jax.ShapeDtypeStruct((M, D_MODEL), F32),
                   jax.ShapeDtypeStruct((B, CONV_W - 1, 2 * D_FF), F32)],
        scratch_shapes=[pltpu.VMEM((8, 2 * D_FF), F32)],
        compiler_params=_cparams(("arbitrary",)),
        name="ffn_prompt",
    )(x2d, o_nsa, o_diff, *ws)


def _ffn_step(x2d, o_nsa, o_diff, ws, p0, p1):
    M = x2d.shape[0]
    full = lambda a: pl.BlockSpec(a.shape, lambda i: (0,) * a.ndim)
    ins = (x2d, o_nsa, o_diff) + tuple(ws) + (p0, p1)
    return pl.pallas_call(
        _ffn_step_kernel,
        grid=(1,),
        in_specs=[full(a) for a in ins],
        out_specs=[pl.BlockSpec((M, D_MODEL), lambda i: (0, 0)), pl.BlockSpec((M, 2 * D_FF), lambda i: (0, 0))],
        out_shape=[jax.ShapeDtypeStruct((M, D_MODEL), F32), jax.ShapeDtypeStruct((M, 2 * D_FF), F32)],
        compiler_params=_cparams(("arbitrary",)),
        name="ffn_sample",
    )(*ins)


ROWS_PER_PAGE = PAGE_SIZE // CMP_STRIDE


def _compress_sample_kernel(pt_ref, pool_ref, tail_ref, w1cat_ref, posb_ref, w1n_ref, w2bd_ref, o_ref,
                            pbuf, xs, r_ref, sem, *, n_pages, nb, rows):
    b = pl.program_id(0)
    past = n_pages * PAGE_SIZE

    def page_copy(bb, n, slot):
        return pltpu.make_async_copy(pool_ref.at[pt_ref[bb * n_pages + n]], pbuf.at[slot, n], sem.at[slot])

    def start_all(bb, slot):
        def body(n, _):
            page_copy(bb, n, slot).start()
            return 0
        lax.fori_loop(0, n_pages, body, 0)

    @pl.when(b == 0)
    def _():
        start_all(0, 0)

    @pl.when(b + 1 < nb)
    def _():
        start_all(b + 1, (b + 1) % 2)

    slot = b % 2

    pltpu.make_async_copy(pbuf.at[slot], pbuf.at[slot], sem.at[slot]).wait()

    xs[past:past + 8, :] = tail_ref[0]
    xs[past + 8:, :] = jnp.zeros((xs.shape[0] - past - 8, KV_W), F32)

    def hidden_rows(r0, n):
        acc = None
        for s in range(0, CMP_STRIDE, 2):
            lhs = jnp.concatenate([xs[pl.ds(r0 * CMP_STRIDE + s, n, stride=CMP_STRIDE), :],
                                   xs[pl.ds(r0 * CMP_STRIDE + s + 1, n, stride=CMP_STRIDE), :]],
                                  axis=1).astype(BF16)
            part = jnp.dot(lhs, w1cat_ref[s * KV_W:(s + 2) * KV_W, :], preferred_element_type=F32)
            acc = part if acc is None else acc + part
        r_ref[r0:r0 + n, :] = acc

    n_chunks = 4
    pages_pc = n_pages // n_chunks
    rows_pc = pages_pc * ROWS_PER_PAGE
    for c in range(n_chunks):
        for n in range(c * pages_pc, (c + 1) * pages_pc):
            xs[n * PAGE_SIZE:(n + 1) * PAGE_SIZE, :] = pbuf[slot, n].T
        hidden_rows(c * rows_pc, rows_pc)
    hidden_rows(n_chunks * rows_pc, rows - n_chunks * rows_pc)
    o_ref[0, 0:rows, :] = _compress_finish(r_ref[...], posb_ref[...], w1n_ref[...], w2bd_ref[...])
    o_ref[0, rows:, :] = jnp.zeros((o_ref.shape[1] - rows, KV_W), F32)


def _compress_sample(pool, pt_flat, new_row, cw, nb, n_pages, rows, rows_out):
    pool_t = _pages_transposed(pool)
    tail = jnp.zeros((nb, 8, KV_W), F32).at[:, 0, :].set(new_row)
    kern = functools.partial(_compress_sample_kernel, n_pages=n_pages, nb=nb, rows=rows)
    full = lambda a: pl.BlockSpec(a.shape, lambda b, pt: (0,) * a.ndim)
    return pl.pallas_call(
        kern,
        grid_spec=pltpu.PrefetchScalarGridSpec(
            num_scalar_prefetch=1,
            grid=(nb,),
            in_specs=[pl.BlockSpec(memory_space=pl.ANY),
                      pl.BlockSpec((1, 8, KV_W), lambda b, pt: (b, 0, 0))] + [full(a) for a in cw],
            out_specs=pl.BlockSpec((1, rows_out, KV_W), lambda b, pt: (b, 0, 0)),
            scratch_shapes=[pltpu.VMEM((2, n_pages, KV_W, PAGE_SIZE), F32),
                            pltpu.VMEM((rows * CMP_STRIDE, KV_W), F32),
                            pltpu.VMEM((rows, 4 * CMP_HIDDEN), F32), pltpu.SemaphoreType.DMA((2,))]),
        out_shape=jax.ShapeDtypeStruct((nb, rows_out, KV_W), F32),
        compiler_params=_cparams(("arbitrary",)),
        name="compress_sample",
    )(pt_flat, pool_t, tail, *cw)


def _pages_transposed(pool):
    n_pool = pool.shape[0]
    return jnp.transpose(pool, (0, 2, 3, 1)).reshape(n_pool, KV_W, pool.shape[1])


def _pad_rows8(x):
    return jnp.concatenate([x, jnp.zeros((8 - x.shape[0], x.shape[1]), x.dtype)], axis=0)


def _nsa_sample_a_kernel(q_ref, kc_ref, vc_ref, ovbd_ref, oc_ref, info_ref, *, n_cmp, n_sel, q_pos):
    rk = kc_ref.shape[1]
    n_pad = ovbd_ref.shape[1] // NSA_GROUPS
    q8 = _pad_rows8(q_ref[0]).astype(BF16)
    kcbd = _block_diag_rows(kc_ref[0], BF16)
    vcbd = _block_diag_rows(vc_ref[0], BF16)
    s = _dot_nt(q8, kcbd)
    kidx = lax.broadcasted_iota(jnp.int32, s.shape, 1) % rk
    cmask = (kidx * CMP_STRIDE + (CMP_BLOCK - 1) <= q_pos) & (kidx < n_cmp)
    s = jnp.where(cmask, s * ATTN_SCALE, NEG_INF)
    pcs = []
    for g in range(NSA_GROUPS):
        sg = s[:, g * rk:(g + 1) * rk]
        m = jnp.max(sg, axis=-1, keepdims=True)
        m = jnp.where(m == NEG_INF, 0.0, m)
        p = jnp.exp(sg - m)
        pcs.append(p / jnp.maximum(jnp.sum(p, axis=-1, keepdims=True), 1e-30))
    pc = jnp.concatenate(pcs, axis=1)
    oc_ref[0] = jnp.dot(pc.astype(BF16), vcbd, preferred_element_type=F32)
    imp = jnp.sum(pc[0:NSA_REP], axis=0, keepdims=True)
    imp8 = jnp.concatenate([imp] * 8, axis=0)
    ih, il = _split2(imp8)
    score = (jnp.dot(ih, ovbd_ref[...], preferred_element_type=F32)
             + jnp.dot(il, ovbd_ref[...], preferred_element_type=F32))[0:1]

    blk_r = lax.broadcasted_iota(jnp.int32, (1, n_pad), 1)
    blk_c = lax.broadcasted_iota(jnp.int32, (n_pad, 1), 0)
    cur = q_pos // SEL_BLOCK
    forced_r = (blk_r == 0) | (blk_r == cur) | (blk_r == cur - 1)
    ok_r = (blk_r * SEL_BLOCK <= q_pos) & (blk_r < n_sel)
    ok_c = (blk_c * SEL_BLOCK <= q_pos) & (blk_c < n_sel)
    ii = lax.broadcasted_iota(jnp.int32, (n_pad, n_pad), 1)
    jj = lax.broadcasted_iota(jnp.int32, (n_pad, n_pad), 0)
    kk = lax.broadcasted_iota(jnp.int32, (n_pad, LANES), 1)
    jf = lax.broadcasted_iota(jnp.int32, (n_pad, LANES), 0).astype(F32)
    rows_out = []
    for g in range(NSA_GROUPS):
        sr = score[:, g * n_pad:(g + 1) * n_pad]
        sr = jnp.where(ok_r, sr + jnp.where(forced_r, FORCE_BONUS, 0.0), NEG_INF)
        sc = jnp.sum(jnp.where(ii == jj, sr, 0.0), axis=1, keepdims=True)
        ahead = jnp.where(ii < jj, jnp.where(sr >= sc, 1.0, 0.0), jnp.where(sr > sc, 1.0, 0.0))
        cnt = jnp.sum(ahead, axis=1, keepdims=True)
        chosen = (cnt < float(SEL_TOPN)) & ok_c
        hit = chosen & (cnt == kk.astype(F32))
        rows_out.append((jnp.sum(jnp.where(hit, jf, 0.0), axis=0, keepdims=True),
                         jnp.sum(jnp.where(hit, 1.0, 0.0), axis=0, keepdims=True)))
    info = jnp.concatenate([rows_out[0][0], rows_out[1][0], rows_out[0][1], rows_out[1][1],
                            jnp.zeros((4, LANES), F32)], axis=0)
    info_ref[0] = info.astype(jnp.int32)


def _nsa_sample_a(qraw3, kc, vc, q_pos, n_cmp, n_sel):
    nb, rk, _ = kc.shape
    n_pad = -(-n_sel // LANES) * LANES
    ovbd = _overlap_matrix(rk, n_pad)
    kern = functools.partial(_nsa_sample_a_kernel, n_cmp=n_cmp, n_sel=n_sel, q_pos=q_pos)
    return pl.pallas_call(
        kern,
        grid=(nb,),
        in_specs=[pl.BlockSpec((1, NSA_REP, LANES), lambda b: (b, 0, 0)),
                  pl.BlockSpec((1, rk, KV_W), lambda b: (b, 0, 0)),
                  pl.BlockSpec((1, rk, KV_W), lambda b: (b, 0, 0)),
                  pl.BlockSpec(ovbd.shape, lambda b: (0, 0))],
        out_specs=[pl.BlockSpec((1, 8, LANES), lambda b: (b, 0, 0)),
                   pl.BlockSpec((1, 8, LANES), lambda b: (b, 0, 0))],
        out_shape=[jax.ShapeDtypeStruct((nb, 8, LANES), F32), jax.ShapeDtypeStruct((nb, 8, LANES), jnp.int32)],
        compiler_params=_cparams(("arbitrary",)),
        name="nsa_sample_select",
    )(qraw3, kc, vc, ovbd)


N_SLOTS = NSA_GROUPS * SEL_TOPN


def _decode_attend(q8f, kts, vts, masks, s_new, inc_new, v_new):
    lo, hi = _lane_group_masks((8, KV_W))
    outs = []
    for g, lanes in enumerate((lo, hi)):
        qg = jnp.where(lanes, q8f, 0.0).astype(BF16)
        sg = jnp.dot(qg, kts[g], preferred_element_type=F32) * ATTN_SCALE
        sg = jnp.where(masks[g], sg, NEG_INF)
        sn = jnp.where(inc_new[g] > 0.5, s_new[g], NEG_INF)
        m = jnp.maximum(jnp.max(sg, axis=-1, keepdims=True), sn)
        m = jnp.where(m == NEG_INF, 0.0, m)
        p = jnp.exp(sg - m)
        pnew = jnp.exp(sn - m)
        l = jnp.sum(p, axis=-1, keepdims=True) + pnew
        og = _dot_nt(p.astype(BF16), vts[g]) + pnew * v_new
        outs.append(og / jnp.maximum(l, 1e-30))
    return jnp.where(lo, outs[0], outs[1])


def _nsa_sample_b_kernel(sel_ref, pt_ref, q_ref, gate_ref, oc_ref, info_ref, psk_ref, psv_ref,
                         sknew_ref, svnew_ref, wink_ref, winv_ref, wknew_ref, wvnew_ref, eexp_ref, e2_ref,
                         o_ref, swk_ref, swv_ref, kbuf, vbuf, sem, *, nb, n_pages, n_past_blk, wbuf):
    b = pl.program_id(0)
    blk_per_page = PAGE_SIZE // SEL_BLOCK

    def copies(bb, idx, slot):
        j = jnp.minimum(sel_ref[bb * N_SLOTS + idx], n_past_blk - 1)
        page = pt_ref[bb * n_pages + j // blk_per_page]
        return (pltpu.make_async_copy(psk_ref.at[page], kbuf.at[slot, idx], sem.at[slot]),
                pltpu.make_async_copy(psv_ref.at[page], vbuf.at[slot, idx], sem.at[slot]))

    def start_all(bb, slot):
        def body(idx, _):
            ck, cv = copies(bb, idx, slot)
            ck.start()
            cv.start()
            return 0
        lax.fori_loop(0, N_SLOTS, body, 0)

    @pl.when(b == 0)
    def _():
        start_all(0, 0)

    @pl.when(b + 1 < nb)
    def _():
        start_all(b + 1, (b + 1) % 2)

    slot = b % 2

    pltpu.make_async_copy(kbuf.at[slot], kbuf.at[slot], sem.at[slot]).wait()
    pltpu.make_async_copy(vbuf.at[slot], vbuf.at[slot], sem.at[slot]).wait()

    q8f = _pad_rows8(q_ref[0])
    lo, hi = _lane_group_masks((8, KV_W))

    def new_scores(k_new):
        prod = q8f * k_new
        return [jnp.sum(jnp.where(lo, prod, 0.0), axis=-1, keepdims=True) * ATTN_SCALE,
                jnp.sum(jnp.where(hi, prod, 0.0), axis=-1, keepdims=True) * ATTN_SCALE]

    info = info_ref[0].astype(F32)
    lane1 = lax.broadcasted_iota(jnp.int32, (1, LANES), 1)
    n_keys = SEL_TOPN * PAGE_SIZE
    tok_half = (lax.broadcasted_iota(jnp.int32, (8, n_keys), 1) % PAGE_SIZE) // SEL_BLOCK
    kts, vts, masks, inc = [], [], [], []
    for g in range(NSA_GROUPS):
        kts.append(jnp.concatenate([kbuf[slot, g * SEL_TOPN + k] for k in range(SEL_TOPN)], axis=1).astype(BF16))
        vts.append(jnp.concatenate([vbuf[slot, g * SEL_TOPN + k] for k in range(SEL_TOPN)], axis=1).astype(BF16))
        idx8 = jnp.concatenate([info[g:g + 1]] * 8, axis=0).astype(BF16)
        val8 = jnp.concatenate([info[2 + g:3 + g]] * 8, axis=0).astype(BF16)
        jl = jnp.dot(idx8, eexp_ref[...], preferred_element_type=F32)
        vl = jnp.dot(val8, eexp_ref[...], preferred_element_type=F32)
        masks.append((vl > 0.5) & (jl < n_past_blk - 0.5)
                     & (jl.astype(jnp.int32) % blk_per_page == tok_half))
        is_new = (info[g:g + 1] > n_past_blk - 0.5) & (info[2 + g:3 + g] > 0.5) & (lane1 < SEL_TOPN)
        inc.append(jnp.sum(jnp.where(is_new, 1.0, 0.0), axis=-1, keepdims=True))
    o_s = _decode_attend(q8f, kts, vts, masks, new_scores(sknew_ref[0]), inc, svnew_ref[0])

    wk_old = wink_ref[0]
    wv_old = winv_ref[0]
    widx = lax.broadcasted_iota(jnp.int32, (1, wbuf), 1)
    wmask = widx > wbuf - WINDOW
    always = [jnp.ones((1, 1), F32)] * NSA_GROUPS
    o_w = _decode_attend(q8f, [wk_old.astype(BF16)] * 2, [wv_old.astype(BF16)] * 2, [wmask] * 2,
                         new_scores(wknew_ref[0]), always, wvnew_ref[0])
    eye = (lax.broadcasted_iota(jnp.int32, (KV_W, KV_W), 0) == lax.broadcasted_iota(jnp.int32, (KV_W, KV_W), 1))
    for new_ref, old, out_ref in ((wknew_ref, wk_old, swk_ref), (wvnew_ref, wv_old, swv_ref)):
        col = jnp.sum(jnp.where(eye, new_ref[0], 0.0), axis=1, keepdims=True)
        out_ref[0] = jnp.where(widx == wbuf - 1, col, pltpu.roll(old, wbuf - 1, 1))

    gate8 = jnp.concatenate([gate_ref[0]] * 8, axis=0)
    rr = lax.broadcasted_iota(jnp.int32, (8, LANES), 0)
    ll = lax.broadcasted_iota(jnp.int32, (8, LANES), 1)
    o = jnp.zeros((8, KV_W), F32)
    for c, ob in enumerate((oc_ref[0], o_s, o_w)):
        base = c * NSA_HEADS + rr * NSA_GROUPS
        gsel = jnp.where((ll >= base) & (ll < base + NSA_GROUPS), gate8, 0.0)
        gh, gl = _split2(gsel)
        gx = (jnp.dot(gh, e2_ref[...], preferred_element_type=F32)
              + jnp.dot(gl, e2_ref[...], preferred_element_type=F32))
        o = o + gx * ob
    o_ref[0] = o[0:NSA_REP]


def _nsa_sample_b(sel_flat, pt_flat, qrot3, gate3, oc, info, pool_sk, pool_sv, sknew, svnew,
                  wink, winv, wknew, wvnew, nb, n_pages):
    wbuf = wink.shape[2]
    n_past_blk = n_pages * (PAGE_SIZE // SEL_BLOCK)
    e = np.zeros((LANES, SEL_TOPN * PAGE_SIZE), np.float32)
    for k in range(SEL_TOPN):
        e[k, k * PAGE_SIZE:(k + 1) * PAGE_SIZE] = 1.0
    eexp = jnp.asarray(e, BF16)
    e2 = np.zeros((LANES, LANES), np.float32)
    for j in range(LANES):
        e2[j, (j % NSA_GROUPS) * HEAD_DIM:(j % NSA_GROUPS + 1) * HEAD_DIM] = 1.0
    e2 = jnp.asarray(e2, BF16)
    per_b = lambda r, w: pl.BlockSpec((1, r, w), lambda b, s, p: (b, 0, 0))
    const = lambda a: pl.BlockSpec(a.shape, lambda b, s, p: (0, 0))
    anyspec = pl.BlockSpec(memory_space=pl.ANY)
    kern = functools.partial(_nsa_sample_b_kernel, nb=nb, n_pages=n_pages, n_past_blk=n_past_blk, wbuf=wbuf)
    r3 = lambda a: a.reshape(nb, 1, KV_W)
    return pl.pallas_call(
        kern,
        grid_spec=pltpu.PrefetchScalarGridSpec(
            num_scalar_prefetch=2,
            grid=(nb,),
            in_specs=[per_b(NSA_REP, LANES), per_b(1, LANES), per_b(8, LANES), per_b(8, LANES),
                      anyspec, anyspec, per_b(1, KV_W), per_b(1, KV_W),
                      per_b(KV_W, wbuf), per_b(KV_W, wbuf), per_b(1, KV_W), per_b(1, KV_W),
                      const(eexp), const(e2)],
            out_specs=[per_b(NSA_REP, LANES), per_b(KV_W, wbuf), per_b(KV_W, wbuf)],
            scratch_shapes=[pltpu.VMEM((2, N_SLOTS, KV_W, PAGE_SIZE), F32),
                            pltpu.VMEM((2, N_SLOTS, KV_W, PAGE_SIZE), F32),
                            pltpu.SemaphoreType.DMA((2,))]),
        out_shape=[jax.ShapeDtypeStruct((nb, NSA_REP, LANES), F32),
                   jax.ShapeDtypeStruct((nb, KV_W, wbuf), F32),
                   jax.ShapeDtypeStruct((nb, KV_W, wbuf), F32)],
        compiler_params=_cparams(("arbitrary",)),
        name="nsa_sample_attend",
    )(sel_flat, pt_flat, qrot3, gate3, oc, info, _pages_transposed(pool_sk), _pages_transposed(pool_sv),
      r3(sknew), r3(svnew), wink, winv, r3(wknew), r3(wvnew), eexp, e2)


DIFF_PAGES_PER_STEP = 8


def _diff_sample_kernel(pt_ref, lam_ref, dq_ref, dknew_ref, dvnew_ref, g_ref, pk_ref, pv_ref, o_ref,
                        kbuf, vbuf, sem, m_ref, l_ref, acc_ref, *, nb, n_pages, lam_init):
    b = pl.program_id(0)
    c = pl.program_id(1)
    nc = n_pages // DIFF_PAGES_PER_STEP
    step = b * nc + c

    def copies(st, p, slot):
        bb = st // nc
        cc = st % nc
        page = pt_ref[bb * n_pages + cc * DIFF_PAGES_PER_STEP + p]
        dst = pl.ds(p * PAGE_SIZE * DIFF_HEADS, PAGE_SIZE * DIFF_HEADS)
        return (pltpu.make_async_copy(pk_ref.at[page], kbuf.at[slot, p], sem.at[slot]),
                pltpu.make_async_copy(pv_ref.at[page], vbuf.at[slot, dst, :], sem.at[slot]))

    def start_all(st, slot):
        for p in range(DIFF_PAGES_PER_STEP):
            ck, cv = copies(st, p, slot)
            ck.start()
            cv.start()

    @pl.when(step == 0)
    def _():
        start_all(0, 0)

    @pl.when(step + 1 < nb * nc)
    def _():
        start_all(step + 1, (step + 1) % 2)

    slot = step % 2
    for p in range(DIFF_PAGES_PER_STEP):
        ck, cv = copies(step, p, slot)
        ck.wait()
        cv.wait()

    @pl.when(c == 0)
    def _():
        m_ref[...] = jnp.full(m_ref.shape, NEG_INF, F32)
        l_ref[...] = jnp.zeros(l_ref.shape, F32)
        acc_ref[...] = jnp.zeros(acc_ref.shape, F32)

    rr = lax.broadcasted_iota(jnp.int32, (8, DIFF_WIDTH), 0)
    ll = lax.broadcasted_iota(jnp.int32, (8, DIFF_WIDTH), 1)
    own = (ll // HEAD_DIM) == rr
    q8f = jnp.where(own, jnp.concatenate([dq_ref[0]] * 8, axis=0), 0.0)
    q8 = q8f.astype(BF16)
    s = jnp.concatenate([jnp.dot(q8, kbuf[slot, p].astype(BF16), preferred_element_type=F32)
                         for p in range(DIFF_PAGES_PER_STEP)], axis=1) * ATTN_SCALE
    m_old = m_ref[...]
    m_new = jnp.maximum(m_old, jnp.max(s, axis=-1, keepdims=True))
    alpha = jnp.exp(m_old - m_new)
    p = jnp.exp(s - m_new)
    l_ref[...] = alpha * l_ref[...] + jnp.sum(p, axis=-1, keepdims=True)
    m_ref[...] = m_new
    pb = p.astype(BF16)
    keys = DIFF_PAGES_PER_STEP * PAGE_SIZE
    pv = jnp.concatenate([jnp.dot(pb, vbuf[slot, pl.ds(h, keys, stride=DIFF_HEADS), :].astype(BF16),
                                  preferred_element_type=F32) for h in range(DIFF_HEADS)], axis=1)
    acc_ref[...] = alpha * acc_ref[...] + pv

    @pl.when(c == nc - 1)
    def _():
        s_new = jnp.sum(q8f * dknew_ref[0], axis=-1, keepdims=True) * ATTN_SCALE
        m_old = m_ref[...]
        m_new = jnp.maximum(m_old, s_new)
        alpha = jnp.exp(m_old - m_new)
        p_new = jnp.exp(s_new - m_new)
        l = alpha * l_ref[...] + p_new
        o = (alpha * acc_ref[...] + p_new * dvnew_ref[0]) / l
        lam = _lambda_value(lam_ref, lam_init)
        for h in range(DIFF_HEADS):
            cs = slice(h * DIFF_VDIM, (h + 1) * DIFF_VDIM)
            o_ref[0, :, cs] = _diff_merge_rows(o[2 * h:2 * h + 1, cs], o[2 * h + 1:2 * h + 2, cs], lam,
                                               g_ref[...], lam_init)


def _diff_sample(pt_flat, lam4, dq, dknew, dvnew, subln_g, pool_k, pool_v, nb, n_pages, lam_init):
    n_pool = pool_k.shape[0]
    nc = n_pages // DIFF_PAGES_PER_STEP
    keys = DIFF_PAGES_PER_STEP * PAGE_SIZE
    per_b = pl.BlockSpec((1, 1, DIFF_WIDTH), lambda b, c, pt: (b, 0, 0))
    anyspec = pl.BlockSpec(memory_space=pl.ANY)
    kern = functools.partial(_diff_sample_kernel, nb=nb, n_pages=n_pages, lam_init=lam_init)
    r3 = lambda a: a.reshape(nb, 1, DIFF_WIDTH)
    return pl.pallas_call(
        kern,
        grid_spec=pltpu.PrefetchScalarGridSpec(
            num_scalar_prefetch=1,
            grid=(nb, nc),
            in_specs=[pl.BlockSpec((4, HEAD_DIM), lambda b, c, pt: (0, 0)), per_b, per_b, per_b,
                      pl.BlockSpec((1, DIFF_VDIM), lambda b, c, pt: (0, 0)), anyspec, anyspec],
            out_specs=per_b,
            scratch_shapes=[pltpu.VMEM((2, DIFF_PAGES_PER_STEP, DIFF_WIDTH, PAGE_SIZE), F32),
                            pltpu.VMEM((2, keys * DIFF_HEADS, DIFF_VDIM), F32),
                            pltpu.SemaphoreType.DMA((2,)),
                            pltpu.VMEM((8, 1), F32), pltpu.VMEM((8, 1), F32), pltpu.VMEM((8, DIFF_WIDTH), F32)]),
        out_shape=jax.ShapeDtypeStruct((nb, 1, DIFF_WIDTH), F32),
        compiler_params=_cparams(("arbitrary", "arbitrary")),
        name="diff_sample",
    )(pt_flat, lam4, r3(dq), r3(dknew), r3(dvnew), subln_g,
      jnp.transpose(pool_k, (0, 2, 3, 4, 1)).reshape(n_pool, DIFF_WIDTH, PAGE_SIZE),
      pool_v.reshape(n_pool, PAGE_SIZE * DIFF_HEADS, DIFF_VDIM))


def _prep_w_in(w):
    splits = (NSA_WIDTH, KV_W, KV_W, KV_W, KV_W, KV_W, KV_W, NSA_HEADS * 3, DIFF_WIDTH, DIFF_WIDTH, DIFF_WIDTH)
    cuts = [int(c) for c in np.cumsum(splits)[:-1]]
    q, ck, cv, sk, sv, wk, wv, gate, dq, dk, dv = jnp.split(w, cuts, axis=1)
    q = q.reshape(D_MODEL, NSA_GROUPS, NSA_REP, HEAD_DIM).transpose(0, 2, 1, 3).reshape(D_MODEL, NSA_WIDTH)
    gate = gate.reshape(D_MODEL, NSA_GROUPS, NSA_REP, 3).transpose(0, 3, 2, 1).reshape(D_MODEL, NSA_HEADS * 3)
    gate = jnp.pad(gate, ((0, 0), (0, LANES - NSA_HEADS * 3)))
    return jnp.concatenate([q, ck, cv, sk, sv, wk, wv, dq, dk, dv, gate], axis=1).astype(BF16)


def _prep_w_out(w):
    won = w[:NSA_WIDTH].reshape(NSA_GROUPS, NSA_REP, HEAD_DIM, D_MODEL).transpose(1, 0, 2, 3)
    return won.reshape(NSA_WIDTH, D_MODEL).astype(BF16), w[NSA_WIDTH:].astype(BF16)


def _prep_ffn(p):
    won, wod = _prep_w_out(p["w_out"][0])
    return (won, wod, p["ffn_norm"], p["w_up"][0].astype(BF16), p["conv_w"][0], p["conv_b"],
            p["w_down"][0].astype(BF16), p["final_norm"].reshape(1, D_MODEL))


def _rope_tables(pos):
    half = HEAD_DIM // 2
    inv = 1.0 / (ROPE_THETA ** (jnp.arange(half, dtype=F32) / half))
    ang = pos.astype(F32)[:, None] * inv[None, :]
    cos, sin = jnp.cos(ang), jnp.sin(ang)
    return jnp.tile(cos, (1, 4)), jnp.tile(jnp.concatenate([-sin, sin], axis=1), (1, 2))


def _prompt_group(xp, attn_norm, w_in_r, cw_k, cw_v, lam4, subln_g, ffn_ws, lam_init):
    B, T, _ = xp.shape
    x2d = xp.reshape(B * T, D_MODEL)
    tm = min(256, T)
    cos_t, sin_t = _rope_tables(jnp.arange(T, dtype=jnp.int32))
    (qraw, qrot, ck, cv, sk, sv, wk, wv, gate, dq, dk, dv,
     ck_t, cv_t, sk_t, sv_t, wk_t, wv_t, dk_t) = _project(x2d, attn_norm, w_in_r, cos_t, sin_t, tm,
                                                          transposed_batch=B)
    kc = _compress_prompt(ck.reshape(B, T, KV_W), cw_k)
    vc = _compress_prompt(cv.reshape(B, T, KV_W), cw_v)
    o_nsa = _nsa_prompt(qraw, qrot, gate, kc, vc, sk, sv, wk, wv, B, T, tq=256, tk=256)
    o_diff = _diff_prompt(lam4, dq, dk, dv, subln_g, B, T, tq=min(512, T), tk=512, lam_init=lam_init)
    y, conv = _ffn_seq(x2d, o_nsa, o_diff, ffn_ws, B, T, tm)
    kv = lambda a: jnp.transpose(a.reshape(B, NSA_GROUPS, HEAD_DIM, a.shape[2]), (0, 3, 1, 2))[None]
    wb = min(WINDOW, T)
    dk_out = jnp.transpose(dk_t.reshape(B, DIFF_HEADS, 2, HEAD_DIM, T), (0, 4, 1, 2, 3))[None]
    return (y.reshape(B, T, D_MODEL), kv(ck_t), kv(cv_t), kv(sk_t), kv(sv_t),
            dk_out, dv.reshape(1, B, T, DIFF_HEADS, DIFF_VDIM),
            kv(wk_t[:, :, T - wb:]), kv(wv_t[:, :, T - wb:]), conv[None])


def _sample_group(xs, caches, page_table, attn_norm, w_in_r, cw_k, cw_v, lam4, subln_g, ffn_ws, lam_init):
    pool_ck, pool_cv, pool_sk, pool_sv, pool_dk, pool_dv, win_k, win_v, conv_state = caches
    nb = xs.shape[0]
    n_pages = page_table.shape[1]
    past = n_pages * PAGE_SIZE
    t_pad = -(-(past + 1) // SEL_BLOCK) * SEL_BLOCK
    n_cmp = t_pad // CMP_STRIDE - CMP_BLOCK // CMP_STRIDE + 1
    n_sel = t_pad // SEL_BLOCK
    rows = -(-(t_pad // CMP_STRIDE) // 8) * 8
    rows_out = -(-rows // LANES) * LANES
    pt_flat = page_table.reshape(-1)
    x2d = xs.reshape(nb, D_MODEL)
    pos = jnp.full((nb,), past, dtype=jnp.int32)
    cos_t, sin_t = _rope_tables(pos)
    qraw, qrot, ck, cv, sk, sv, wk, wv, gate, dq, dk, dv = _project(x2d, attn_norm, w_in_r, cos_t, sin_t, nb)
    kc = _compress_sample(pool_ck, pt_flat, ck, cw_k, nb, n_pages, rows, rows_out)
    vc = _compress_sample(pool_cv, pt_flat, cv, cw_v, nb, n_pages, rows, rows_out)
    o_c, info = _nsa_sample_a(qraw.reshape(nb, NSA_REP, LANES), kc, vc, past, n_cmp, n_sel)
    sel_flat = info[:, 0:NSA_GROUPS, 0:SEL_TOPN].reshape(-1)
    wbuf = win_k.shape[1]
    o_nsa, swk, swv = _nsa_sample_b(sel_flat, pt_flat, qrot.reshape(nb, NSA_REP, LANES),
                                    gate.reshape(nb, 1, LANES), o_c, info, pool_sk, pool_sv, sk, sv,
                                    _pages_transposed(win_k), _pages_transposed(win_v), wk, wv, nb, n_pages)
    from_t = lambda a: jnp.transpose(a.reshape(nb, NSA_GROUPS, HEAD_DIM, wbuf), (0, 3, 1, 2))[None]
    o_diff = _diff_sample(pt_flat, lam4, dq, dk, dv, subln_g, pool_dk, pool_dv, nb, n_pages, lam_init)
    y, u = _ffn_step(x2d, o_nsa.reshape(nb, NSA_WIDTH), o_diff.reshape(nb, DIFF_WIDTH), ffn_ws,
                     conv_state[:, 0], conv_state[:, 1])
    kv = lambda a: a.reshape(1, nb, 1, NSA_GROUPS, HEAD_DIM)
    return (y.reshape(nb, 1, D_MODEL), kv(ck), kv(cv), kv(sk), kv(sv),
            dk.reshape(1, nb, 1, DIFF_HEADS, 2, HEAD_DIM), dv.reshape(1, nb, 1, DIFF_HEADS, DIFF_VDIM),
            from_t(swk), from_t(swv),
            jnp.stack([conv_state[:, 1], u], axis=1)[None])


def kernel(x_prompt, x_sample, cache_cmp_k, cache_cmp_v, cache_sel_k, cache_sel_v, cache_diff_k, cache_diff_v,
           cache_win_k, cache_win_v, state_ffn_conv, page_table, attn_norm, w_in, cmp_pos_k, cmp_w1_k, cmp_w2_k,
           cmp_pos_v, cmp_w1_v, cmp_w2_v, lambda_q1, lambda_k1, lambda_q2, lambda_k2, subln_g, w_out, ffn_norm,
           w_up, conv_w, conv_b, w_down, final_norm):
    assert w_in.shape[0] == 1 and x_sample.shape[1] == 1, "one layer, one new token per sequence"
    lam_init = 0.8 - 0.6 * math.exp(0.0)
    w_in_r = _prep_w_in(w_in[0])
    cw_k = _compress_weights(cmp_pos_k[0], cmp_w1_k[0], cmp_w2_k[0])
    cw_v = _compress_weights(cmp_pos_v[0], cmp_w1_v[0], cmp_w2_v[0])
    lam4 = jnp.stack([lambda_q1[0], lambda_k1[0], lambda_q2[0], lambda_k2[0]])
    ffn_ws = _prep_ffn(dict(w_out=w_out, ffn_norm=ffn_norm, w_up=w_up, conv_w=conv_w, conv_b=conv_b,
                            w_down=w_down, final_norm=final_norm))
    p = _prompt_group(x_prompt, attn_norm, w_in_r, cw_k, cw_v, lam4, subln_g, ffn_ws, lam_init)
    caches = (cache_cmp_k[0], cache_cmp_v[0], cache_sel_k[0], cache_sel_v[0], cache_diff_k[0], cache_diff_v[0],
              cache_win_k[0], cache_win_v[0], state_ffn_conv[0])
    s = _sample_group(x_sample, caches, page_table, attn_norm, w_in_r, cw_k, cw_v, lam4, subln_g, ffn_ws,
                      lam_init)
    return (p[0], s[0]) + tuple(p[1:]) + tuple(s[1:])
```
